```python
import jax, jax.numpy as jnp
from jax import lax
import numpy as np

D_MODEL = 1024
BATCH = 2
SEQ = 8192
DEPTH = 2

POOL_WINDOWS = (2, 4, 8, 16)
POOL_GROUPS = 4
POOL_CH = D_MODEL // 8
POOL_WIDTH = POOL_GROUPS * POOL_CH
CONV_WIDTH = D_MODEL // 2
CONV_K = 31
SGU_CHUNK = 128
SGU_HEADS = 4
SGU_HEAD_DIM = D_MODEL // 8
SGU_WIDTH = SGU_HEADS * SGU_HEAD_DIM
N_BRANCH = 3
BRANCH_WIDTH = 512
IN_WIDTH = POOL_WIDTH + 2 * CONV_WIDTH + 2 * SGU_WIDTH + N_BRANCH * D_MODEL
N_EXPERTS = 32
TOP_K = 4
D_FF = D_MODEL
SWIGLU_LIMIT = 7.0
SWIGLU_ALPHA = 1.702
MOE_BLOCK = 256
PLE_DIM = 256
EPS = 1e-6

kernel_name = "hybrid_pool_conv_sgu_moe_block"


def rms_norm(x, g):
    xf = x.astype(jnp.float32)
    y = xf * lax.rsqrt(jnp.mean(xf * xf, axis=-1, keepdims=True) + EPS)
    return (y * g.astype(jnp.float32)).astype(x.dtype)


def layer_norm(x, g, b):
    xf = x.astype(jnp.float32)
    mu = jnp.mean(xf, axis=-1, keepdims=True)
    var = jnp.mean(jnp.square(xf - mu), axis=-1, keepdims=True)
    y = (xf - mu) * lax.rsqrt(var + EPS)
    return (y * g.astype(jnp.float32) + b.astype(jnp.float32)).astype(x.dtype)


def causal_multiscale_pool(z, pool_w, pool_scale):
    B, S, _ = z.shape
    zf = z.astype(jnp.float32).reshape(B, S, POOL_GROUPS, POOL_CH)
    cs = jnp.cumsum(zf, axis=1)
    t = jnp.arange(S)
    outs = []
    for g, w in enumerate(POOL_WINDOWS):
        c = cs[:, :, g]
        prev = jnp.pad(c, ((0, 0), (w, 0), (0, 0)))[:, :S]
        count = jnp.minimum(t + 1, w).astype(jnp.float32)[None, :, None]
        outs.append((c - prev) / count - zf[:, :, g])
    pooled = jnp.stack(outs, axis=2).astype(z.dtype)
    mixed = jnp.einsum('bsgc,gcd->bsgd', pooled, pool_w).reshape(B, S, POOL_WIDTH)
    return mixed * pool_scale


def causal_depthwise_conv(z, w, b):
    K, C = w.shape
    y = lax.conv_general_dilated(z, w[:, None, :], window_strides=(1,), padding=((K - 1, 0),),
                                 dimension_numbers=('NWC', 'WIO', 'NWC'), feature_group_count=C)
    return y + b


def conformer_conv(z, conv_w, conv_b, norm_g, norm_b):
    a, gate = jnp.split(z, 2, axis=-1)
    y = a * jax.nn.sigmoid(gate)
    y = causal_depthwise_conv(y, conv_w, conv_b)
    y = layer_norm(y, norm_g, norm_b)
    return jax.nn.silu(y)


def chunked_spatial_gating(u, v, norm_g, norm_b, sgu_w, sgu_b):
    B, S, _ = v.shape
    n = S // SGU_CHUNK
    v = layer_norm(v, norm_g, norm_b)
    vh = v.reshape(B, n, SGU_CHUNK, SGU_HEADS, SGU_HEAD_DIM)
    mask = jnp.tril(jnp.ones((SGU_CHUNK, SGU_CHUNK), dtype=bool))
    w = jnp.where(mask[None], sgu_w, jnp.zeros_like(sgu_w))
    mixed = jnp.einsum('hts,bnshc->bnthc', w, vh) + sgu_b.T[None, None, :, :, None]
    return u * mixed.reshape(B, S, SGU_WIDTH)


def mixer_block(h, w_in, b_in, pool_w, pool_scale, conv_w, conv_b, conv_norm_g, conv_norm_b,
                sgu_norm_g, sgu_norm_b, sgu_w, sgu_b, branch_w, branch_b, w_out):
    B, S, D = h.shape
    z = h @ w_in + b_in
    cuts = np.cumsum([POOL_WIDTH, 2 * CONV_WIDTH, SGU_WIDTH, SGU_WIDTH]).tolist()
    za, zb, zu, zv, zg = jnp.split(z, cuts, axis=-1)
    ya = causal_multiscale_pool(za, pool_w, pool_scale)
    yb = conformer_conv(zb, conv_w, conv_b, conv_norm_g, conv_norm_b)
    yc = chunked_spatial_gating(zu, zv, sgu_norm_g, sgu_norm_b, sgu_w, sgu_b)
    branches = jnp.stack([ya, yb, yc], axis=2)
    proj = jnp.einsum('bskc,kcd->bskd', branches, branch_w) + branch_b
    gates = jax.nn.sigmoid(zg.reshape(B, S, N_BRANCH, D))
    merged = jnp.einsum('bskd,bskd->bsd', gates, proj)
    return merged @ w_out


def moe_ffn(h, router_w, router_b, w_up, b_up, w_down, b_down):
    B, S, D = h.shape
    hf = h.reshape(B * S, D)
    T = hf.shape[0]
    logits = (hf @ router_w + router_b).astype(jnp.float32)
    top_val, top_idx = lax.top_k(logits, TOP_K)
    weights = jax.nn.softmax(top_val, axis=-1)
    flat_e = top_idx.reshape(-1)
    order = jnp.argsort(flat_e)
    sorted_e = flat_e[order]
    tok = (order // TOP_K).astype(jnp.int32)
    gate = weights.reshape(-1)[order].astype(h.dtype)
    counts = jnp.bincount(flat_e, length=N_EXPERTS)
    padded = (counts + MOE_BLOCK - 1) // MOE_BLOCK * MOE_BLOCK
    pad_start = jnp.cumsum(padded) - padded
    start = jnp.cumsum(counts) - counts
    dest = pad_start[sorted_e] + jnp.arange(T * TOP_K) - start[sorted_e]
    n_blocks = -(-(T * TOP_K) // MOE_BLOCK) + N_EXPERTS
    rows = n_blocks * MOE_BLOCK
    slot_tok = jnp.zeros((rows,), jnp.int32).at[dest].set(tok)
    slot_gate = jnp.zeros((rows,), h.dtype).at[dest].set(gate)
    block_e = jnp.searchsorted(jnp.cumsum(padded), jnp.arange(n_blocks) * MOE_BLOCK, side='right')
    block_e = jnp.minimum(block_e, N_EXPERTS - 1)

    def run_block(args):
        tok_b, gate_b, e = args
        xb = hf[tok_b]
        gu = xb @ w_up[e] + b_up[e]
        x_glu, x_lin = jnp.split(gu, 2, axis=-1)
        x_glu = jnp.minimum(x_glu, SWIGLU_LIMIT)
        x_lin = jnp.clip(x_lin, -SWIGLU_LIMIT, SWIGLU_LIMIT)
        act = x_glu * jax.nn.sigmoid(SWIGLU_ALPHA * x_glu) * (x_lin + 1.0)
        y = act @ w_down[e] + b_down[e]
        return y * gate_b[:, None]

    ys = lax.map(run_block, (slot_tok.reshape(n_blocks, MOE_BLOCK),
                             slot_gate.reshape(n_blocks, MOE_BLOCK), block_e))
    out = jnp.zeros_like(hf).at[slot_tok].add(ys.reshape(rows, D))
    return out.reshape(B, S, D)


def setup_inputs(seed: int = 0) -> dict:
    key = jax.random.key(seed)
    ks = iter(jax.random.split(key, 40))
    f32 = jnp.float32
    L, D = DEPTH, D_MODEL

    def nrm(shape, scale):
        return jax.random.normal(next(ks), shape, f32) * scale

    def gain(shape):
        return 1.0 + nrm(shape, 0.05)

    return {
        "x": nrm((BATCH, SEQ, D), 1.0),
        "p": nrm((L, BATCH, SEQ, PLE_DIM), 1.0),
        "mix_norm": gain((L, D)),
        "w_in": nrm((L, D, IN_WIDTH), D ** -0.5),
        "b_in": nrm((L, IN_WIDTH), 0.02),
        "pool_w": nrm((L, POOL_GROUPS, POOL_CH, POOL_CH), POOL_CH ** -0.5),
        "pool_scale": gain((L, POOL_WIDTH)),
        "conv_w": nrm((L, CONV_K, CONV_WIDTH), CONV_K ** -0.5),
        "conv_b": nrm((L, CONV_WIDTH), 0.02),
        "conv_norm_g": gain((L, CONV_WIDTH)),
        "conv_norm_b": nrm((L, CONV_WIDTH), 0.02),
        "sgu_norm_g": gain((L, SGU_WIDTH)),
        "sgu_norm_b": nrm((L, SGU_WIDTH), 0.02),
        "sgu_w": nrm((L, SGU_HEADS, SGU_CHUNK, SGU_CHUNK), 0.5 * SGU_CHUNK ** -0.5),
        "sgu_b": gain((L, SGU_HEADS, SGU_CHUNK)),
        "branch_w": nrm((L, N_BRANCH, BRANCH_WIDTH, D), BRANCH_WIDTH ** -0.5),
        "branch_b": nrm((L, N_BRANCH, D), 0.02),
        "w_out": nrm((L, D, D), D ** -0.5),
        "moe_norm": gain((L, D)),
        "router_w": nrm((L, D, N_EXPERTS), D ** -0.5),
        "router_b": nrm((L, N_EXPERTS), 0.01),
        "expert_w_up": nrm((L, N_EXPERTS, D, 2 * D_FF), D ** -0.5),
        "expert_b_up": nrm((L, N_EXPERTS, 2 * D_FF), 0.02),
        "expert_w_down": nrm((L, N_EXPERTS, D_FF, D), D_FF ** -0.5),
        "expert_b_down": nrm((L, N_EXPERTS, D), 0.02),
        "ple_norm": gain((L, D)),
        "ple_gate_w": nrm((L, D, D), D ** -0.5),
        "ple_proj_w": nrm((L, PLE_DIM, D), PLE_DIM ** -0.5),
        "final_norm": gain((D,)),
    }


def reference(x, p, mix_norm, w_in, b_in, pool_w, pool_scale, conv_w, conv_b, conv_norm_g,
              conv_norm_b, sgu_norm_g, sgu_norm_b, sgu_w, sgu_b, branch_w, branch_b, w_out,
              moe_norm, router_w, router_b, expert_w_up, expert_b_up, expert_w_down,
              expert_b_down, ple_norm, ple_gate_w, ple_proj_w, final_norm):
    for i in range(DEPTH):
        h = rms_norm(x, mix_norm[i])
        x = x + mixer_block(h, w_in[i], b_in[i], pool_w[i], pool_scale[i], conv_w[i], conv_b[i],
                            conv_norm_g[i], conv_norm_b[i], sgu_norm_g[i], sgu_norm_b[i],
                            sgu_w[i], sgu_b[i], branch_w[i], branch_b[i], w_out[i])
        h = rms_norm(x, moe_norm[i])
        x = x + moe_ffn(h, router_w[i], router_b[i], expert_w_up[i], expert_b_up[i],
                        expert_w_down[i], expert_b_down[i])
        h = rms_norm(x, ple_norm[i])
        x = x + jax.nn.sigmoid(h @ ple_gate_w[i]) * (p[i] @ ple_proj_w[i])
    return rms_norm(x, final_norm)
```

```python
import functools

import jax
import jax.numpy as jnp
from jax import lax
from jax.experimental import pallas as pl
from jax.experimental.pallas import tpu as pltpu

F32 = jnp.float32
BF16 = jnp.bfloat16
I32 = jnp.int32

D_MODEL = 1024
POOL_WINDOWS = (2, 4, 8, 16)
POOL_CH = 128
BRANCH_WIDTH = 512
CONV_K = 31
SGU_CHUNK = 128
SGU_HEADS = 4
N_BRANCH = 3
N_EXPERTS = 32
TOP_K = 4
D_FF = 1024
SWIGLU_LIMIT = 7.0
SWIGLU_ALPHA = 1.702
MOE_BLOCK = 256
PLE_DIM = 256
EPS = 1e-6

V7X_SUBLANES = 8
V7X_LANES = 128
ROW_TILES = D_MODEL // V7X_LANES
V7X_VMEM_BYTES = 64 * 1024 * 1024

MIX_TS = 512
HALO = 32
CONV_ROWS = 32
COMB_TS = 256

C_POOL = 0
C_CONV = C_POOL + BRANCH_WIDTH
C_SGU_U = C_CONV + 2 * BRANCH_WIDTH
C_SGU_V = C_SGU_U + BRANCH_WIDTH
C_GATE = C_SGU_V + BRANCH_WIDTH
IN_WIDTH = C_GATE + N_BRANCH * D_MODEL


def _rms_norm(x, g):
    return x * lax.rsqrt(jnp.mean(x * x, axis=-1, keepdims=True) + EPS) * g


def _layer_norm(x, g, b):
    mu = jnp.mean(x, axis=-1, keepdims=True)
    xc = x - mu
    var = jnp.mean(xc * xc, axis=-1, keepdims=True)
    return xc * lax.rsqrt(var + EPS) * g + b


def _sigmoid(x):
    return 1.0 / (1.0 + jnp.exp(-x))


def _dot(a, b):
    return jnp.dot(a, b, preferred_element_type=F32)


def _mixer_kernel(x_ref, mix_g_ref, w_in_ref, b_in_ref, pool_w_ref, pool_scale_ref, conv_w_ref,
                  conv_b_ref, cn_g_ref, cn_b_ref, sn_g_ref, sn_b_ref, sgu_w_ref, sgu_bt_ref,
                  branch_w_ref, branch_b_ref, w_out_ref, moe_g_ref, rw_t_ref, rb_ref,
                  x1_ref, h2r_ref, idx_ref, gate_ref, rank_ref, counts_ref,
                  pool_hist, conv_hist, count_carry):
    ts = MIX_TS
    b = pl.program_id(0)
    s = pl.program_id(1)

    @pl.when(s == 0)
    def _():
        pool_hist[0:HALO, :] = jnp.zeros((HALO, BRANCH_WIDTH), F32)
        conv_hist[0:HALO, :] = jnp.zeros((HALO, BRANCH_WIDTH), F32)

    @pl.when((b == 0) & (s == 0))
    def _():
        count_carry[...] = jnp.zeros_like(count_carry)

    x = x_ref[0]
    h = _rms_norm(x, mix_g_ref[...]).astype(BF16)

    def in_proj(c0, width):
        return _dot(h, w_in_ref[:, c0:c0 + width]) + b_in_ref[:, c0:c0 + width]

    za = in_proj(C_POOL, BRANCH_WIDTH)
    pool_hist[HALO:HALO + ts, :] = za
    row = lax.broadcasted_iota(I32, (ts, 1), 0) + s * ts
    mixed = []
    for g, w in enumerate(POOL_WINDOWS):
        c0 = g * POOL_CH
        cur = za[:, c0:c0 + POOL_CH]
        acc = cur
        for j in range(1, w):
            acc = acc + pool_hist[HALO - j:HALO - j + ts, c0:c0 + POOL_CH]
        count = jnp.minimum(row + 1, w).astype(F32)
        pooled = (acc / count - cur).astype(BF16)
        mixed.append(_dot(pooled, pool_w_ref[g]))
    ya = (jnp.concatenate(mixed, axis=-1) * pool_scale_ref[...]).astype(BF16)
    pool_hist[0:HALO, :] = pool_hist[ts:ts + HALO, :]

    zb = in_proj(C_CONV, 2 * BRANCH_WIDTH)
    conv_hist[HALO:HALO + ts, :] = zb[:, :BRANCH_WIDTH] * _sigmoid(zb[:, BRANCH_WIDTH:])
    yb_parts = []
    for r0 in range(0, ts, CONV_ROWS):
        acc = jnp.zeros((CONV_ROWS, BRANCH_WIDTH), F32) + conv_b_ref[...]
        for k in range(CONV_K):
            off = HALO - (CONV_K - 1) + k + r0
            acc = acc + conv_w_ref[k:k + 1, :] * conv_hist[off:off + CONV_ROWS, :]
        yb_rows = _layer_norm(acc, cn_g_ref[...], cn_b_ref[...])
        yb_parts.append((yb_rows * _sigmoid(yb_rows)).astype(BF16))
    yb = jnp.concatenate(yb_parts, axis=0)
    conv_hist[0:HALO, :] = conv_hist[ts:ts + HALO, :]

    zu = in_proj(C_SGU_U, BRANCH_WIDTH)
    zv = in_proj(C_SGU_V, BRANCH_WIDTH)
    v = _layer_norm(zv, sn_g_ref[...], sn_b_ref[...]).astype(BF16)
    tri = (lax.broadcasted_iota(I32, (SGU_CHUNK, SGU_CHUNK), 0)
           >= lax.broadcasted_iota(I32, (SGU_CHUNK, SGU_CHUNK), 1))
    chunks = []
    for c in range(ts // SGU_CHUNK):
        heads = []
        for hd in range(SGU_HEADS):
            w_tri = jnp.where(tri, sgu_w_ref[hd], 0.0).astype(BF16)
            vv = v[c * SGU_CHUNK:(c + 1) * SGU_CHUNK, hd * 128:(hd + 1) * 128]
            heads.append(_dot(w_tri, vv) + sgu_bt_ref[:, hd:hd + 1])
        chunks.append(jnp.concatenate(heads, axis=-1))
    yc = (zu * jnp.concatenate(chunks, axis=0)).astype(BF16)

    merged = jnp.zeros((ts, D_MODEL), F32)
    for k, yk in enumerate((ya, yb, yc)):
        proj = _dot(yk, branch_w_ref[k]) + branch_b_ref[k:k + 1, :]
        zg = in_proj(C_GATE + k * D_MODEL, D_MODEL)
        merged = merged + _sigmoid(zg) * proj
    x1 = x + _dot(merged.astype(BF16), w_out_ref[...])
    x1_ref[0] = x1

    h2 = _rms_norm(x1, moe_g_ref[...])
    for j in range(ROW_TILES):
        h2r_ref[pl.ds(j, ts, stride=ROW_TILES), :] = h2[:, j * V7X_LANES:(j + 1) * V7X_LANES]
    logits = lax.dot_general(rw_t_ref[...], h2, (((1,), (1,)), ((), ())),
                             precision=lax.Precision.HIGHEST,
                             preferred_element_type=F32) + rb_ref[...]
    e_iota = lax.broadcasted_iota(I32, (N_EXPERTS, ts), 0).astype(F32)
    vals = logits
    top_v, top_i, sels = [], [], []
    for _k in range(TOP_K):
        m = jnp.max(vals, axis=0, keepdims=True)
        idx = jnp.min(jnp.where(vals == m, e_iota, float(N_EXPERTS)), axis=0, keepdims=True)
        sel = e_iota == idx
        vals = jnp.where(sel, -jnp.inf, vals)
        top_v.append(m)
        top_i.append(idx)
        sels.append(sel)
    exps = [jnp.exp(tv - top_v[0]) for tv in top_v]
    denom = exps[0] + exps[1] + exps[2] + exps[3]
    chosen = jnp.zeros((N_EXPERTS, ts), F32)
    for sel in sels:
        chosen = chosen + jnp.where(sel, 1.0, 0.0)
    before = (lax.broadcasted_iota(I32, (ts, ts), 0) < lax.broadcasted_iota(I32, (ts, ts), 1))
    prefix = _dot(chosen.astype(BF16), jnp.where(before, 1.0, 0.0).astype(BF16))
    base = prefix + count_carry[:, 0:1]
    zeros4 = jnp.zeros((V7X_SUBLANES - TOP_K, ts), F32)
    ranks = [jnp.sum(jnp.where(sel, base, 0.0), axis=0, keepdims=True) for sel in sels]
    idx_ref[...] = jnp.concatenate(top_i + [zeros4], axis=0).astype(I32)
    rank_ref[...] = jnp.concatenate(ranks + [zeros4], axis=0).astype(I32)
    gate_ref[...] = jnp.concatenate([e / denom for e in exps] + [zeros4], axis=0)
    new_counts = count_carry[...] + jnp.sum(chosen, axis=1, keepdims=True)
    count_carry[...] = new_counts
    counts_ref[...] = new_counts


def _const_spec(shape):
    nd = len(shape)
    return pl.BlockSpec(shape, lambda *_: (0,) * nd, pipeline_mode=pl.Buffered(1))


def _mixer_call(x, lw):
    B, S, D = x.shape
    ts = MIX_TS
    n_s = S // ts
    T = B * S
    tok_blk = lambda b, s: (0, b * n_s + s)
    consts = [lw["mix_norm"], lw["w_in"], lw["b_in"], lw["pool_w"], lw["pool_scale"], lw["conv_w"],
              lw["conv_b"], lw["conv_norm_g"], lw["conv_norm_b"], lw["sgu_norm_g"], lw["sgu_norm_b"],
              lw["sgu_w"], lw["sgu_bt"], lw["branch_w"], lw["branch_b"], lw["w_out"], lw["moe_norm"],
              lw["router_wt"], lw["router_b"]]
    in_specs = [pl.BlockSpec((1, ts, D), lambda b, s: (b, s, 0))] + [_const_spec(c.shape) for c in consts]
    out_shape = (
        jax.ShapeDtypeStruct((B, S, D), F32),
        jax.ShapeDtypeStruct((T * ROW_TILES, V7X_LANES), F32),
        jax.ShapeDtypeStruct((V7X_SUBLANES, T), I32),
        jax.ShapeDtypeStruct((V7X_SUBLANES, T), F32),
        jax.ShapeDtypeStruct((V7X_SUBLANES, T), I32),
        jax.ShapeDtypeStruct((N_EXPERTS, V7X_LANES), F32),
    )
    out_specs = (
        pl.BlockSpec((1, ts, D), lambda b, s: (b, s, 0)),
        pl.BlockSpec((ts * ROW_TILES, V7X_LANES), lambda b, s: (b * n_s + s, 0)),
        pl.BlockSpec((V7X_SUBLANES, ts), tok_blk),
        pl.BlockSpec((V7X_SUBLANES, ts), tok_blk),
        pl.BlockSpec((V7X_SUBLANES, ts), tok_blk),
        pl.BlockSpec((N_EXPERTS, V7X_LANES), lambda b, s: (0, 0)),
    )
    return pl.pallas_call(
        _mixer_kernel,
        grid=(B, n_s),
        in_specs=in_specs,
        out_specs=out_specs,
        out_shape=out_shape,
        scratch_shapes=[pltpu.VMEM((HALO + ts, BRANCH_WIDTH), F32),
                        pltpu.VMEM((HALO + ts, BRANCH_WIDTH), F32),
                        pltpu.VMEM((N_EXPERTS, V7X_LANES), F32)],
        compiler_params=pltpu.CompilerParams(
            dimension_semantics=("arbitrary", "arbitrary"),
            vmem_limit_bytes=V7X_VMEM_BYTES - 8 * 1024 * 1024),
        name="mixer",
    )(x, *consts)


def _moe_kernel(be_ref, nv_ref, sf_ref, h2r_hbm, wup_ref, bup_ref, wdn_ref, bdn_ref, ybuf_hbm,
                xrows, yrows, wup_bf, wdn_bf, gsem, ssem):
    blk = pl.program_id(0)
    n = nv_ref[blk]
    prev = jnp.maximum(blk - 1, 0)
    new_expert = (blk == 0) | (be_ref[blk] != be_ref[prev])

    @pl.when(blk == 0)
    def _():
        xrows[...] = jnp.zeros_like(xrows)

    @pl.when(new_expert & (n > 0))
    def _():
        wup_bf[...] = wup_ref[0].astype(BF16)
        wdn_bf[...] = wdn_ref[0].astype(BF16)

    def row_slice(r):
        return pl.ds(pl.multiple_of(r * ROW_TILES, ROW_TILES), ROW_TILES)

    def gather_copy(r):
        tok = sf_ref[0, 0, r] // TOP_K
        return pltpu.make_async_copy(h2r_hbm.at[tok], xrows.at[row_slice(r)], gsem)

    def scatter_copy(r):
        return pltpu.make_async_copy(yrows.at[row_slice(r)], ybuf_hbm.at[sf_ref[0, 0, r]], ssem)

    @pl.when(n > 0)
    def _():
        def start_gather(r, c):
            gather_copy(r).start()
            return c
        lax.fori_loop(0, n, start_gather, 0)

        def wait_gather(r, c):
            gather_copy(r).wait()
            return c
        lax.fori_loop(0, n, wait_gather, 0)

        xs = jnp.concatenate(
            [xrows[pl.ds(j, MOE_BLOCK, stride=ROW_TILES), :] for j in range(ROW_TILES)],
            axis=-1).astype(BF16)
        gu = _dot(xs, wup_bf[...]) + bup_ref[0]
        x_glu = jnp.minimum(gu[:, :D_FF], SWIGLU_LIMIT)
        x_lin = jnp.clip(gu[:, D_FF:], -SWIGLU_LIMIT, SWIGLU_LIMIT)
        act = x_glu * _sigmoid(SWIGLU_ALPHA * x_glu) * (x_lin + 1.0)
        y = _dot(act.astype(BF16), wdn_bf[...]) + bdn_ref[0]
        for j in range(ROW_TILES):
            yrows[pl.ds(j, MOE_BLOCK, stride=ROW_TILES), :] = y[:, j * V7X_LANES:(j + 1) * V7X_LANES]

        def start_scatter(r, c):
            scatter_copy(r).start()
            return c
        lax.fori_loop(0, n, start_scatter, 0)

        def wait_scatter(r, c):
            scatter_copy(r).wait()
            return c
        lax.fori_loop(0, n, wait_scatter, 0)


def _moe_call(h2r, slot_flat, block_e, n_valid, lw):
    T = h2r.shape[0] // ROW_TILES
    n_blocks = block_e.shape[0]
    h2r3 = h2r.reshape(T, ROW_TILES, V7X_LANES)
    sf3 = slot_flat.reshape(n_blocks, 1, MOE_BLOCK)
    grid_spec = pltpu.PrefetchScalarGridSpec(
        num_scalar_prefetch=2,
        grid=(n_blocks,),
        in_specs=[
            pl.BlockSpec((1, 1, MOE_BLOCK), lambda i, be, nv: (i, 0, 0), memory_space=pltpu.SMEM),
            pl.BlockSpec(memory_space=pl.ANY),
            pl.BlockSpec((1, D_MODEL, 2 * D_FF), lambda i, be, nv: (be[i], 0, 0)),
            pl.BlockSpec((1, 1, 2 * D_FF), lambda i, be, nv: (be[i], 0, 0)),
            pl.BlockSpec((1, D_FF, D_MODEL), lambda i, be, nv: (be[i], 0, 0)),
            pl.BlockSpec((1, 1, D_MODEL), lambda i, be, nv: (be[i], 0, 0)),
        ],
        out_specs=pl.BlockSpec(memory_space=pl.ANY),
        scratch_shapes=[
            pltpu.VMEM((MOE_BLOCK * ROW_TILES, V7X_LANES), F32),
            pltpu.VMEM((MOE_BLOCK * ROW_TILES, V7X_LANES), F32),
            pltpu.VMEM((D_MODEL, 2 * D_FF), BF16),
            pltpu.VMEM((D_FF, D_MODEL), BF16),
            pltpu.SemaphoreType.DMA(()),
            pltpu.SemaphoreType.DMA(()),
        ],
    )
    return pl.pallas_call(
        _moe_kernel,
        grid_spec=grid_spec,
        out_shape=jax.ShapeDtypeStruct((T * TOP_K, ROW_TILES, V7X_LANES), F32),
        compiler_params=pltpu.CompilerParams(
            dimension_semantics=("arbitrary",),
            vmem_limit_bytes=V7X_VMEM_BYTES - 8 * 1024 * 1024),
        name="moe",
    )(block_e, n_valid, sf3, h2r3, lw["expert_w_up"], lw["expert_b_up"], lw["expert_w_down"],
      lw["expert_b_down"])


def _combine_kernel(x1_ref, y_ref, gate_ref, p_ref, ple_g_ref, gate_w_ref, proj_w_ref, fin_g_ref,
                    out_ref, *, last):
    tc = COMB_TS
    x2 = x1_ref[...]
    stride = TOP_K * ROW_TILES
    for k in range(TOP_K):
        yk = jnp.concatenate(
            [y_ref[pl.ds(k * ROW_TILES + j, tc, stride=stride), :] for j in range(ROW_TILES)], axis=-1)
        x2 = x2 + gate_ref[:, k:k + 1] * yk
    h3 = _rms_norm(x2, ple_g_ref[...]).astype(BF16)
    g = _sigmoid(_dot(h3, gate_w_ref[...]))
    pp = _dot(p_ref[...].astype(BF16), proj_w_ref[...])
    x3 = x2 + g * pp
    if last:
        x3 = _rms_norm(x3, fin_g_ref[...])
    out_ref[...] = x3


def _combine_call(x1, ybuf, gates_tm, p, lw, fin_g, last):
    T, D = x1.shape
    tc = COMB_TS
    y2 = ybuf.reshape(T * TOP_K * ROW_TILES, V7X_LANES)
    consts = [lw["ple_norm"], lw["ple_gate_w"], lw["ple_proj_w"], fin_g]
    return pl.pallas_call(
        functools.partial(_combine_kernel, last=last),
        grid=(T // tc,),
        in_specs=[
            pl.BlockSpec((tc, D), lambda i: (i, 0)),
            pl.BlockSpec((tc * TOP_K * ROW_TILES, V7X_LANES), lambda i: (i, 0)),
            pl.BlockSpec((tc, V7X_SUBLANES), lambda i: (i, 0)),
            pl.BlockSpec((tc, PLE_DIM), lambda i: (i, 0)),
        ] + [_const_spec(c.shape) for c in consts],
        out_specs=pl.BlockSpec((tc, D), lambda i: (i, 0)),
        out_shape=jax.ShapeDtypeStruct((T, D), F32),
        compiler_params=pltpu.CompilerParams(
            dimension_semantics=("arbitrary",),
            vmem_limit_bytes=V7X_VMEM_BYTES // 2),
        name="combine",
    )(x1, y2, gates_tm, p, *consts)


def _routing_tables(idx8, rank8, counts, n_blocks):
    T = idx8.shape[1]
    counts_i = counts[:, 0].astype(I32)
    padded = (counts_i + MOE_BLOCK - 1) // MOE_BLOCK * MOE_BLOCK
    pad_end = jnp.cumsum(padded)
    pad_start = pad_end - padded
    idx4, rank4 = idx8[:TOP_K], rank8[:TOP_K]
    onehot = idx4[:, :, None] == jnp.arange(N_EXPERTS, dtype=I32)[None, None, :]
    dest = jnp.sum(jnp.where(onehot, pad_start[None, None, :], 0), axis=-1) + rank4
    flat = jnp.arange(T, dtype=I32)[None, :] * TOP_K + jnp.arange(TOP_K, dtype=I32)[:, None]
    slot_flat = jnp.zeros((n_blocks * MOE_BLOCK,), I32).at[dest.reshape(-1)].set(flat.reshape(-1))
    blk0 = jnp.arange(n_blocks, dtype=I32) * MOE_BLOCK
    used = blk0 < pad_end[-1]
    be = jnp.minimum(jnp.searchsorted(pad_end, blk0, side="right"), N_EXPERTS - 1).astype(I32)
    n_valid = jnp.where(used, jnp.clip(counts_i[be] - (blk0 - pad_start[be]), 0, MOE_BLOCK), 0)
    last_used = jnp.maximum(pad_end[-1] // MOE_BLOCK - 1, 0)
    be = jnp.where(used, be, be[last_used])
    return slot_flat, be, n_valid.astype(I32)


def kernel(x, p, mix_norm, w_in, b_in, pool_w, pool_scale, conv_w, conv_b, conv_norm_g, conv_norm_b,
           sgu_norm_g, sgu_norm_b, sgu_w, sgu_b, branch_w, branch_b, w_out, moe_norm, router_w,
           router_b, expert_w_up, expert_b_up, expert_w_down, expert_b_down, ple_norm, ple_gate_w,
           ple_proj_w, final_norm):
    B, S, D = x.shape
    T = B * S
    depth = w_in.shape[0]
    assert D == D_MODEL and S % MIX_TS == 0 and T % COMB_TS == 0 and w_in.shape[2] == IN_WIDTH
    n_blocks = -(-(T * TOP_K) // MOE_BLOCK) + N_EXPERTS
    row = lambda a: a.reshape(1, -1)
    for i in range(depth):
        lw = {
            "mix_norm": row(mix_norm[i]), "w_in": w_in[i].astype(BF16), "b_in": row(b_in[i]),
            "pool_w": pool_w[i].astype(BF16), "pool_scale": row(pool_scale[i]),
            "conv_w": conv_w[i], "conv_b": row(conv_b[i]),
            "conv_norm_g": row(conv_norm_g[i]), "conv_norm_b": row(conv_norm_b[i]),
            "sgu_norm_g": row(sgu_norm_g[i]), "sgu_norm_b": row(sgu_norm_b[i]),
            "sgu_w": sgu_w[i], "sgu_bt": sgu_b[i].T,
            "branch_w": branch_w[i].astype(BF16), "branch_b": branch_b[i],
            "w_out": w_out[i].astype(BF16), "moe_norm": row(moe_norm[i]),
            "router_wt": router_w[i].T, "router_b": router_b[i].reshape(-1, 1),
            "expert_w_up": expert_w_up[i], "expert_b_up": expert_b_up[i][:, None, :],
            "expert_w_down": expert_w_down[i], "expert_b_down": expert_b_down[i][:, None, :],
            "ple_norm": row(ple_norm[i]), "ple_gate_w": ple_gate_w[i].astype(BF16),
            "ple_proj_w": ple_proj_w[i].astype(BF16),
        }
        x1, h2r, idx8, gate8, rank8, counts = _mixer_call(x, lw)
        slot_flat, block_e, n_valid = _routing_tables(idx8, rank8, counts, n_blocks)
        ybuf = _moe_call(h2r, slot_flat, block_e, n_valid, lw)
        x = _combine_call(x1.reshape(T, D), ybuf, gate8.T, p[i].reshape(T, PLE_DIM), lw,
                          row(final_norm), last=(i == depth - 1)).reshape(B, S, D)
    return x
```

```python
import functools

import jax
import jax.numpy as jnp
from jax import lax
from jax.experimental import pallas as pl
from jax.experimental.pallas import tpu as pltpu

F32 = jnp.float32
BF16 = jnp.bfloat16
I32 = jnp.int32

D_MODEL = 1024
POOL_WINDOWS = (2, 4, 8, 16)
POOL_CH = 128
BRANCH_WIDTH = 512
CONV_K = 31
SGU_CHUNK = 128
SGU_HEADS = 4
N_BRANCH = 3
N_EXPERTS = 32
TOP_K = 4
D_FF = 1024
SWIGLU_LIMIT = 7.0
SWIGLU_ALPHA = 1.702
MOE_BLOCK = 256
PLE_DIM = 256
EPS = 1e-6

V7X_SUBLANES = 8
V7X_LANES = 128
ROW_TILES = D_MODEL // V7X_LANES
V7X_VMEM_BYTES = 64 * 1024 * 1024

MIX_TS = 512
HALO = 32
CONV_ROWS = 32
COMB_TS = 512
ROUTE_CHUNK = 8192
ROUTE_UNROLL = 16

C_POOL = 0
C_CONV = C_POOL + BRANCH_WIDTH
C_SGU_U = C_CONV + 2 * BRANCH_WIDTH
C_SGU_V = C_SGU_U + BRANCH_WIDTH
C_GATE = C_SGU_V + BRANCH_WIDTH
IN_WIDTH = C_GATE + N_BRANCH * D_MODEL


def _rms_norm(x, g):
    return x * lax.rsqrt(jnp.mean(x * x, axis=-1, keepdims=True) + EPS) * g


def _layer_norm(x, g, b):
    mu = jnp.mean(x, axis=-1, keepdims=True)
    xc = x - mu
    var = jnp.mean(xc * xc, axis=-1, keepdims=True)
    return xc * lax.rsqrt(var + EPS) * g + b


def _sigmoid(x):
    return 1.0 / (1.0 + jnp.exp(-x))


def _dot(a, b):
    return jnp.dot(a, b, preferred_element_type=F32)


def _mixer_kernel(x_ref, mix_g_ref, w_in_ref, b_in_ref, pool_w_ref, pool_scale_ref, conv_w_ref,
                  conv_b_ref, cn_g_ref, cn_b_ref, sn_g_ref, sn_b_ref, sgu_w_ref, sgu_bt_ref,
                  branch_w_ref, branch_b_ref, w_out_ref, moe_g_ref, rw_t_ref, rb_ref,
                  x1_ref, h2r_ref, idx_ref, gate_ref, rank_ref, counts_ref,
                  pool_hist, conv_hist, count_carry):
    ts = MIX_TS
    b = pl.program_id(0)
    s = pl.program_id(1)

    @pl.when(s == 0)
    def _():
        pool_hist[0:HALO, :] = jnp.zeros((HALO, BRANCH_WIDTH), F32)
        conv_hist[0:HALO, :] = jnp.zeros((HALO, BRANCH_WIDTH), F32)

    @pl.when((b == 0) & (s == 0))
    def _():
        count_carry[...] = jnp.zeros_like(count_carry)

    x = x_ref[0]
    h = _rms_norm(x, mix_g_ref[...]).astype(BF16)

    def in_proj(c0, width):
        return _dot(h, w_in_ref[:, c0:c0 + width]) + b_in_ref[:, c0:c0 + width]

    za = in_proj(C_POOL, BRANCH_WIDTH)
    pool_hist[HALO:HALO + ts, :] = za
    row = lax.broadcasted_iota(I32, (ts, 1), 0) + s * ts
    mixed = []
    for g, w in enumerate(POOL_WINDOWS):
        c0 = g * POOL_CH
        cur = za[:, c0:c0 + POOL_CH]
        acc = cur
        for j in range(1, w):
            acc = acc + pool_hist[HALO - j:HALO - j + ts, c0:c0 + POOL_CH]
        count = jnp.minimum(row + 1, w).astype(F32)
        pooled = (acc / count - cur).astype(BF16)
        mixed.append(_dot(pooled, pool_w_ref[g]))
    ya = (jnp.concatenate(mixed, axis=-1) * pool_scale_ref[...]).astype(BF16)
    pool_hist[0:HALO, :] = pool_hist[ts:ts + HALO, :]

    zb = in_proj(C_CONV, 2 * BRANCH_WIDTH)
    conv_hist[HALO:HALO + ts, :] = zb[:, :BRANCH_WIDTH] * _sigmoid(zb[:, BRANCH_WIDTH:])
    yb_parts = []
    for r0 in range(0, ts, CONV_ROWS):
        acc = jnp.zeros((CONV_ROWS, BRANCH_WIDTH), F32) + conv_b_ref[...]
        for k in range(CONV_K):
            off = HALO - (CONV_K - 1) + k + r0
            acc = acc + conv_w_ref[k:k + 1, :] * conv_hist[off:off + CONV_ROWS, :]
        yb_rows = _layer_norm(acc, cn_g_ref[...], cn_b_ref[...])
        yb_parts.append((yb_rows * _sigmoid(yb_rows)).astype(BF16))
    yb = jnp.concatenate(yb_parts, axis=0)
    conv_hist[0:HALO, :] = conv_hist[ts:ts + HALO, :]

    zu = in_proj(C_SGU_U, BRANCH_WIDTH)
    zv = in_proj(C_SGU_V, BRANCH_WIDTH)
    v = _layer_norm(zv, sn_g_ref[...], sn_b_ref[...]).astype(BF16)
    tri = (lax.broadcasted_iota(I32, (SGU_CHUNK, SGU_CHUNK), 0)
           >= lax.broadcasted_iota(I32, (SGU_CHUNK, SGU_CHUNK), 1))
    chunks = []
    for c in range(ts // SGU_CHUNK):
        heads = []
        for hd in range(SGU_HEADS):
            w_tri = jnp.where(tri, sgu_w_ref[hd], 0.0).astype(BF16)
            vv = v[c * SGU_CHUNK:(c + 1) * SGU_CHUNK, hd * 128:(hd + 1) * 128]
            heads.append(_dot(w_tri, vv) + sgu_bt_ref[:, hd:hd + 1])
        chunks.append(jnp.concatenate(heads, axis=-1))
    yc = (zu * jnp.concatenate(chunks, axis=0)).astype(BF16)

    merged = jnp.zeros((ts, D_MODEL), F32)
    for k, yk in enumerate((ya, yb, yc)):
        proj = _dot(yk, branch_w_ref[k]) + branch_b_ref[k:k + 1, :]
        zg = in_proj(C_GATE + k * D_MODEL, D_MODEL)
        merged = merged + _sigmoid(zg) * proj
    x1 = x + _dot(merged.astype(BF16), w_out_ref[...])
    x1_ref[0] = x1

    h2 = _rms_norm(x1, moe_g_ref[...])
    for j in range(ROW_TILES):
        h2r_ref[pl.ds(j, ts, stride=ROW_TILES), :] = h2[:, j * V7X_LANES:(j + 1) * V7X_LANES]
    logits = lax.dot_general(rw_t_ref[...], h2, (((1,), (1,)), ((), ())),
                             precision=lax.Precision.HIGHEST,
                             preferred_element_type=F32) + rb_ref[...]
    e_iota = lax.broadcasted_iota(I32, (N_EXPERTS, ts), 0).astype(F32)
    vals = logits
    top_v, top_i, sels = [], [], []
    for _k in range(TOP_K):
        m = jnp.max(vals, axis=0, keepdims=True)
        idx = jnp.min(jnp.where(vals == m, e_iota, float(N_EXPERTS)), axis=0, keepdims=True)
        sel = e_iota == idx
        vals = jnp.where(sel, -jnp.inf, vals)
        top_v.append(m)
        top_i.append(idx)
        sels.append(sel)
    exps = [jnp.exp(tv - top_v[0]) for tv in top_v]
    denom = exps[0] + exps[1] + exps[2] + exps[3]
    chosen = jnp.zeros((N_EXPERTS, ts), F32)
    for sel in sels:
        chosen = chosen + jnp.where(sel, 1.0, 0.0)
    before = (lax.broadcasted_iota(I32, (ts, ts), 0) < lax.broadcasted_iota(I32, (ts, ts), 1))
    prefix = _dot(chosen.astype(BF16), jnp.where(before, 1.0, 0.0).astype(BF16))
    base = prefix + count_carry[:, 0:1]
    zeros4 = jnp.zeros((V7X_SUBLANES - TOP_K, ts), F32)
    ranks = [jnp.sum(jnp.where(sel, base, 0.0), axis=0, keepdims=True) for sel in sels]
    idx_ref[...] = jnp.concatenate(top_i + [zeros4], axis=0).astype(I32)
    rank_ref[...] = jnp.concatenate(ranks + [zeros4], axis=0).astype(I32)
    gate_ref[...] = jnp.concatenate([e / denom for e in exps] + [zeros4], axis=0)
    new_counts = count_carry[...] + jnp.sum(chosen, axis=1, keepdims=True)
    count_carry[...] = new_counts
    counts_ref[...] = new_counts


def _const_spec(shape):
    nd = len(shape)
    return pl.BlockSpec(shape, lambda *_: (0,) * nd, pipeline_mode=pl.Buffered(1))


def _mixer_call(x, lw):
    B, S, D = x.shape
    ts = MIX_TS
    n_s = S // ts
    T = B * S
    tok_blk = lambda b, s: (0, b * n_s + s)
    consts = [lw["mix_norm"], lw["w_in"], lw["b_in"], lw["pool_w"], lw["pool_scale"], lw["conv_w"],
              lw["conv_b"], lw["conv_norm_g"], lw["conv_norm_b"], lw["sgu_norm_g"], lw["sgu_norm_b"],
              lw["sgu_w"], lw["sgu_bt"], lw["branch_w"], lw["branch_b"], lw["w_out"], lw["moe_norm"],
              lw["router_wt"], lw["router_b"]]
    in_specs = [pl.BlockSpec((1, ts, D), lambda b, s: (b, s, 0))] + [_const_spec(c.shape) for c in consts]
    out_shape = (
        jax.ShapeDtypeStruct((B, S, D), F32),
        jax.ShapeDtypeStruct((T * ROW_TILES, V7X_LANES), F32),
        jax.ShapeDtypeStruct((V7X_SUBLANES, T), I32),
        jax.ShapeDtypeStruct((V7X_SUBLANES, T), F32),
        jax.ShapeDtypeStruct((V7X_SUBLANES, T), I32),
        jax.ShapeDtypeStruct((N_EXPERTS, V7X_LANES), F32),
    )
    out_specs = (
        pl.BlockSpec((1, ts, D), lambda b, s: (b, s, 0)),
        pl.BlockSpec((ts * ROW_TILES, V7X_LANES), lambda b, s: (b * n_s + s, 0)),
        pl.BlockSpec((V7X_SUBLANES, ts), tok_blk),
        pl.BlockSpec((V7X_SUBLANES, ts), tok_blk),
        pl.BlockSpec((V7X_SUBLANES, ts), tok_blk),
        pl.BlockSpec((N_EXPERTS, V7X_LANES), lambda b, s: (0, 0)),
    )
    return pl.pallas_call(
        _mixer_kernel,
        grid=(B, n_s),
        in_specs=in_specs,
        out_specs=out_specs,
        out_shape=out_shape,
        scratch_shapes=[pltpu.VMEM((HALO + ts, BRANCH_WIDTH), F32),
                        pltpu.VMEM((HALO + ts, BRANCH_WIDTH), F32),
                        pltpu.VMEM((N_EXPERTS, V7X_LANES), F32)],
        compiler_params=pltpu.CompilerParams(
            dimension_semantics=("arbitrary", "arbitrary"),
            vmem_limit_bytes=V7X_VMEM_BYTES - 8 * 1024 * 1024),
        name="mixer",
    )(x, *consts)


def _route_kernel(dest_ref, zeros_hbm, slot_a_ref, sem):
    step = pl.program_id(0)

    @pl.when(step == 0)
    def _():
        init = pltpu.make_async_copy(zeros_hbm, slot_a_ref, sem)
        init.start()
        init.wait()

    base = step * ROUTE_CHUNK

    def body(i, c):
        for u in range(ROUTE_UNROLL):
            j = i * ROUTE_UNROLL + u
            slot_a_ref[dest_ref[j]] = base + j
        return c
    lax.fori_loop(0, ROUTE_CHUNK // ROUTE_UNROLL, body, 0)


def _route_call(dest_flat, n_slots):
    n = dest_flat.shape[0]
    assert n % ROUTE_CHUNK == 0
    return pl.pallas_call(
        _route_kernel,
        grid=(n // ROUTE_CHUNK,),
        in_specs=[pl.BlockSpec((ROUTE_CHUNK,), lambda i: (i,), memory_space=pltpu.SMEM),
                  pl.BlockSpec(memory_space=pl.ANY)],
        out_specs=pl.BlockSpec((n_slots,), lambda i: (0,), memory_space=pltpu.SMEM),
        out_shape=jax.ShapeDtypeStruct((n_slots,), I32),
        scratch_shapes=[pltpu.SemaphoreType.DMA(())],
        compiler_params=pltpu.CompilerParams(dimension_semantics=("arbitrary",)),
        name="route",
    )(dest_flat, jnp.zeros((n_slots,), I32))


MODE_IDLE, MODE_BLOCK, MODE_DRAIN = 0, 1, 2


def _moe_kernel(be_ref, mode_ref, src_cur, src_next, dst_prev, h2r_hbm, wup_ref, bup_ref,
                wdn_ref, bdn_ref, ybuf_hbm, x_even, x_odd, y_even, y_odd, wup_bf, wdn_bf, gsem, ssem,
                *, n_assign):
    g = pl.program_id(0)
    mode = mode_ref[g]
    xbufs, ybufs = (x_even, x_odd), (y_even, y_odd)

    def gather_copy(idx_ref, r, par):
        rows = pl.ds(r * ROW_TILES, ROW_TILES)
        return pltpu.make_async_copy(h2r_hbm.at[idx_ref[0, 0, r]], xbufs[par].at[rows], gsem.at[par])

    def scatter_copy(r, par):
        rows = pl.ds(r * ROW_TILES, ROW_TILES)
        return pltpu.make_async_copy(ybufs[par].at[rows], ybuf_hbm.at[dst_prev[0, 0, r]], ssem.at[par])

    def spare_fill_copy(r):
        rows = pl.ds(r * ROW_TILES, ROW_TILES)
        return pltpu.make_async_copy(y_odd.at[rows], ybuf_hbm.at[n_assign + r], ssem.at[1])

    @pl.when(g == 0)
    def _():
        y_odd[...] = jnp.zeros_like(y_odd)
        for r in range(MOE_BLOCK):
            spare_fill_copy(r).start()
        for r in range(MOE_BLOCK):
            gather_copy(src_cur, r, 0).start()
        for r in range(MOE_BLOCK):
            spare_fill_copy(r).wait()

    prev = jnp.maximum(g - 1, 0)
    new_expert = (g == 0) | (be_ref[g] != be_ref[prev])

    @pl.when(new_expert & (mode == MODE_BLOCK))
    def _():
        wup_bf[...] = wup_ref[0, 0].astype(BF16)
        wdn_bf[...] = wdn_ref[0, 0].astype(BF16)

    def step(par):
        other = 1 - par
        x_cur, y_cur = xbufs[par], ybufs[par]

        @pl.when(mode != MODE_IDLE)
        def _():
            for r in range(MOE_BLOCK):
                gather_copy(src_cur, r, par).wait()

        @pl.when((mode != MODE_IDLE) & (g >= 1))
        def _():
            for r in range(MOE_BLOCK):
                scatter_copy(r, par).wait()

        @pl.when(mode == MODE_BLOCK)
        def _():
            for r in range(MOE_BLOCK):
                gather_copy(src_next, r, other).start()
            for r in range(MOE_BLOCK):
                scatter_copy(r, other).start()
            xs = jnp.concatenate(
                [x_cur[pl.ds(j, MOE_BLOCK, stride=ROW_TILES), :] for j in range(ROW_TILES)],
                axis=-1).astype(BF16)
            gu = _dot(xs, wup_bf[...]) + bup_ref[0, 0]
            x_glu = jnp.minimum(gu[:, :D_FF], SWIGLU_LIMIT)
            x_lin = jnp.clip(gu[:, D_FF:], -SWIGLU_LIMIT, SWIGLU_LIMIT)
            act = x_glu * _sigmoid(SWIGLU_ALPHA * x_glu) * (x_lin + 1.0)
            y = _dot(act.astype(BF16), wdn_bf[...]) + bdn_ref[0, 0]
            for j in range(ROW_TILES):
                y_cur[pl.ds(j, MOE_BLOCK, stride=ROW_TILES), :] = y[:, j * V7X_LANES:(j + 1) * V7X_LANES]

        @pl.when(mode == MODE_DRAIN)
        def _():
            for r in range(MOE_BLOCK):
                scatter_copy(r, other).start()
            for r in range(MOE_BLOCK):
                scatter_copy(r, other).wait()

    for par in (0, 1):
        pl.when(g % 2 == par)(functools.partial(step, par))


def _moe_call(layer, h2r, src_tok, dst_ext, block_e, mode, w_up, b_up, w_down, b_down, n_rows_out):
    T = h2r.shape[0] // ROW_TILES
    n_blocks = src_tok.shape[0]
    n_steps = n_blocks + 1
    h2r3 = h2r.reshape(T, ROW_TILES, V7X_LANES)
    last = n_blocks - 1
    smem_blk = lambda f: pl.BlockSpec((1, 1, MOE_BLOCK), f, memory_space=pltpu.SMEM)
    w_idx = lambda g, be, md: (layer, be[g], 0, 0)
    grid_spec = pltpu.PrefetchScalarGridSpec(
        num_scalar_prefetch=2,
        grid=(n_steps,),
        in_specs=[
            smem_blk(lambda g, be, md: (jnp.minimum(g, last), 0, 0)),
            smem_blk(lambda g, be, md: (jnp.minimum(g + 1, last), 0, 0)),
            smem_blk(lambda g, be, md: (g, 0, 0)),
            pl.BlockSpec(memory_space=pl.ANY),
            pl.BlockSpec((1, 1, D_MODEL, 2 * D_FF), w_idx),
            pl.BlockSpec((1, 1, 1, 2 * D_FF), w_idx),
            pl.BlockSpec((1, 1, D_FF, D_MODEL), w_idx),
            pl.BlockSpec((1, 1, 1, D_MODEL), w_idx),
        ],
        out_specs=pl.BlockSpec(memory_space=pl.ANY),
        scratch_shapes=[pltpu.VMEM((MOE_BLOCK * ROW_TILES, V7X_LANES), F32)] * 4 + [
            pltpu.VMEM((D_MODEL, 2 * D_FF), BF16),
            pltpu.VMEM((D_FF, D_MODEL), BF16),
            pltpu.SemaphoreType.DMA((2,)),
            pltpu.SemaphoreType.DMA((2,)),
        ],
    )
    return pl.pallas_call(
        functools.partial(_moe_kernel, n_assign=TOP_K * T),
        grid_spec=grid_spec,
        out_shape=jax.ShapeDtypeStruct((n_rows_out, ROW_TILES, V7X_LANES), F32),
        compiler_params=pltpu.CompilerParams(
            dimension_semantics=("arbitrary",),
            vmem_limit_bytes=V7X_VMEM_BYTES - 8 * 1024 * 1024),
        name="moe",
    )(block_e, mode, src_tok.reshape(n_blocks, 1, MOE_BLOCK), src_tok.reshape(n_blocks, 1, MOE_BLOCK),
      dst_ext.reshape(n_steps, 1, MOE_BLOCK), h2r3, w_up, b_up, w_down, b_down)


def _combine_kernel(x1_ref, y0_ref, y1_ref, y2_ref, y3_ref, gate_ref, p_ref, ple_g_ref, gate_w_ref,
                    proj_w_ref, fin_g_ref, out_ref, *, last):
    tc = COMB_TS
    x2 = x1_ref[...]
    for k, y_ref in enumerate((y0_ref, y1_ref, y2_ref, y3_ref)):
        yk = jnp.concatenate(
            [y_ref[pl.ds(j, tc, stride=ROW_TILES), :] for j in range(ROW_TILES)], axis=-1)
        x2 = x2 + gate_ref[:, k:k + 1] * yk
    h3 = _rms_norm(x2, ple_g_ref[...]).astype(BF16)
    g = _sigmoid(_dot(h3, gate_w_ref[...]))
    pp = _dot(p_ref[...].astype(BF16), proj_w_ref[...])
    x3 = x2 + g * pp
    if last:
        x3 = _rms_norm(x3, fin_g_ref[...])
    out_ref[...] = x3


def _combine_call(layer, x1, ybuf, gates_tm, p2, lw, fin_g, last):
    T, D = x1.shape
    tc = COMB_TS
    n_t = T // tc
    y2 = ybuf.reshape(ybuf.shape[0] * ROW_TILES, V7X_LANES)
    consts = [lw["ple_norm"], lw["ple_gate_w"], lw["ple_proj_w"], fin_g]
    y_spec = lambda k: pl.BlockSpec((tc * ROW_TILES, V7X_LANES), lambda i: (k * n_t + i, 0))
    return pl.pallas_call(
        functools.partial(_combine_kernel, last=last),
        grid=(n_t,),
        in_specs=[pl.BlockSpec((tc, D), lambda i: (i, 0))] + [y_spec(k) for k in range(TOP_K)]
        + [pl.BlockSpec((tc, V7X_SUBLANES), lambda i: (i, 0)),
           pl.BlockSpec((tc, PLE_DIM), lambda i: (layer * n_t + i, 0))]
        + [_const_spec(c.shape) for c in consts],
        out_specs=pl.BlockSpec((tc, D), lambda i: (i, 0)),
        out_shape=jax.ShapeDtypeStruct((T, D), F32),
        compiler_params=pltpu.CompilerParams(
            dimension_semantics=("arbitrary",),
            vmem_limit_bytes=V7X_VMEM_BYTES - 16 * 1024 * 1024),
        name="combine",
    )(x1, y2, y2, y2, y2, gates_tm, p2, *consts)


def _assignment_slots(idx8, rank8, counts):
    counts_i = counts[:, 0].astype(I32)
    padded = (counts_i + MOE_BLOCK - 1) // MOE_BLOCK * MOE_BLOCK
    pad_end = jnp.cumsum(padded)
    pad_start = pad_end - padded
    idx4, rank4 = idx8[:TOP_K], rank8[:TOP_K]
    onehot = idx4[:, :, None] == jnp.arange(N_EXPERTS, dtype=I32)[None, None, :]
    dest = jnp.sum(jnp.where(onehot, pad_start[None, None, :], 0), axis=-1) + rank4
    return dest.reshape(-1), counts_i, pad_start, pad_end


def _block_tables(slot_a, counts_i, pad_start, pad_end, n_blocks, T):
    n_assign = TOP_K * T
    blk0 = jnp.arange(n_blocks, dtype=I32) * MOE_BLOCK
    n_used = pad_end[-1] // MOE_BLOCK
    be = jnp.minimum(jnp.sum(blk0[:, None] >= pad_end[None, :], axis=1), N_EXPERTS - 1).astype(I32)
    n_valid = jnp.clip(counts_i[be] - (blk0 - pad_start[be]), 0, MOE_BLOCK)
    n_valid = jnp.where(jnp.arange(n_blocks) < n_used, n_valid, 0)
    slot_pos = jnp.arange(MOE_BLOCK, dtype=I32)[None, :]
    valid = slot_pos < n_valid[:, None]
    slot_a2 = slot_a.reshape(n_blocks, MOE_BLOCK)
    src_tok = jnp.where(valid, slot_a2 % T, 0)
    parity = (jnp.arange(n_blocks, dtype=I32) % 2)[:, None]
    dst = jnp.where(valid, slot_a2, n_assign + parity * MOE_BLOCK + slot_pos)
    dst_ext = jnp.concatenate([n_assign + MOE_BLOCK + slot_pos, dst], axis=0)
    steps = jnp.arange(n_blocks + 1, dtype=I32)
    mode = jnp.where(steps < n_used, MODE_BLOCK, jnp.where(steps == n_used, MODE_DRAIN, MODE_IDLE))
    be_ext = be[jnp.minimum(steps, n_used - 1)]
    return src_tok.astype(I32), dst_ext.astype(I32), be_ext.astype(I32), mode.astype(I32)


def kernel(x, p, mix_norm, w_in, b_in, pool_w, pool_scale, conv_w, conv_b, conv_norm_g, conv_norm_b,
           sgu_norm_g, sgu_norm_b, sgu_w, sgu_b, branch_w, branch_b, w_out, moe_norm, router_w,
           router_b, expert_w_up, expert_b_up, expert_w_down, expert_b_down, ple_norm, ple_gate_w,
           ple_proj_w, final_norm):
    B, S, D = x.shape
    T = B * S
    depth = w_in.shape[0]
    assert D == D_MODEL and S % MIX_TS == 0 and T % COMB_TS == 0 and w_in.shape[2] == IN_WIDTH
    n_blocks = -(-(T * TOP_K) // MOE_BLOCK) + N_EXPERTS
    n_slots = n_blocks * MOE_BLOCK
    n_rows_out = TOP_K * T + 2 * MOE_BLOCK
    row = lambda a: a.reshape(1, -1)
    p2 = p.reshape(depth * T, PLE_DIM)
    b_up4 = expert_b_up[:, :, None, :]
    b_down4 = expert_b_down[:, :, None, :]
    for i in range(depth):
        lw = {
            "mix_norm": row(mix_norm[i]), "w_in": w_in[i].astype(BF16), "b_in": row(b_in[i]),
            "pool_w": pool_w[i].astype(BF16), "pool_scale": row(pool_scale[i]),
            "conv_w": conv_w[i], "conv_b": row(conv_b[i]),
            "conv_norm_g": row(conv_norm_g[i]), "conv_norm_b": row(conv_norm_b[i]),
            "sgu_norm_g": row(sgu_norm_g[i]), "sgu_norm_b": row(sgu_norm_b[i]),
            "sgu_w": sgu_w[i], "sgu_bt": sgu_b[i].T,
            "branch_w": branch_w[i].astype(BF16), "branch_b": branch_b[i],
            "w_out": w_out[i].astype(BF16), "moe_norm": row(moe_norm[i]),
            "router_wt": router_w[i].T, "router_b": router_b[i].reshape(-1, 1),
            "ple_norm": row(ple_norm[i]), "ple_gate_w": ple_gate_w[i].astype(BF16),
            "ple_proj_w": ple_proj_w[i].astype(BF16),
        }
        x1, h2r, idx8, gate8, rank8, counts = _mixer_call(x, lw)
        dest, counts_i, pad_start, pad_end = _assignment_slots(idx8, rank8, counts)
        slot_a = _route_call(dest, n_slots)
        src_tok, dst_ext, block_e, mode = _block_tables(slot_a, counts_i, pad_start, pad_end, n_blocks, T)
        ybuf = _moe_call(i, h2r, src_tok, dst_ext, block_e, mode, expert_w_up, b_up4,
                         expert_w_down, b_down4, n_rows_out)
        x = _combine_call(i, x1.reshape(T, D), ybuf, gate8.T, p2, lw, row(final_norm),
                          last=(i == depth - 1)).reshape(B, S, D)
    return x
```

```python
import functools

import jax
import jax.numpy as jnp
from jax import lax
from jax.experimental import pallas as pl
from jax.experimental.pallas import tpu as pltpu

F32 = jnp.float32
BF16 = jnp.bfloat16
I32 = jnp.int32

D_MODEL = 1024
POOL_WINDOWS = (2, 4, 8, 16)
POOL_CH = 128
BRANCH_WIDTH = 512
CONV_K = 31
SGU_CHUNK = 128
SGU_HEADS = 4
N_BRANCH = 3
N_EXPERTS = 32
TOP_K = 4
D_FF = 1024
SWIGLU_LIMIT = 7.0
SWIGLU_ALPHA = 1.702
MOE_BLOCK = 256
PLE_DIM = 256
EPS = 1e-6

V7X_SUBLANES = 8
V7X_LANES = 128
ROW_TILES = D_MODEL // V7X_LANES
V7X_VMEM_BYTES = 64 * 1024 * 1024

MIX_TS = 512
HALO = 32
CONV_ROWS = 32
COMB_TS = 512
ROUTE_CHUNK = 8192
ROUTE_UNROLL = 16

C_POOL = 0
C_CONV = C_POOL + BRANCH_WIDTH
C_SGU_U = C_CONV + 2 * BRANCH_WIDTH
C_SGU_V = C_SGU_U + BRANCH_WIDTH
C_GATE = C_SGU_V + BRANCH_WIDTH
IN_WIDTH = C_GATE + N_BRANCH * D_MODEL


def _rms_norm(x, g):
    return x * lax.rsqrt(jnp.mean(x * x, axis=-1, keepdims=True) + EPS) * g


def _layer_norm(x, g, b):
    mu = jnp.mean(x, axis=-1, keepdims=True)
    xc = x - mu
    var = jnp.mean(xc * xc, axis=-1, keepdims=True)
    return xc * lax.rsqrt(var + EPS) * g + b


def _sigmoid(x):
    return 1.0 / (1.0 + jnp.exp(-x))


def _dot(a, b):
    return jnp.dot(a, b, preferred_element_type=F32)


def _mixer_kernel(x_ref, mix_g_ref, w_in_ref, b_in_ref, pool_w_ref, pool_scale_ref, conv_w_ref,
                  conv_b_ref, cn_g_ref, cn_b_ref, sn_g_ref, sn_b_ref, sgu_w_ref, sgu_bt_ref,
                  branch_w_ref, branch_b_ref, w_out_ref, moe_g_ref, rw_t_ref, rb_ref,
                  x1_ref, h2r_ref, idx_ref, gate_ref, rank_ref, counts_ref,
                  pool_hist, conv_hist, conv_shift, count_carry):
    ts = MIX_TS
    b = pl.program_id(0)
    s = pl.program_id(1)

    @pl.when(s == 0)
    def _():
        pool_hist[0:HALO, :] = jnp.zeros((HALO, BRANCH_WIDTH), F32)
        conv_hist[0:HALO, :] = jnp.zeros((HALO, BRANCH_WIDTH), F32)

    @pl.when((b == 0) & (s == 0))
    def _():
        count_carry[...] = jnp.zeros_like(count_carry)

    x = x_ref[0]
    h = _rms_norm(x, mix_g_ref[...]).astype(BF16)

    def in_proj(c0, width):
        return _dot(h, w_in_ref[:, c0:c0 + width]) + b_in_ref[:, c0:c0 + width]

    za = in_proj(C_POOL, BRANCH_WIDTH)
    pool_hist[HALO:HALO + ts, :] = za
    row = lax.broadcasted_iota(I32, (ts, 1), 0) + s * ts
    mixed = []
    for g, w in enumerate(POOL_WINDOWS):
        c0 = g * POOL_CH
        cur = za[:, c0:c0 + POOL_CH]
        acc = cur
        for j in range(1, w):
            acc = acc + pool_hist[HALO - j:HALO - j + ts, c0:c0 + POOL_CH]
        count = jnp.minimum(row + 1, w).astype(F32)
        pooled = (acc / count - cur).astype(BF16)
        mixed.append(_dot(pooled, pool_w_ref[g]))
    ya = (jnp.concatenate(mixed, axis=-1) * pool_scale_ref[...]).astype(BF16)
    pool_hist[0:HALO, :] = pool_hist[ts:ts + HALO, :]

    zb = in_proj(C_CONV, 2 * BRANCH_WIDTH)
    conv_hist[HALO:HALO + ts, :] = zb[:, :BRANCH_WIDTH] * _sigmoid(zb[:, BRANCH_WIDTH:])
    n_shift_rows = HALO + ts - V7X_SUBLANES
    for sft in range(1, V7X_SUBLANES):
        conv_shift[sft - 1] = conv_hist[sft:sft + n_shift_rows, :]
    yb_parts = []
    for r0 in range(0, ts, CONV_ROWS):
        acc = jnp.zeros((CONV_ROWS, BRANCH_WIDTH), F32) + conv_b_ref[...]
        for k in range(CONV_K):
            off = HALO - (CONV_K - 1) + k + r0
            base, sft = off - off % V7X_SUBLANES, off % V7X_SUBLANES
            if sft == 0:
                window = conv_hist[base:base + CONV_ROWS, :]
            else:
                window = conv_shift[sft - 1, base:base + CONV_ROWS, :]
            acc = acc + conv_w_ref[k:k + 1, :] * window
        yb_rows = _layer_norm(acc, cn_g_ref[...], cn_b_ref[...])
        yb_parts.append((yb_rows * _sigmoid(yb_rows)).astype(BF16))
    yb = jnp.concatenate(yb_parts, axis=0)
    conv_hist[0:HALO, :] = conv_hist[ts:ts + HALO, :]

    zu = in_proj(C_SGU_U, BRANCH_WIDTH)
    zv = in_proj(C_SGU_V, BRANCH_WIDTH)
    v = _layer_norm(zv, sn_g_ref[...], sn_b_ref[...]).astype(BF16)
    tri = (lax.broadcasted_iota(I32, (SGU_CHUNK, SGU_CHUNK), 0)
           >= lax.broadcasted_iota(I32, (SGU_CHUNK, SGU_CHUNK), 1))
    chunks = []
    for c in range(ts // SGU_CHUNK):
        heads = []
        for hd in range(SGU_HEADS):
            w_tri = jnp.where(tri, sgu_w_ref[hd], 0.0).astype(BF16)
            vv = v[c * SGU_CHUNK:(c + 1) * SGU_CHUNK, hd * 128:(hd + 1) * 128]
            heads.append(_dot(w_tri, vv) + sgu_bt_ref[:, hd:hd + 1])
        chunks.append(jnp.concatenate(heads, axis=-1))
    yc = (zu * jnp.concatenate(chunks, axis=0)).astype(BF16)

    merged = jnp.zeros((ts, D_MODEL), F32)
    for k, yk in enumerate((ya, yb, yc)):
        proj = _dot(yk, branch_w_ref[k]) + branch_b_ref[k:k + 1, :]
        zg = in_proj(C_GATE + k * D_MODEL, D_MODEL)
        merged = merged + _sigmoid(zg) * proj
    x1 = x + _dot(merged.astype(BF16), w_out_ref[...])
    x1_ref[0] = x1

    h2 = _rms_norm(x1, moe_g_ref[...])
    for j in range(ROW_TILES):
        h2r_ref[pl.ds(j, ts, stride=ROW_TILES), :] = h2[:, j * V7X_LANES:(j + 1) * V7X_LANES]
    logits = lax.dot_general(rw_t_ref[...], h2, (((1,), (1,)), ((), ())),
                             precision=lax.Precision.HIGHEST,
                             preferred_element_type=F32) + rb_ref[...]
    e_iota = lax.broadcasted_iota(I32, (N_EXPERTS, ts), 0).astype(F32)
    vals = logits
    top_v, top_i, sels = [], [], []
    for _k in range(TOP_K):
        m = jnp.max(vals, axis=0, keepdims=True)
        idx = jnp.min(jnp.where(vals == m, e_iota, float(N_EXPERTS)), axis=0, keepdims=True)
        sel = e_iota == idx
        vals = jnp.where(sel, -jnp.inf, vals)
        top_v.append(m)
        top_i.append(idx)
        sels.append(sel)
    exps = [jnp.exp(tv - top_v[0]) for tv in top_v]
    denom = exps[0] + exps[1] + exps[2] + exps[3]
    chosen = jnp.zeros((N_EXPERTS, ts), F32)
    for sel in sels:
        chosen = chosen + jnp.where(sel, 1.0, 0.0)
    before = (lax.broadcasted_iota(I32, (ts, ts), 0) < lax.broadcasted_iota(I32, (ts, ts), 1))
    prefix = _dot(chosen.astype(BF16), jnp.where(before, 1.0, 0.0).astype(BF16))
    base = prefix + count_carry[:, 0:1]
    zeros4 = jnp.zeros((V7X_SUBLANES - TOP_K, ts), F32)
    ranks = [jnp.sum(jnp.where(sel, base, 0.0), axis=0, keepdims=True) for sel in sels]
    idx_ref[...] = jnp.concatenate(top_i + [zeros4], axis=0).astype(I32)
    rank_ref[...] = jnp.concatenate(ranks + [zeros4], axis=0).astype(I32)
    gate_ref[...] = jnp.concatenate([e / denom for e in exps] + [zeros4], axis=0)
    new_counts = count_carry[...] + jnp.sum(chosen, axis=1, keepdims=True)
    count_carry[...] = new_counts
    counts_ref[...] = new_counts


def _const_spec(shape):
    nd = len(shape)
    return pl.BlockSpec(shape, lambda *_: (0,) * nd, pipeline_mode=pl.Buffered(1))


def _mixer_call(x, lw):
    B, S, D = x.shape
    ts = MIX_TS
    n_s = S // ts
    T = B * S
    tok_blk = lambda b, s: (0, b * n_s + s)
    consts = [lw["mix_norm"], lw["w_in"], lw["b_in"], lw["pool_w"], lw["pool_scale"], lw["conv_w"],
              lw["conv_b"], lw["conv_norm_g"], lw["conv_norm_b"], lw["sgu_norm_g"], lw["sgu_norm_b"],
              lw["sgu_w"], lw["sgu_bt"], lw["branch_w"], lw["branch_b"], lw["w_out"], lw["moe_norm"],
              lw["router_wt"], lw["router_b"]]
    in_specs = [pl.BlockSpec((1, ts, D), lambda b, s: (b, s, 0))] + [_const_spec(c.shape) for c in consts]
    out_shape = (
        jax.ShapeDtypeStruct((B, S, D), F32),
        jax.ShapeDtypeStruct((T * ROW_TILES, V7X_LANES), F32),
        jax.ShapeDtypeStruct((V7X_SUBLANES, T), I32),
        jax.ShapeDtypeStruct((V7X_SUBLANES, T), F32),
        jax.ShapeDtypeStruct((V7X_SUBLANES, T), I32),
        jax.ShapeDtypeStruct((N_EXPERTS, V7X_LANES), F32),
    )
    out_specs = (
        pl.BlockSpec((1, ts, D), lambda b, s: (b, s, 0)),
        pl.BlockSpec((ts * ROW_TILES, V7X_LANES), lambda b, s: (b * n_s + s, 0)),
        pl.BlockSpec((V7X_SUBLANES, ts), tok_blk),
        pl.BlockSpec((V7X_SUBLANES, ts), tok_blk),
        pl.BlockSpec((V7X_SUBLANES, ts), tok_blk),
        pl.BlockSpec((N_EXPERTS, V7X_LANES), lambda b, s: (0, 0)),
    )
    return pl.pallas_call(
        _mixer_kernel,
        grid=(B, n_s),
        in_specs=in_specs,
        out_specs=out_specs,
        out_shape=out_shape,
        scratch_shapes=[pltpu.VMEM((HALO + ts, BRANCH_WIDTH), F32),
                        pltpu.VMEM((HALO + ts, BRANCH_WIDTH), F32),
                        pltpu.VMEM((V7X_SUBLANES - 1, HALO + ts - V7X_SUBLANES, BRANCH_WIDTH), F32),
                        pltpu.VMEM((N_EXPERTS, V7X_LANES), F32)],
        compiler_params=pltpu.CompilerParams(
            dimension_semantics=("arbitrary", "arbitrary"),
            vmem_limit_bytes=V7X_VMEM_BYTES - 8 * 1024 * 1024),
        name="mixer",
    )(x, *consts)


def _route_kernel(dest_ref, zeros_hbm, slot_a_ref, sem):
    step = pl.program_id(0)

    @pl.when(step == 0)
    def _():
        init = pltpu.make_async_copy(zeros_hbm, slot_a_ref, sem)
        init.start()
        init.wait()

    base = step * ROUTE_CHUNK

    def body(i, c):
        for u in range(ROUTE_UNROLL):
            j = i * ROUTE_UNROLL + u
            slot_a_ref[dest_ref[j]] = base + j
        return c
    lax.fori_loop(0, ROUTE_CHUNK // ROUTE_UNROLL, body, 0)


def _route_call(dest_flat, n_slots):
    n = dest_flat.shape[0]
    assert n % ROUTE_CHUNK == 0
    return pl.pallas_call(
        _route_kernel,
        grid=(n // ROUTE_CHUNK,),
        in_specs=[pl.BlockSpec((ROUTE_CHUNK,), lambda i: (i,), memory_space=pltpu.SMEM),
                  pl.BlockSpec(memory_space=pl.ANY)],
        out_specs=pl.BlockSpec((n_slots,), lambda i: (0,), memory_space=pltpu.SMEM),
        out_shape=jax.ShapeDtypeStruct((n_slots,), I32),
        scratch_shapes=[pltpu.SemaphoreType.DMA(())],
        compiler_params=pltpu.CompilerParams(dimension_semantics=("arbitrary",)),
        name="route",
    )(dest_flat, jnp.zeros((n_slots,), I32))


MODE_IDLE, MODE_BLOCK, MODE_DRAIN = 0, 1, 2


def _moe_kernel(be_ref, mode_ref, src_cur, src_next, dst_prev, h2r_hbm, wup_ref, bup_ref,
                wdn_ref, bdn_ref, ybuf_hbm, x_even, x_odd, y_even, y_odd, wup_bf, wdn_bf, gsem, ssem,
                *, n_assign):
    g = pl.program_id(0)
    mode = mode_ref[g]
    xbufs, ybufs = (x_even, x_odd), (y_even, y_odd)

    def gather_copy(idx_ref, r, par):
        rows = pl.ds(r * ROW_TILES, ROW_TILES)
        return pltpu.make_async_copy(h2r_hbm.at[idx_ref[0, 0, r]], xbufs[par].at[rows], gsem.at[par])

    def scatter_copy(r, par):
        rows = pl.ds(r * ROW_TILES, ROW_TILES)
        return pltpu.make_async_copy(ybufs[par].at[rows], ybuf_hbm.at[dst_prev[0, 0, r]], ssem.at[par])

    def spare_fill_copy(r):
        rows = pl.ds(r * ROW_TILES, ROW_TILES)
        return pltpu.make_async_copy(y_odd.at[rows], ybuf_hbm.at[n_assign + r], ssem.at[1])

    @pl.when(g == 0)
    def _():
        y_odd[...] = jnp.zeros_like(y_odd)
        for r in range(MOE_BLOCK):
            spare_fill_copy(r).start()
        for r in range(MOE_BLOCK):
            gather_copy(src_cur, r, 0).start()
        for r in range(MOE_BLOCK):
            spare_fill_copy(r).wait()

    prev = jnp.maximum(g - 1, 0)
    new_expert = (g == 0) | (be_ref[g] != be_ref[prev])

    @pl.when(new_expert & (mode == MODE_BLOCK))
    def _():
        wup_bf[...] = wup_ref[0, 0].astype(BF16)
        wdn_bf[...] = wdn_ref[0, 0].astype(BF16)

    def step(par):
        other = 1 - par
        x_cur, y_cur = xbufs[par], ybufs[par]

        @pl.when(mode != MODE_IDLE)
        def _():
            for r in range(MOE_BLOCK):
                gather_copy(src_cur, r, par).wait()

        @pl.when((mode != MODE_IDLE) & (g >= 1))
        def _():
            for r in range(MOE_BLOCK):
                scatter_copy(r, par).wait()

        @pl.when(mode == MODE_BLOCK)
        def _():
            for r in range(MOE_BLOCK):
                gather_copy(src_next, r, other).start()
            for r in range(MOE_BLOCK):
                scatter_copy(r, other).start(priority=1)
            xs = jnp.concatenate(
                [x_cur[pl.ds(j, MOE_BLOCK, stride=ROW_TILES), :] for j in range(ROW_TILES)],
                axis=-1).astype(BF16)
            gu = _dot(xs, wup_bf[...]) + bup_ref[0, 0]
            x_glu = jnp.minimum(gu[:, :D_FF], SWIGLU_LIMIT)
            x_lin = jnp.clip(gu[:, D_FF:], -SWIGLU_LIMIT, SWIGLU_LIMIT)
            act = x_glu * _sigmoid(SWIGLU_ALPHA * x_glu) * (x_lin + 1.0)
            y = _dot(act.astype(BF16), wdn_bf[...]) + bdn_ref[0, 0]
            for j in range(ROW_TILES):
                y_cur[pl.ds(j, MOE_BLOCK, stride=ROW_TILES), :] = y[:, j * V7X_LANES:(j + 1) * V7X_LANES]

        @pl.when(mode == MODE_DRAIN)
        def _():
            for r in range(MOE_BLOCK):
                scatter_copy(r, other).start()
            for r in range(MOE_BLOCK):
                scatter_copy(r, other).wait()

    for par in (0, 1):
        pl.when(g % 2 == par)(functools.partial(step, par))


def _moe_call(layer, h2r, src_tok, dst_ext, block_e, mode, w_up, b_up, w_down, b_down, n_rows_out):
    T = h2r.shape[0] // ROW_TILES
    n_blocks = src_tok.shape[0]
    n_steps = n_blocks + 1
    h2r3 = h2r.reshape(T, ROW_TILES, V7X_LANES)
    last = n_blocks - 1
    smem_blk = lambda f: pl.BlockSpec((1, 1, MOE_BLOCK), f, memory_space=pltpu.SMEM)
    w_idx = lambda g, be, md: (layer, be[g], 0, 0)
    grid_spec = pltpu.PrefetchScalarGridSpec(
        num_scalar_prefetch=2,
        grid=(n_steps,),
        in_specs=[
            smem_blk(lambda g, be, md: (jnp.minimum(g, last), 0, 0)),
            smem_blk(lambda g, be, md: (jnp.minimum(g + 1, last), 0, 0)),
            smem_blk(lambda g, be, md: (g, 0, 0)),
            pl.BlockSpec(memory_space=pl.ANY),
            pl.BlockSpec((1, 1, D_MODEL, 2 * D_FF), w_idx),
            pl.BlockSpec((1, 1, 1, 2 * D_FF), w_idx),
            pl.BlockSpec((1, 1, D_FF, D_MODEL), w_idx),
            pl.BlockSpec((1, 1, 1, D_MODEL), w_idx),
        ],
        out_specs=pl.BlockSpec(memory_space=pl.ANY),
        scratch_shapes=[pltpu.VMEM((MOE_BLOCK * ROW_TILES, V7X_LANES), F32)] * 4 + [
            pltpu.VMEM((D_MODEL, 2 * D_FF), BF16),
            pltpu.VMEM((D_FF, D_MODEL), BF16),
            pltpu.SemaphoreType.DMA((2,)),
            pltpu.SemaphoreType.DMA((2,)),
        ],
    )
    return pl.pallas_call(
        functools.partial(_moe_kernel, n_assign=TOP_K * T),
        grid_spec=grid_spec,
        out_shape=jax.ShapeDtypeStruct((n_rows_out, ROW_TILES, V7X_LANES), F32),
        compiler_params=pltpu.CompilerParams(
            dimension_semantics=("arbitrary",),
            vmem_limit_bytes=V7X_VMEM_BYTES - 8 * 1024 * 1024),
        name="moe",
    )(block_e, mode, src_tok.reshape(n_blocks, 1, MOE_BLOCK), src_tok.reshape(n_blocks, 1, MOE_BLOCK),
      dst_ext.reshape(n_steps, 1, MOE_BLOCK), h2r3, w_up, b_up, w_down, b_down)


def _combine_kernel(x1_ref, y0_ref, y1_ref, y2_ref, y3_ref, gate_ref, p_ref, ple_g_ref, gate_w_ref,
                    proj_w_ref, fin_g_ref, out_ref, *, last):
    tc = COMB_TS
    x2 = x1_ref[...]
    for k, y_ref in enumerate((y0_ref, y1_ref, y2_ref, y3_ref)):
        yk = jnp.concatenate(
            [y_ref[pl.ds(j, tc, stride=ROW_TILES), :] for j in range(ROW_TILES)], axis=-1)
        x2 = x2 + gate_ref[:, k:k + 1] * yk
    h3 = _rms_norm(x2, ple_g_ref[...]).astype(BF16)
    g = _sigmoid(_dot(h3, gate_w_ref[...]))
    pp = _dot(p_ref[...].astype(BF16), proj_w_ref[...])
    x3 = x2 + g * pp
    if last:
        x3 = _rms_norm(x3, fin_g_ref[...])
    out_ref[...] = x3


def _combine_call(layer, x1, ybuf, gates_tm, p2, lw, fin_g, last):
    T, D = x1.shape
    tc = COMB_TS
    n_t = T // tc
    y2 = ybuf.reshape(ybuf.shape[0] * ROW_TILES, V7X_LANES)
    consts = [lw["ple_norm"], lw["ple_gate_w"], lw["ple_proj_w"], fin_g]
    y_spec = lambda k: pl.BlockSpec((tc * ROW_TILES, V7X_LANES), lambda i: (k * n_t + i, 0))
    return pl.pallas_call(
        functools.partial(_combine_kernel, last=last),
        grid=(n_t,),
        in_specs=[pl.BlockSpec((tc, D), lambda i: (i, 0))] + [y_spec(k) for k in range(TOP_K)]
        + [pl.BlockSpec((tc, V7X_SUBLANES), lambda i: (i, 0)),
           pl.BlockSpec((tc, PLE_DIM), lambda i: (layer * n_t + i, 0))]
        + [_const_spec(c.shape) for c in consts],
        out_specs=pl.BlockSpec((tc, D), lambda i: (i, 0)),
        out_shape=jax.ShapeDtypeStruct((T, D), F32),
        compiler_params=pltpu.CompilerParams(
            dimension_semantics=("arbitrary",),
            vmem_limit_bytes=V7X_VMEM_BYTES - 16 * 1024 * 1024),
        name="combine",
    )(x1, y2, y2, y2, y2, gates_tm, p2, *consts)


def _assignment_slots(idx8, rank8, counts):
    counts_i = counts[:, 0].astype(I32)
    padded = (counts_i + MOE_BLOCK - 1) // MOE_BLOCK * MOE_BLOCK
    pad_end = jnp.cumsum(padded)
    pad_start = pad_end - padded
    idx4, rank4 = idx8[:TOP_K], rank8[:TOP_K]
    onehot = idx4[:, :, None] == jnp.arange(N_EXPERTS, dtype=I32)[None, None, :]
    dest = jnp.sum(jnp.where(onehot, pad_start[None, None, :], 0), axis=-1) + rank4
    return dest.reshape(-1), counts_i, pad_start, pad_end


def _block_tables(slot_a, counts_i, pad_start, pad_end, n_blocks, T):
    n_assign = TOP_K * T
    blk0 = jnp.arange(n_blocks, dtype=I32) * MOE_BLOCK
    n_used = pad_end[-1] // MOE_BLOCK
    be = jnp.minimum(jnp.sum(blk0[:, None] >= pad_end[None, :], axis=1), N_EXPERTS - 1).astype(I32)
    n_valid = jnp.clip(counts_i[be] - (blk0 - pad_start[be]), 0, MOE_BLOCK)
    n_valid = jnp.where(jnp.arange(n_blocks) < n_used, n_valid, 0)
    slot_pos = jnp.arange(MOE_BLOCK, dtype=I32)[None, :]
    valid = slot_pos < n_valid[:, None]
    slot_a2 = slot_a.reshape(n_blocks, MOE_BLOCK)
    src_tok = jnp.where(valid, slot_a2 % T, 0)
    parity = (jnp.arange(n_blocks, dtype=I32) % 2)[:, None]
    dst = jnp.where(valid, slot_a2, n_assign + parity * MOE_BLOCK + slot_pos)
    dst_ext = jnp.concatenate([n_assign + MOE_BLOCK + slot_pos, dst], axis=0)
    steps = jnp.arange(n_blocks + 1, dtype=I32)
    mode = jnp.where(steps < n_used, MODE_BLOCK, jnp.where(steps == n_used, MODE_DRAIN, MODE_IDLE))
    be_ext = be[jnp.minimum(steps, n_used - 1)]
    return src_tok.astype(I32), dst_ext.astype(I32), be_ext.astype(I32), mode.astype(I32)


def kernel(x, p, mix_norm, w_in, b_in, pool_w, pool_scale, conv_w, conv_b, conv_norm_g, conv_norm_b,
           sgu_norm_g, sgu_norm_b, sgu_w, sgu_b, branch_w, branch_b, w_out, moe_norm, router_w,
           router_b, expert_w_up, expert_b_up, expert_w_down, expert_b_down, ple_norm, ple_gate_w,
           ple_proj_w, final_norm):
    B, S, D = x.shape
    T = B * S
    depth = w_in.shape[0]
    assert D == D_MODEL and S % MIX_TS == 0 and T % COMB_TS == 0 and w_in.shape[2] == IN_WIDTH
    n_blocks = -(-(T * TOP_K) // MOE_BLOCK) + N_EXPERTS
    n_slots = n_blocks * MOE_BLOCK
    n_rows_out = TOP_K * T + 2 * MOE_BLOCK
    row = lambda a: a.reshape(1, -1)
    p2 = p.reshape(depth * T, PLE_DIM)
    b_up4 = expert_b_up[:, :, None, :]
    b_down4 = expert_b_down[:, :, None, :]
    for i in range(depth):
        lw = {
            "mix_norm": row(mix_norm[i]), "w_in": w_in[i].astype(BF16), "b_in": row(b_in[i]),
            "pool_w": pool_w[i].astype(BF16), "pool_scale": row(pool_scale[i]),
            "conv_w": conv_w[i], "conv_b": row(conv_b[i]),
            "conv_norm_g": row(conv_norm_g[i]), "conv_norm_b": row(conv_norm_b[i]),
            "sgu_norm_g": row(sgu_norm_g[i]), "sgu_norm_b": row(sgu_norm_b[i]),
            "sgu_w": sgu_w[i], "sgu_bt": sgu_b[i].T,
            "branch_w": branch_w[i].astype(BF16), "branch_b": branch_b[i],
            "w_out": w_out[i].astype(BF16), "moe_norm": row(moe_norm[i]),
            "router_wt": router_w[i].T, "router_b": router_b[i].reshape(-1, 1),
            "ple_norm": row(ple_norm[i]), "ple_gate_w": ple_gate_w[i].astype(BF16),
            "ple_proj_w": ple_proj_w[i].astype(BF16),
        }
        x1, h2r, idx8, gate8, rank8, counts = _mixer_call(x, lw)
        dest, counts_i, pad_start, pad_end = _assignment_slots(idx8, rank8, counts)
        slot_a = _route_call(dest, n_slots)
        src_tok, dst_ext, block_e, mode = _block_tables(slot_a, counts_i, pad_start, pad_end, n_blocks, T)
        ybuf = _moe_call(i, h2r, src_tok, dst_ext, block_e, mode, expert_w_up, b_up4,
                         expert_w_down, b_down4, n_rows_out)
        x = _combine_call(i, x1.reshape(T, D), ybuf, gate8.T, p2, lw, row(final_norm),
                          last=(i == depth - 1)).reshape(B, S, D)
    return x
```

```python
import functools

import jax
import jax.numpy as jnp
from jax import lax
from jax.experimental import pallas as pl
from jax.experimental.pallas import tpu as pltpu

F32 = jnp.float32
BF16 = jnp.bfloat16
I32 = jnp.int32

D_MODEL = 1024
POOL_WINDOWS = (2, 4, 8, 16)
POOL_CH = 128
BRANCH_WIDTH = 512
CONV_K = 31
SGU_CHUNK = 128
SGU_HEADS = 4
N_BRANCH = 3
N_EXPERTS = 32
TOP_K = 4
D_FF = 1024
SWIGLU_LIMIT = 7.0
SWIGLU_ALPHA = 1.702
MOE_BLOCK = 256
PLE_DIM = 256
EPS = 1e-6

V7X_SUBLANES = 8
V7X_LANES = 128
ROW_TILES = D_MODEL // V7X_LANES
V7X_VMEM_BYTES = 64 * 1024 * 1024

MIX_TS = 512
HALO = 32
CONV_ROWS = 32
N_SORTED = TOP_K * MIX_TS

C_POOL = 0
C_CONV = C_POOL + BRANCH_WIDTH
C_SGU_U = C_CONV + 2 * BRANCH_WIDTH
C_SGU_V = C_SGU_U + BRANCH_WIDTH
C_GATE = C_SGU_V + BRANCH_WIDTH
IN_WIDTH = C_GATE + N_BRANCH * D_MODEL


def _rms_norm(x, g):
    return x * lax.rsqrt(jnp.mean(x * x, axis=-1, keepdims=True) + EPS) * g


def _layer_norm(x, g, b):
    mu = jnp.mean(x, axis=-1, keepdims=True)
    xc = x - mu
    var = jnp.mean(xc * xc, axis=-1, keepdims=True)
    return xc * lax.rsqrt(var + EPS) * g + b


def _sigmoid(x):
    return 1.0 / (1.0 + jnp.exp(-x))


def _dot(a, b):
    return jnp.dot(a, b, preferred_element_type=F32)


def _mixer_kernel(x_ref, mix_g_ref, w_in_ref, b_in_ref, pool_w_ref, pool_scale_ref, conv_w_ref,
                  conv_b_ref, cn_g_ref, cn_b_ref, sn_g_ref, sn_b_ref, sgu_w_ref, sgu_bt_ref,
                  branch_w_ref, branch_b_ref, w_out_ref, moe_g_ref, rw_t_ref, rb_ref,
                  x1_ref, h2b_ref, pos_ref, gate_ref, counts_ref,
                  pool_hist, conv_hist, conv_shift):
    ts = MIX_TS
    s = pl.program_id(1)

    @pl.when(s == 0)
    def _():
        pool_hist[0:HALO, :] = jnp.zeros((HALO, BRANCH_WIDTH), F32)
        conv_hist[0:HALO, :] = jnp.zeros((HALO, BRANCH_WIDTH), F32)

    x = x_ref[0]
    h = _rms_norm(x, mix_g_ref[...]).astype(BF16)

    def in_proj(c0, width):
        return _dot(h, w_in_ref[:, c0:c0 + width]) + b_in_ref[:, c0:c0 + width]

    za = in_proj(C_POOL, BRANCH_WIDTH)
    pool_hist[HALO:HALO + ts, :] = za
    row = lax.broadcasted_iota(I32, (ts, 1), 0) + s * ts
    mixed = []
    for g, w in enumerate(POOL_WINDOWS):
        c0 = g * POOL_CH
        cur = za[:, c0:c0 + POOL_CH]
        acc = cur
        for j in range(1, w):
            acc = acc + pool_hist[HALO - j:HALO - j + ts, c0:c0 + POOL_CH]
        count = jnp.minimum(row + 1, w).astype(F32)
        pooled = (acc / count - cur).astype(BF16)
        mixed.append(_dot(pooled, pool_w_ref[g]))
    ya = (jnp.concatenate(mixed, axis=-1) * pool_scale_ref[...]).astype(BF16)
    pool_hist[0:HALO, :] = pool_hist[ts:ts + HALO, :]

    zb = in_proj(C_CONV, 2 * BRANCH_WIDTH)
    conv_hist[HALO:HALO + ts, :] = zb[:, :BRANCH_WIDTH] * _sigmoid(zb[:, BRANCH_WIDTH:])
    n_shift_rows = HALO + ts - V7X_SUBLANES
    for sft in range(1, V7X_SUBLANES):
        conv_shift[sft - 1] = conv_hist[sft:sft + n_shift_rows, :]
    yb_parts = []
    for r0 in range(0, ts, CONV_ROWS):
        acc = jnp.zeros((CONV_ROWS, BRANCH_WIDTH), F32) + conv_b_ref[...]
        for k in range(CONV_K):
            off = HALO - (CONV_K - 1) + k + r0
            base, sft = off - off % V7X_SUBLANES, off % V7X_SUBLANES
            if sft == 0:
                window = conv_hist[base:base + CONV_ROWS, :]
            else:
                window = conv_shift[sft - 1, base:base + CONV_ROWS, :]
            acc = acc + conv_w_ref[k:k + 1, :] * window
        yb_rows = _layer_norm(acc, cn_g_ref[...], cn_b_ref[...])
        yb_parts.append((yb_rows * _sigmoid(yb_rows)).astype(BF16))
    yb = jnp.concatenate(yb_parts, axis=0)
    conv_hist[0:HALO, :] = conv_hist[ts:ts + HALO, :]

    zu = in_proj(C_SGU_U, BRANCH_WIDTH)
    zv = in_proj(C_SGU_V, BRANCH_WIDTH)
    v = _layer_norm(zv, sn_g_ref[...], sn_b_ref[...]).astype(BF16)
    tri = (lax.broadcasted_iota(I32, (SGU_CHUNK, SGU_CHUNK), 0)
           >= lax.broadcasted_iota(I32, (SGU_CHUNK, SGU_CHUNK), 1))
    chunks = []
    for c in range(ts // SGU_CHUNK):
        heads = []
        for hd in range(SGU_HEADS):
            w_tri = jnp.where(tri, sgu_w_ref[hd], 0.0).astype(BF16)
            vv = v[c * SGU_CHUNK:(c + 1) * SGU_CHUNK, hd * 128:(hd + 1) * 128]
            heads.append(_dot(w_tri, vv) + sgu_bt_ref[:, hd:hd + 1])
        chunks.append(jnp.concatenate(heads, axis=-1))
    yc = (zu * jnp.concatenate(chunks, axis=0)).astype(BF16)

    merged = jnp.zeros((ts, D_MODEL), F32)
    for k, yk in enumerate((ya, yb, yc)):
        proj = _dot(yk, branch_w_ref[k]) + branch_b_ref[k:k + 1, :]
        zg = in_proj(C_GATE + k * D_MODEL, D_MODEL)
        merged = merged + _sigmoid(zg) * proj
    x1 = x + _dot(merged.astype(BF16), w_out_ref[...])
    x1_ref[0] = x1

    h2 = _rms_norm(x1, moe_g_ref[...])
    h2b_ref[...] = h2.astype(BF16)
    logits = lax.dot_general(rw_t_ref[...], h2, (((1,), (1,)), ((), ())),
                             precision=lax.Precision.HIGHEST,
                             preferred_element_type=F32) + rb_ref[...]
    e_iota = lax.broadcasted_iota(I32, (N_EXPERTS, ts), 0).astype(F32)
    vals = logits
    top_v, sels = [], []
    for _k in range(TOP_K):
        m = jnp.max(vals, axis=0, keepdims=True)
        idx = jnp.min(jnp.where(vals == m, e_iota, float(N_EXPERTS)), axis=0, keepdims=True)
        sel = e_iota == idx
        vals = jnp.where(sel, -jnp.inf, vals)
        top_v.append(m)
        sels.append(sel)
    exps = [jnp.exp(tv - top_v[0]) for tv in top_v]
    denom = exps[0] + exps[1] + exps[2] + exps[3]
    chosen = jnp.zeros((N_EXPERTS, ts), F32)
    for sel in sels:
        chosen = chosen + jnp.where(sel, 1.0, 0.0)
    chosen_b = chosen.astype(BF16)
    before = (lax.broadcasted_iota(I32, (ts, ts), 0) < lax.broadcasted_iota(I32, (ts, ts), 1))
    prefix = _dot(chosen_b, jnp.where(before, 1.0, 0.0).astype(BF16))
    lower = (lax.broadcasted_iota(I32, (N_EXPERTS, N_EXPERTS), 0)
             > lax.broadcasted_iota(I32, (N_EXPERTS, N_EXPERTS), 1))
    tile_start = jnp.sum(_dot(jnp.where(lower, 1.0, 0.0).astype(BF16), chosen_b), axis=1, keepdims=True)
    base = prefix + tile_start
    zeros4 = jnp.zeros((V7X_SUBLANES - TOP_K, ts), F32)
    pos = [jnp.sum(jnp.where(sel, base, 0.0), axis=0, keepdims=True) for sel in sels]
    pos_ref[...] = jnp.concatenate(pos + [zeros4], axis=0).astype(I32)
    gate_ref[...] = jnp.concatenate([e / denom for e in exps] + [zeros4], axis=0)
    counts_ref[0] = jnp.broadcast_to(jnp.sum(chosen, axis=1, keepdims=True), (N_EXPERTS, V7X_LANES))


def _const_spec(shape):
    nd = len(shape)
    return pl.BlockSpec(shape, lambda *_: (0,) * nd, pipeline_mode=pl.Buffered(1))


def _mixer_call(x, lw):
    B, S, D = x.shape
    ts = MIX_TS
    n_s = S // ts
    T = B * S
    tok_blk = lambda b, s: (0, b * n_s + s)
    consts = [lw["mix_norm"], lw["w_in"], lw["b_in"], lw["pool_w"], lw["pool_scale"], lw["conv_w"],
              lw["conv_b"], lw["conv_norm_g"], lw["conv_norm_b"], lw["sgu_norm_g"], lw["sgu_norm_b"],
              lw["sgu_w"], lw["sgu_bt"], lw["branch_w"], lw["branch_b"], lw["w_out"], lw["moe_norm"],
              lw["router_wt"], lw["router_b"]]
    in_specs = [pl.BlockSpec((1, ts, D), lambda b, s: (b, s, 0))] + [_const_spec(c.shape) for c in consts]
    out_shape = (
        jax.ShapeDtypeStruct((B, S, D), F32),
        jax.ShapeDtypeStruct((T, D), BF16),
        jax.ShapeDtypeStruct((V7X_SUBLANES, T), I32),
        jax.ShapeDtypeStruct((V7X_SUBLANES, T), F32),
        jax.ShapeDtypeStruct((T // ts, N_EXPERTS, V7X_LANES), F32),
    )
    out_specs = (
        pl.BlockSpec((1, ts, D), lambda b, s: (b, s, 0)),
        pl.BlockSpec((ts, D), lambda b, s: (b * n_s + s, 0)),
        pl.BlockSpec((V7X_SUBLANES, ts), tok_blk),
        pl.BlockSpec((V7X_SUBLANES, ts), tok_blk),
        pl.BlockSpec((1, N_EXPERTS, V7X_LANES), lambda b, s: (b * n_s + s, 0, 0)),
    )
    return pl.pallas_call(
        _mixer_kernel,
        grid=(B, n_s),
        in_specs=in_specs,
        out_specs=out_specs,
        out_shape=out_shape,
        scratch_shapes=[pltpu.VMEM((HALO + ts, BRANCH_WIDTH), F32),
                        pltpu.VMEM((HALO + ts, BRANCH_WIDTH), F32),
                        pltpu.VMEM((V7X_SUBLANES - 1, HALO + ts - V7X_SUBLANES, BRANCH_WIDTH), F32)],
        compiler_params=pltpu.CompilerParams(
            dimension_semantics=("arbitrary", "arbitrary"),
            vmem_limit_bytes=V7X_VMEM_BYTES - 8 * 1024 * 1024),
        name="mixer",
    )(x, *consts)


def _row_span(first_row, n_rows):
    return pl.ds(pl.multiple_of(first_row * ROW_TILES, ROW_TILES),
                 pl.multiple_of(n_rows * ROW_TILES, ROW_TILES))


def _dispatch_kernel(run_src, run_n, run_dst, pad_dst, pad_n, n_used_ref, h2b_ref, pos_ref, xs_hbm,
                     stage, zero_rows, run_sem, fill_sem, *, n_blocks):
    i = pl.program_id(0)
    n_tiles = pl.num_programs(0)
    slot = i % 2
    block_rows = MOE_BLOCK * ROW_TILES

    def run_copy(tile, e, buf):
        k = tile * N_EXPERTS + e
        return pltpu.make_async_copy(stage.at[buf, _row_span(run_src[k], run_n[k])],
                                     xs_hbm.at[_row_span(run_dst[k], run_n[k])], run_sem.at[buf])

    def pad_copy(e):
        return pltpu.make_async_copy(zero_rows.at[_row_span(0, pad_n[e])],
                                     xs_hbm.at[_row_span(pad_dst[e], pad_n[e])], fill_sem)

    def idle_block_copy(blk):
        return pltpu.make_async_copy(
            zero_rows, xs_hbm.at[pl.ds(pl.multiple_of(blk * block_rows, block_rows), block_rows)], fill_sem)

    @pl.when(i == 0)
    def _():
        zero_rows[...] = jnp.zeros_like(zero_rows)

        def start_idle(blk, c):
            idle_block_copy(blk).start()
            return c

        def wait_idle(blk, c):
            idle_block_copy(blk).wait()
            return c

        for e in range(N_EXPERTS):
            pad_copy(e).start()
        lax.fori_loop(n_used_ref[0], n_blocks, start_idle, 0)
        for e in range(N_EXPERTS):
            pad_copy(e).wait()
        lax.fori_loop(n_used_ref[0], n_blocks, wait_idle, 0)

    @pl.when(i >= 2)
    def _():
        for e in range(N_EXPERTS):
            run_copy(i - 2, e, slot).wait()

    h2b = h2b_ref[...]
    for a0 in range(0, N_SORTED, MOE_BLOCK):
        a_iota = lax.broadcasted_iota(I32, (MOE_BLOCK, MIX_TS), 0) + a0
        onehot = jnp.zeros((MOE_BLOCK, MIX_TS), F32)
        for k in range(TOP_K):
            onehot = jnp.where(a_iota == pos_ref[k:k + 1, :], 1.0, onehot)
        rows = _dot(onehot.astype(BF16), h2b)
        for j in range(ROW_TILES):
            stage[slot, pl.ds(a0 * ROW_TILES + j, MOE_BLOCK, stride=ROW_TILES), :] = (
                rows[:, j * V7X_LANES:(j + 1) * V7X_LANES])

    for e in range(N_EXPERTS):
        run_copy(i, e, slot).start()

    @pl.when(i == n_tiles - 1)
    def _():
        @pl.when(i >= 1)
        def _():
            for e in range(N_EXPERTS):
                run_copy(i - 1, e, 1 - slot).wait()
        for e in range(N_EXPERTS):
            run_copy(i, e, slot).wait()


def _dispatch_call(h2b, pos8, tables, n_blocks):
    T, D = h2b.shape
    n_tiles = T // MIX_TS
    grid_spec = pltpu.PrefetchScalarGridSpec(
        num_scalar_prefetch=6,
        grid=(n_tiles,),
        in_specs=[pl.BlockSpec((MIX_TS, D), lambda i, *_: (i, 0)),
                  pl.BlockSpec((V7X_SUBLANES, MIX_TS), lambda i, *_: (0, i))],
        out_specs=pl.BlockSpec(memory_space=pl.ANY),
        scratch_shapes=[pltpu.VMEM((2, N_SORTED * ROW_TILES, V7X_LANES), F32),
                        pltpu.VMEM((MOE_BLOCK * ROW_TILES, V7X_LANES), F32),
                        pltpu.SemaphoreType.DMA((2,)),
                        pltpu.SemaphoreType.DMA(())],
    )
    return pl.pallas_call(
        functools.partial(_dispatch_kernel, n_blocks=n_blocks),
        grid_spec=grid_spec,
        out_shape=jax.ShapeDtypeStruct((n_blocks * MOE_BLOCK * ROW_TILES, V7X_LANES), F32),
        compiler_params=pltpu.CompilerParams(
            dimension_semantics=("arbitrary",),
            vmem_limit_bytes=V7X_VMEM_BYTES - 16 * 1024 * 1024),
        name="dispatch",
    )(tables["run_src"], tables["run_n"], tables["run_dst"], tables["pad_dst"], tables["pad_n"],
      tables["n_used"], h2b, pos8)


def _moe_kernel(be_ref, n_used_ref, xs_ref, wup_ref, bup_ref, wdn_ref, bdn_ref, ys_ref, wup_bf, wdn_bf):
    blk = pl.program_id(0)
    active = blk < n_used_ref[0]
    prev = jnp.maximum(blk - 1, 0)
    new_expert = (blk == 0) | (be_ref[blk] != be_ref[prev])

    @pl.when(active & new_expert)
    def _():
        wup_bf[...] = wup_ref[0, 0].astype(BF16)
        wdn_bf[...] = wdn_ref[0, 0].astype(BF16)

    @pl.when(active)
    def _():
        xs = jnp.concatenate(
            [xs_ref[pl.ds(j, MOE_BLOCK, stride=ROW_TILES), :] for j in range(ROW_TILES)],
            axis=-1).astype(BF16)
        gu = _dot(xs, wup_bf[...]) + bup_ref[0, 0]
        x_glu = jnp.minimum(gu[:, :D_FF], SWIGLU_LIMIT)
        x_lin = jnp.clip(gu[:, D_FF:], -SWIGLU_LIMIT, SWIGLU_LIMIT)
        act = x_glu * _sigmoid(SWIGLU_ALPHA * x_glu) * (x_lin + 1.0)
        y = _dot(act.astype(BF16), wdn_bf[...]) + bdn_ref[0, 0]
        for j in range(ROW_TILES):
            ys_ref[pl.ds(j, MOE_BLOCK, stride=ROW_TILES), :] = y[:, j * V7X_LANES:(j + 1) * V7X_LANES]

    @pl.when(jnp.logical_not(active))
    def _():
        ys_ref[...] = jnp.zeros_like(ys_ref)


def _moe_call(layer, xs, block_e, n_used, w_up, b_up, w_down, b_down):
    n_blocks = block_e.shape[0]
    block_rows = MOE_BLOCK * ROW_TILES
    w_idx = lambda b, be, nu: (layer, be[b], 0, 0)
    grid_spec = pltpu.PrefetchScalarGridSpec(
        num_scalar_prefetch=2,
        grid=(n_blocks,),
        in_specs=[
            pl.BlockSpec((block_rows, V7X_LANES), lambda b, be, nu: (b, 0)),
            pl.BlockSpec((1, 1, D_MODEL, 2 * D_FF), w_idx),
            pl.BlockSpec((1, 1, 1, 2 * D_FF), w_idx),
            pl.BlockSpec((1, 1, D_FF, D_MODEL), w_idx),
            pl.BlockSpec((1, 1, 1, D_MODEL), w_idx),
        ],
        out_specs=pl.BlockSpec((block_rows, V7X_LANES), lambda b, be, nu: (b, 0)),
        scratch_shapes=[pltpu.VMEM((D_MODEL, 2 * D_FF), BF16), pltpu.VMEM((D_FF, D_MODEL), BF16)],
    )
    return pl.pallas_call(
        _moe_kernel,
        grid_spec=grid_spec,
        out_shape=jax.ShapeDtypeStruct(xs.shape, F32),
        compiler_params=pltpu.CompilerParams(
            dimension_semantics=("arbitrary",),
            vmem_limit_bytes=V7X_VMEM_BYTES - 8 * 1024 * 1024),
        name="moe",
    )(block_e, n_used, xs, w_up, b_up, w_down, b_down)


def _combine_kernel(run_src, run_n, run_dst, x1_ref, pos_ref, gate_ref, p_ref, ple_g_ref, gate_w_ref,
                    proj_w_ref, fin_g_ref, ys_hbm, out_ref, stage, run_sem, *, last):
    i = pl.program_id(0)
    n_tiles = pl.num_programs(0)
    slot = i % 2

    def run_copy(tile, e, buf):
        k = tile * N_EXPERTS + e
        return pltpu.make_async_copy(ys_hbm.at[_row_span(run_dst[k], run_n[k])],
                                     stage.at[buf, _row_span(run_src[k], run_n[k])], run_sem.at[buf])

    @pl.when(i == 0)
    def _():
        for e in range(N_EXPERTS):
            run_copy(0, e, 0).start()

    @pl.when(i + 1 < n_tiles)
    def _():
        for e in range(N_EXPERTS):
            run_copy(i + 1, e, 1 - slot).start()

    for e in range(N_EXPERTS):
        run_copy(i, e, slot).wait()

    x2 = x1_ref[...]
    chunk = MIX_TS
    for a0 in range(0, N_SORTED, chunk):
        a_iota = lax.broadcasted_iota(I32, (MIX_TS, chunk), 1) + a0
        weights = jnp.zeros((MIX_TS, chunk), F32)
        for k in range(TOP_K):
            weights = jnp.where(a_iota == pos_ref[:, k:k + 1], gate_ref[:, k:k + 1], weights)
        y_sorted = jnp.concatenate(
            [stage[slot, pl.ds(a0 * ROW_TILES + j, chunk, stride=ROW_TILES), :] for j in range(ROW_TILES)],
            axis=-1).astype(BF16)
        x2 = x2 + _dot(weights.astype(BF16), y_sorted)
    h3 = _rms_norm(x2, ple_g_ref[...]).astype(BF16)
    g = _sigmoid(_dot(h3, gate_w_ref[...]))
    pp = _dot(p_ref[...].astype(BF16), proj_w_ref[...])
    x3 = x2 + g * pp
    if last:
        x3 = _rms_norm(x3, fin_g_ref[...])
    out_ref[...] = x3


def _combine_call(layer, x1, ys, pos_tm, gate_tm, p2, tables, lw, fin_g, last):
    T, D = x1.shape
    n_tiles = T // MIX_TS
    consts = [lw["ple_norm"], lw["ple_gate_w"], lw["ple_proj_w"], fin_g]
    grid_spec = pltpu.PrefetchScalarGridSpec(
        num_scalar_prefetch=3,
        grid=(n_tiles,),
        in_specs=[pl.BlockSpec((MIX_TS, D), lambda i, *_: (i, 0)),
                  pl.BlockSpec((MIX_TS, V7X_SUBLANES), lambda i, *_: (i, 0)),
                  pl.BlockSpec((MIX_TS, V7X_SUBLANES), lambda i, *_: (i, 0)),
                  pl.BlockSpec((MIX_TS, PLE_DIM), lambda i, *_: (layer * n_tiles + i, 0))]
        + [_const_spec(c.shape) for c in consts] + [pl.BlockSpec(memory_space=pl.ANY)],
        out_specs=pl.BlockSpec((MIX_TS, D), lambda i, *_: (i, 0)),
        scratch_shapes=[pltpu.VMEM((2, N_SORTED * ROW_TILES, V7X_LANES), F32),
                        pltpu.SemaphoreType.DMA((2,))],
    )
    return pl.pallas_call(
        functools.partial(_combine_kernel, last=last),
        grid_spec=grid_spec,
        out_shape=jax.ShapeDtypeStruct((T, D), F32),
        compiler_params=pltpu.CompilerParams(
            dimension_semantics=("arbitrary",),
            vmem_limit_bytes=V7X_VMEM_BYTES - 16 * 1024 * 1024),
        name="combine",
    )(tables["run_src"], tables["run_n"], tables["run_dst"], x1, pos_tm, gate_tm, p2, *consts, ys)


def _routing_tables(tile_counts, n_blocks):
    c = tile_counts[:, :, 0].astype(I32)
    counts = jnp.sum(c, axis=0)
    padded = (counts + MOE_BLOCK - 1) // MOE_BLOCK * MOE_BLOCK
    pad_end = jnp.cumsum(padded)
    pad_start = pad_end - padded
    run_dst = pad_start[None, :] + jnp.cumsum(c, axis=0) - c
    run_src = jnp.cumsum(c, axis=1) - c
    n_used = pad_end[-1] // MOE_BLOCK
    blk0 = jnp.arange(n_blocks, dtype=I32) * MOE_BLOCK
    be = jnp.minimum(jnp.sum(blk0[:, None] >= pad_end[None, :], axis=1), N_EXPERTS - 1)
    be = be[jnp.minimum(jnp.arange(n_blocks), n_used - 1)]
    return {
        "run_src": run_src.reshape(-1).astype(I32), "run_n": c.reshape(-1),
        "run_dst": run_dst.reshape(-1).astype(I32),
        "pad_dst": (pad_start + counts).astype(I32), "pad_n": (padded - counts).astype(I32),
        "n_used": n_used.reshape(1).astype(I32), "block_e": be.astype(I32),
    }


def kernel(x, p, mix_norm, w_in, b_in, pool_w, pool_scale, conv_w, conv_b, conv_norm_g, conv_norm_b,
           sgu_norm_g, sgu_norm_b, sgu_w, sgu_b, branch_w, branch_b, w_out, moe_norm, router_w,
           router_b, expert_w_up, expert_b_up, expert_w_down, expert_b_down, ple_norm, ple_gate_w,
           ple_proj_w, final_norm):
    B, S, D = x.shape
    T = B * S
    depth = w_in.shape[0]
    assert D == D_MODEL and S % MIX_TS == 0 and w_in.shape[2] == IN_WIDTH
    n_blocks = -(-(T * TOP_K) // MOE_BLOCK) + N_EXPERTS
    row = lambda a: a.reshape(1, -1)
    p2 = p.reshape(depth * T, PLE_DIM)
    b_up4 = expert_b_up[:, :, None, :]
    b_down4 = expert_b_down[:, :, None, :]
    for i in range(depth):
        lw = {
            "mix_norm": row(mix_norm[i]), "w_in": w_in[i].astype(BF16), "b_in": row(b_in[i]),
            "pool_w": pool_w[i].astype(BF16), "pool_scale": row(pool_scale[i]),
            "conv_w": conv_w[i], "conv_b": row(conv_b[i]),
            "conv_norm_g": row(conv_norm_g[i]), "conv_norm_b": row(conv_norm_b[i]),
            "sgu_norm_g": row(sgu_norm_g[i]), "sgu_norm_b": row(sgu_norm_b[i]),
            "sgu_w": sgu_w[i], "sgu_bt": sgu_b[i].T,
            "branch_w": branch_w[i].astype(BF16), "branch_b": branch_b[i],
            "w_out": w_out[i].astype(BF16), "moe_norm": row(moe_norm[i]),
            "router_wt": router_w[i].T, "router_b": router_b[i].reshape(-1, 1),
            "ple_norm": row(ple_norm[i]), "ple_gate_w": ple_gate_w[i].astype(BF16),
            "ple_proj_w": ple_proj_w[i].astype(BF16),
        }
        x1, h2b, pos8, gate8, tile_counts = _mixer_call(x, lw)
        tables = _routing_tables(tile_counts, n_blocks)
        xs = _dispatch_call(h2b, pos8, tables, n_blocks)
        ys = _moe_call(i, xs, tables["block_e"], tables["n_used"], expert_w_up, b_up4,
                       expert_w_down, b_down4)
        x = _combine_call(i, x1.reshape(T, D), ys, pos8.T, gate8.T, p2, tables, lw,
                          row(final_norm), last=(i == depth - 1)).reshape(B, S, D)
    return x
```

```python
import functools

import jax
import jax.numpy as jnp
from jax import lax
from jax.experimental import pallas as pl
from jax.experimental.pallas import tpu as pltpu

F32 = jnp.float32
BF16 = jnp.bfloat16
I32 = jnp.int32

D_MODEL = 1024
POOL_WINDOWS = (2, 4, 8, 16)
POOL_CH = 128
BRANCH_WIDTH = 512
CONV_K = 31
SGU_CHUNK = 128
SGU_HEADS = 4
N_BRANCH = 3
N_EXPERTS = 32
TOP_K = 4
D_FF = 1024
SWIGLU_LIMIT = 7.0
SWIGLU_ALPHA = 1.702
MOE_BLOCK = 256
PLE_DIM = 256
EPS = 1e-6

V7X_SUBLANES = 8
V7X_LANES = 128
ROW_TILES = D_MODEL // V7X_LANES
V7X_VMEM_BYTES = 64 * 1024 * 1024

MIX_TS = 512
HALO = 32
CONV_ROWS = 32
N_STAGE = TOP_K * MIX_TS + N_EXPERTS * V7X_SUBLANES
assert N_STAGE % MOE_BLOCK == 0

C_POOL = 0
C_CONV = C_POOL + BRANCH_WIDTH
C_SGU_U = C_CONV + 2 * BRANCH_WIDTH
C_SGU_V = C_SGU_U + BRANCH_WIDTH
C_GATE = C_SGU_V + BRANCH_WIDTH
IN_WIDTH = C_GATE + N_BRANCH * D_MODEL


def _rms_norm(x, g):
    return x * lax.rsqrt(jnp.mean(x * x, axis=-1, keepdims=True) + EPS) * g


def _layer_norm(x, g, b):
    mu = jnp.mean(x, axis=-1, keepdims=True)
    xc = x - mu
    var = jnp.mean(xc * xc, axis=-1, keepdims=True)
    return xc * lax.rsqrt(var + EPS) * g + b


def _sigmoid(x):
    return 1.0 / (1.0 + jnp.exp(-x))


def _dot(a, b):
    return jnp.dot(a, b, preferred_element_type=F32)


def _mixer_kernel(x_ref, mix_g_ref, w_in_ref, b_in_ref, pool_w_ref, pool_scale_ref, conv_w_ref,
                  conv_b_ref, cn_g_ref, cn_b_ref, sn_g_ref, sn_b_ref, sgu_w_ref, sgu_bt_ref,
                  branch_w_ref, branch_b_ref, w_out_ref, moe_g_ref, rw_t_ref, rb_ref,
                  x1_ref, h2b_ref, pos_ref, gate_ref, counts_ref,
                  pool_hist, conv_hist, conv_shift):
    ts = MIX_TS
    s = pl.program_id(1)

    @pl.when(s == 0)
    def _():
        pool_hist[0:HALO, :] = jnp.zeros((HALO, BRANCH_WIDTH), F32)
        conv_hist[0:HALO, :] = jnp.zeros((HALO, BRANCH_WIDTH), F32)

    x = x_ref[0]
    h = _rms_norm(x, mix_g_ref[...]).astype(BF16)

    def in_proj(c0, width):
        return _dot(h, w_in_ref[:, c0:c0 + width]) + b_in_ref[:, c0:c0 + width]

    za = in_proj(C_POOL, BRANCH_WIDTH)
    pool_hist[HALO:HALO + ts, :] = za
    row = lax.broadcasted_iota(I32, (ts, 1), 0) + s * ts
    mixed = []
    for g, w in enumerate(POOL_WINDOWS):
        c0 = g * POOL_CH
        cur = za[:, c0:c0 + POOL_CH]
        acc = cur
        for j in range(1, w):
            acc = acc + pool_hist[HALO - j:HALO - j + ts, c0:c0 + POOL_CH]
        count = jnp.minimum(row + 1, w).astype(F32)
        pooled = (acc / count - cur).astype(BF16)
        mixed.append(_dot(pooled, pool_w_ref[g]))
    ya = (jnp.concatenate(mixed, axis=-1) * pool_scale_ref[...]).astype(BF16)
    pool_hist[0:HALO, :] = pool_hist[ts:ts + HALO, :]

    zb = in_proj(C_CONV, 2 * BRANCH_WIDTH)
    conv_hist[HALO:HALO + ts, :] = zb[:, :BRANCH_WIDTH] * _sigmoid(zb[:, BRANCH_WIDTH:])
    n_shift_rows = HALO + ts - V7X_SUBLANES
    for sft in range(1, V7X_SUBLANES):
        conv_shift[sft - 1] = conv_hist[sft:sft + n_shift_rows, :]
    yb_parts = []
    for r0 in range(0, ts, CONV_ROWS):
        acc = jnp.zeros((CONV_ROWS, BRANCH_WIDTH), F32) + conv_b_ref[...]
        for k in range(CONV_K):
            off = HALO - (CONV_K - 1) + k + r0
            base, sft = off - off % V7X_SUBLANES, off % V7X_SUBLANES
            if sft == 0:
                window = conv_hist[base:base + CONV_ROWS, :]
            else:
                window = conv_shift[sft - 1, base:base + CONV_ROWS, :]
            acc = acc + conv_w_ref[k:k + 1, :] * window
        yb_rows = _layer_norm(acc, cn_g_ref[...], cn_b_ref[...])
        yb_parts.append((yb_rows * _sigmoid(yb_rows)).astype(BF16))
    yb = jnp.concatenate(yb_parts, axis=0)
    conv_hist[0:HALO, :] = conv_hist[ts:ts + HALO, :]

    zu = in_proj(C_SGU_U, BRANCH_WIDTH)
    zv = in_proj(C_SGU_V, BRANCH_WIDTH)
    v = _layer_norm(zv, sn_g_ref[...], sn_b_ref[...]).astype(BF16)
    tri = (lax.broadcasted_iota(I32, (SGU_CHUNK, SGU_CHUNK), 0)
           >= lax.broadcasted_iota(I32, (SGU_CHUNK, SGU_CHUNK), 1))
    chunks = []
    for c in range(ts // SGU_CHUNK):
        heads = []
        for hd in range(SGU_HEADS):
            w_tri = jnp.where(tri, sgu_w_ref[hd], 0.0).astype(BF16)
            vv = v[c * SGU_CHUNK:(c + 1) * SGU_CHUNK, hd * 128:(hd + 1) * 128]
            heads.append(_dot(w_tri, vv) + sgu_bt_ref[:, hd:hd + 1])
        chunks.append(jnp.concatenate(heads, axis=-1))
    yc = (zu * jnp.concatenate(chunks, axis=0)).astype(BF16)

    merged = jnp.zeros((ts, D_MODEL), F32)
    for k, yk in enumerate((ya, yb, yc)):
        proj = _dot(yk, branch_w_ref[k]) + branch_b_ref[k:k + 1, :]
        zg = in_proj(C_GATE + k * D_MODEL, D_MODEL)
        merged = merged + _sigmoid(zg) * proj
    x1 = x + _dot(merged.astype(BF16), w_out_ref[...])
    x1_ref[0] = x1

    h2 = _rms_norm(x1, moe_g_ref[...])
    h2b_ref[...] = h2.astype(BF16)
    logits = lax.dot_general(rw_t_ref[...], h2, (((1,), (1,)), ((), ())),
                             precision=lax.Precision.HIGHEST,
                             preferred_element_type=F32) + rb_ref[...]
    e_iota = lax.broadcasted_iota(I32, (N_EXPERTS, ts), 0).astype(F32)
    vals = logits
    top_v, sels = [], []
    for _k in range(TOP_K):
        m = jnp.max(vals, axis=0, keepdims=True)
        idx = jnp.min(jnp.where(vals == m, e_iota, float(N_EXPERTS)), axis=0, keepdims=True)
        sel = e_iota == idx
        vals = jnp.where(sel, -jnp.inf, vals)
        top_v.append(m)
        sels.append(sel)
    exps = [jnp.exp(tv - top_v[0]) for tv in top_v]
    denom = exps[0] + exps[1] + exps[2] + exps[3]
    chosen = jnp.zeros((N_EXPERTS, ts), F32)
    for sel in sels:
        chosen = chosen + jnp.where(sel, 1.0, 0.0)
    chosen_b = chosen.astype(BF16)
    before = (lax.broadcasted_iota(I32, (ts, ts), 0) < lax.broadcasted_iota(I32, (ts, ts), 1))
    prefix = _dot(chosen_b, jnp.where(before, 1.0, 0.0).astype(BF16))
    count = jnp.sum(chosen, axis=1, keepdims=True)
    run_rows = jnp.floor((count + (V7X_SUBLANES - 1)) * (1.0 / V7X_SUBLANES)) * V7X_SUBLANES
    lower = (lax.broadcasted_iota(I32, (N_EXPERTS, N_EXPERTS), 0)
             > lax.broadcasted_iota(I32, (N_EXPERTS, N_EXPERTS), 1))
    run_start = _dot(jnp.where(lower, 1.0, 0.0).astype(BF16),
                     jnp.broadcast_to(run_rows, (N_EXPERTS, V7X_LANES)).astype(BF16))[:, 0:1]
    base = prefix + run_start
    zeros4 = jnp.zeros((V7X_SUBLANES - TOP_K, ts), F32)
    pos = [jnp.sum(jnp.where(sel, base, 0.0), axis=0, keepdims=True) for sel in sels]
    pos_ref[...] = jnp.concatenate(pos + [zeros4], axis=0).astype(I32)
    gate_ref[...] = jnp.concatenate([e / denom for e in exps] + [zeros4], axis=0)
    counts_ref[0] = jnp.broadcast_to(count, (N_EXPERTS, V7X_LANES))


def _const_spec(shape):
    nd = len(shape)
    return pl.BlockSpec(shape, lambda *_: (0,) * nd, pipeline_mode=pl.Buffered(1))


def _mixer_call(x, lw):
    B, S, D = x.shape
    ts = MIX_TS
    n_s = S // ts
    T = B * S
    tok_blk = lambda b, s: (0, b * n_s + s)
    consts = [lw["mix_norm"], lw["w_in"], lw["b_in"], lw["pool_w"], lw["pool_scale"], lw["conv_w"],
              lw["conv_b"], lw["conv_norm_g"], lw["conv_norm_b"], lw["sgu_norm_g"], lw["sgu_norm_b"],
              lw["sgu_w"], lw["sgu_bt"], lw["branch_w"], lw["branch_b"], lw["w_out"], lw["moe_norm"],
              lw["router_wt"], lw["router_b"]]
    in_specs = [pl.BlockSpec((1, ts, D), lambda b, s: (b, s, 0))] + [_const_spec(c.shape) for c in consts]
    out_shape = (
        jax.ShapeDtypeStruct((B, S, D), F32),
        jax.ShapeDtypeStruct((T, D), BF16),
        jax.ShapeDtypeStruct((V7X_SUBLANES, T), I32),
        jax.ShapeDtypeStruct((V7X_SUBLANES, T), F32),
        jax.ShapeDtypeStruct((T // ts, N_EXPERTS, V7X_LANES), F32),
    )
    out_specs = (
        pl.BlockSpec((1, ts, D), lambda b, s: (b, s, 0)),
        pl.BlockSpec((ts, D), lambda b, s: (b * n_s + s, 0)),
        pl.BlockSpec((V7X_SUBLANES, ts), tok_blk),
        pl.BlockSpec((V7X_SUBLANES, ts), tok_blk),
        pl.BlockSpec((1, N_EXPERTS, V7X_LANES), lambda b, s: (b * n_s + s, 0, 0)),
    )
    return pl.pallas_call(
        _mixer_kernel,
        grid=(B, n_s),
        in_specs=in_specs,
        out_specs=out_specs,
        out_shape=out_shape,
        scratch_shapes=[pltpu.VMEM((HALO + ts, BRANCH_WIDTH), F32),
                        pltpu.VMEM((HALO + ts, BRANCH_WIDTH), F32),
                        pltpu.VMEM((V7X_SUBLANES - 1, HALO + ts - V7X_SUBLANES, BRANCH_WIDTH), F32)],
        compiler_params=pltpu.CompilerParams(
            dimension_semantics=("arbitrary", "arbitrary"),
            vmem_limit_bytes=V7X_VMEM_BYTES - 8 * 1024 * 1024),
        name="mixer",
    )(x, *consts)


def _row_span(first_row, n_rows):
    return pl.ds(pl.multiple_of(first_row, V7X_SUBLANES), pl.multiple_of(n_rows, V7X_SUBLANES))


def _dispatch_kernel(run_src, run_n, run_dst, pad_dst, pad_n, n_used_ref, h2b_ref, pos_ref, xs_hbm,
                     stage, zero_rows, run_sem, fill_sem, *, n_blocks):
    i = pl.program_id(0)
    n_tiles = pl.num_programs(0)
    slot = i % 2

    def run_copy(tile, e, buf):
        k = tile * N_EXPERTS + e
        return pltpu.make_async_copy(stage.at[buf, _row_span(run_src[k], run_n[k])],
                                     xs_hbm.at[_row_span(run_dst[k], run_n[k])], run_sem.at[buf])

    def pad_copy(e):
        return pltpu.make_async_copy(zero_rows.at[_row_span(0, pad_n[e])],
                                     xs_hbm.at[_row_span(pad_dst[e], pad_n[e])], fill_sem)

    def idle_block_copy(blk):
        return pltpu.make_async_copy(
            zero_rows, xs_hbm.at[pl.ds(pl.multiple_of(blk * MOE_BLOCK, MOE_BLOCK), MOE_BLOCK)], fill_sem)

    @pl.when(i == 0)
    def _():
        zero_rows[...] = jnp.zeros_like(zero_rows)

        def start_idle(blk, c):
            idle_block_copy(blk).start()
            return c

        def wait_idle(blk, c):
            idle_block_copy(blk).wait()
            return c

        for e in range(N_EXPERTS):
            pad_copy(e).start()
        lax.fori_loop(n_used_ref[0], n_blocks, start_idle, 0)
        for e in range(N_EXPERTS):
            pad_copy(e).wait()
        lax.fori_loop(n_used_ref[0], n_blocks, wait_idle, 0)

    @pl.when(i >= 2)
    def _():
        for e in range(N_EXPERTS):
            run_copy(i - 2, e, slot).wait()

    h2b = h2b_ref[...]
    for a0 in range(0, N_STAGE, MOE_BLOCK):
        a_iota = lax.broadcasted_iota(I32, (MOE_BLOCK, MIX_TS), 0) + a0
        onehot = jnp.zeros((MOE_BLOCK, MIX_TS), F32)
        for k in range(TOP_K):
            onehot = jnp.where(a_iota == pos_ref[k:k + 1, :], 1.0, onehot)
        stage[slot, a0:a0 + MOE_BLOCK, :] = _dot(onehot.astype(BF16), h2b)

    for e in range(N_EXPERTS):
        run_copy(i, e, slot).start()

    @pl.when(i == n_tiles - 1)
    def _():
        @pl.when(i >= 1)
        def _():
            for e in range(N_EXPERTS):
                run_copy(i - 1, e, 1 - slot).wait()
        for e in range(N_EXPERTS):
            run_copy(i, e, slot).wait()


def _dispatch_call(h2b, pos8, tables, n_blocks):
    T, D = h2b.shape
    n_tiles = T // MIX_TS
    grid_spec = pltpu.PrefetchScalarGridSpec(
        num_scalar_prefetch=6,
        grid=(n_tiles,),
        in_specs=[pl.BlockSpec((MIX_TS, D), lambda i, *_: (i, 0)),
                  pl.BlockSpec((V7X_SUBLANES, MIX_TS), lambda i, *_: (0, i))],
        out_specs=pl.BlockSpec(memory_space=pl.ANY),
        scratch_shapes=[pltpu.VMEM((2, N_STAGE, D), F32),
                        pltpu.VMEM((MOE_BLOCK, D), F32),
                        pltpu.SemaphoreType.DMA((2,)),
                        pltpu.SemaphoreType.DMA(())],
    )
    return pl.pallas_call(
        functools.partial(_dispatch_kernel, n_blocks=n_blocks),
        grid_spec=grid_spec,
        out_shape=jax.ShapeDtypeStruct((n_blocks * MOE_BLOCK, D), F32),
        compiler_params=pltpu.CompilerParams(
            dimension_semantics=("arbitrary",),
            vmem_limit_bytes=V7X_VMEM_BYTES - 16 * 1024 * 1024),
        name="dispatch",
    )(tables["run_src"], tables["run_n"], tables["run_dst"], tables["pad_dst"], tables["pad_n"],
      tables["n_used"], h2b, pos8)


def _moe_kernel(be_ref, n_used_ref, xs_ref, wup_ref, bup_ref, wdn_ref, bdn_ref, ys_ref, wup_bf, wdn_bf):
    blk = pl.program_id(0)
    active = blk < n_used_ref[0]
    prev = jnp.maximum(blk - 1, 0)
    new_expert = (blk == 0) | (be_ref[blk] != be_ref[prev])

    @pl.when(active & new_expert)
    def _():
        wup_bf[...] = wup_ref[0, 0].astype(BF16)
        wdn_bf[...] = wdn_ref[0, 0].astype(BF16)

    @pl.when(active)
    def _():
        gu = _dot(xs_ref[...].astype(BF16), wup_bf[...]) + bup_ref[0, 0]
        x_glu = jnp.minimum(gu[:, :D_FF], SWIGLU_LIMIT)
        x_lin = jnp.clip(gu[:, D_FF:], -SWIGLU_LIMIT, SWIGLU_LIMIT)
        act = x_glu * _sigmoid(SWIGLU_ALPHA * x_glu) * (x_lin + 1.0)
        ys_ref[...] = _dot(act.astype(BF16), wdn_bf[...]) + bdn_ref[0, 0]

    @pl.when(jnp.logical_not(active))
    def _():
        ys_ref[...] = jnp.zeros_like(ys_ref)


def _moe_call(layer, xs, block_e, n_used, w_up, b_up, w_down, b_down):
    n_blocks = block_e.shape[0]
    w_idx = lambda b, be, nu: (layer, be[b], 0, 0)
    grid_spec = pltpu.PrefetchScalarGridSpec(
        num_scalar_prefetch=2,
        grid=(n_blocks,),
        in_specs=[
            pl.BlockSpec((MOE_BLOCK, D_MODEL), lambda b, be, nu: (b, 0)),
            pl.BlockSpec((1, 1, D_MODEL, 2 * D_FF), w_idx),
            pl.BlockSpec((1, 1, 1, 2 * D_FF), w_idx),
            pl.BlockSpec((1, 1, D_FF, D_MODEL), w_idx),
            pl.BlockSpec((1, 1, 1, D_MODEL), w_idx),
        ],
        out_specs=pl.BlockSpec((MOE_BLOCK, D_MODEL), lambda b, be, nu: (b, 0)),
        scratch_shapes=[pltpu.VMEM((D_MODEL, 2 * D_FF), BF16), pltpu.VMEM((D_FF, D_MODEL), BF16)],
    )
    return pl.pallas_call(
        _moe_kernel,
        grid_spec=grid_spec,
        out_shape=jax.ShapeDtypeStruct(xs.shape, F32),
        compiler_params=pltpu.CompilerParams(
            dimension_semantics=("arbitrary",),
            vmem_limit_bytes=V7X_VMEM_BYTES - 8 * 1024 * 1024),
        name="moe",
    )(block_e, n_used, xs, w_up, b_up, w_down, b_down)


def _combine_kernel(run_src, run_n, run_dst, x1_ref, pos_ref, gate_ref, p_ref, ple_g_ref, gate_w_ref,
                    proj_w_ref, fin_g_ref, ys_hbm, out_ref, stage, run_sem, *, last):
    i = pl.program_id(0)
    n_tiles = pl.num_programs(0)
    slot = i % 2

    def run_copy(tile, e, buf):
        k = tile * N_EXPERTS + e
        return pltpu.make_async_copy(ys_hbm.at[_row_span(run_dst[k], run_n[k])],
                                     stage.at[buf, _row_span(run_src[k], run_n[k])], run_sem.at[buf])

    @pl.when(i == 0)
    def _():
        stage[...] = jnp.zeros_like(stage)
        for e in range(N_EXPERTS):
            run_copy(0, e, 0).start()

    @pl.when(i + 1 < n_tiles)
    def _():
        for e in range(N_EXPERTS):
            run_copy(i + 1, e, 1 - slot).start()

    for e in range(N_EXPERTS):
        run_copy(i, e, slot).wait()

    x2 = x1_ref[...]
    for a0 in range(0, N_STAGE, MOE_BLOCK):
        a_iota = lax.broadcasted_iota(I32, (MIX_TS, MOE_BLOCK), 1) + a0
        weights = jnp.zeros((MIX_TS, MOE_BLOCK), F32)
        for k in range(TOP_K):
            weights = jnp.where(a_iota == pos_ref[:, k:k + 1], gate_ref[:, k:k + 1], weights)
        x2 = x2 + _dot(weights.astype(BF16), stage[slot, a0:a0 + MOE_BLOCK, :].astype(BF16))
    h3 = _rms_norm(x2, ple_g_ref[...]).astype(BF16)
    g = _sigmoid(_dot(h3, gate_w_ref[...]))
    pp = _dot(p_ref[...].astype(BF16), proj_w_ref[...])
    x3 = x2 + g * pp
    if last:
        x3 = _rms_norm(x3, fin_g_ref[...])
    out_ref[...] = x3


def _combine_call(layer, x1, ys, pos_tm, gate_tm, p2, tables, lw, fin_g, last):
    T, D = x1.shape
    n_tiles = T // MIX_TS
    consts = [lw["ple_norm"], lw["ple_gate_w"], lw["ple_proj_w"], fin_g]
    grid_spec = pltpu.PrefetchScalarGridSpec(
        num_scalar_prefetch=3,
        grid=(n_tiles,),
        in_specs=[pl.BlockSpec((MIX_TS, D), lambda i, *_: (i, 0)),
                  pl.BlockSpec((MIX_TS, V7X_SUBLANES), lambda i, *_: (i, 0)),
                  pl.BlockSpec((MIX_TS, V7X_SUBLANES), lambda i, *_: (i, 0)),
                  pl.BlockSpec((MIX_TS, PLE_DIM), lambda i, *_: (layer * n_tiles + i, 0))]
        + [_const_spec(c.shape) for c in consts] + [pl.BlockSpec(memory_space=pl.ANY)],
        out_specs=pl.BlockSpec((MIX_TS, D), lambda i, *_: (i, 0)),
        scratch_shapes=[pltpu.VMEM((2, N_STAGE, D), F32),
                        pltpu.SemaphoreType.DMA((2,))],
    )
    return pl.pallas_call(
        functools.partial(_combine_kernel, last=last),
        grid_spec=grid_spec,
        out_shape=jax.ShapeDtypeStruct((T, D), F32),
        compiler_params=pltpu.CompilerParams(
            dimension_semantics=("arbitrary",),
            vmem_limit_bytes=V7X_VMEM_BYTES - 16 * 1024 * 1024),
        name="combine",
    )(tables["run_src"], tables["run_n"], tables["run_dst"], x1, pos_tm, gate_tm, p2, *consts, ys)


def _routing_tables(tile_counts, n_blocks):
    c = tile_counts[:, :, 0].astype(I32)
    c = (c + V7X_SUBLANES - 1) // V7X_SUBLANES * V7X_SUBLANES
    counts = jnp.sum(c, axis=0)
    padded = (counts + MOE_BLOCK - 1) // MOE_BLOCK * MOE_BLOCK
    pad_end = jnp.cumsum(padded)
    pad_start = pad_end - padded
    run_dst = pad_start[None, :] + jnp.cumsum(c, axis=0) - c
    run_src = jnp.cumsum(c, axis=1) - c
    n_used = pad_end[-1] // MOE_BLOCK
    blk0 = jnp.arange(n_blocks, dtype=I32) * MOE_BLOCK
    be = jnp.minimum(jnp.sum(blk0[:, None] >= pad_end[None, :], axis=1), N_EXPERTS - 1)
    be = be[jnp.minimum(jnp.arange(n_blocks), n_used - 1)]
    return {
        "run_src": run_src.reshape(-1).astype(I32), "run_n": c.reshape(-1),
        "run_dst": run_dst.reshape(-1).astype(I32),
        "pad_dst": (pad_start + counts).astype(I32), "pad_n": (padded - counts).astype(I32),
        "n_used": n_used.reshape(1).astype(I32), "block_e": be.astype(I32),
    }


def kernel(x, p, mix_norm, w_in, b_in, pool_w, pool_scale, conv_w, conv_b, conv_norm_g, conv_norm_b,
           sgu_norm_g, sgu_norm_b, sgu_w, sgu_b, branch_w, branch_b, w_out, moe_norm, router_w,
           router_b, expert_w_up, expert_b_up, expert_w_down, expert_b_down, ple_norm, ple_gate_w,
           ple_proj_w, final_norm):
    B, S, D = x.shape
    T = B * S
    depth = w_in.shape[0]
    assert D == D_MODEL and S % MIX_TS == 0 and w_in.shape[2] == IN_WIDTH
    max_rows = T * TOP_K + (T // MIX_TS) * N_EXPERTS * (V7X_SUBLANES - 1)
    n_blocks = -(-max_rows // MOE_BLOCK) + N_EXPERTS
    row = lambda a: a.reshape(1, -1)
    p2 = p.reshape(depth * T, PLE_DIM)
    b_up4 = expert_b_up[:, :, None, :]
    b_down4 = expert_b_down[:, :, None, :]
    for i in range(depth):
        lw = {
            "mix_norm": row(mix_norm[i]), "w_in": w_in[i].astype(BF16), "b_in": row(b_in[i]),
            "pool_w": pool_w[i].astype(BF16), "pool_scale": row(pool_scale[i]),
            "conv_w": conv_w[i], "conv_b": row(conv_b[i]),
            "conv_norm_g": row(conv_norm_g[i]), "conv_norm_b": row(conv_norm_b[i]),
            "sgu_norm_g": row(sgu_norm_g[i]), "sgu_norm_b": row(sgu_norm_b[i]),
            "sgu_w": sgu_w[i], "sgu_bt": sgu_b[i].T,
            "branch_w": branch_w[i].astype(BF16), "branch_b": branch_b[i],
            "w_out": w_out[i].astype(BF16), "moe_norm": row(moe_norm[i]),
            "router_wt": router_w[i].T, "router_b": router_b[i].reshape(-1, 1),
            "ple_norm": row(ple_norm[i]), "ple_gate_w": ple_gate_w[i].astype(BF16),
            "ple_proj_w": ple_proj_w[i].astype(BF16),
        }
        x1, h2b, pos8, gate8, tile_counts = _mixer_call(x, lw)
        tables = _routing_tables(tile_counts, n_blocks)
        xs = _dispatch_call(h2b, pos8, tables, n_blocks)
        ys = _moe_call(i, xs, tables["block_e"], tables["n_used"], expert_w_up, b_up4,
                       expert_w_down, b_down4)
        x = _combine_call(i, x1.reshape(T, D), ys, pos8.T, gate8.T, p2, tables, lw,
                          row(final_norm), last=(i == depth - 1)).reshape(B, S, D)
    return x
```

```python
import functools

import jax
import jax.numpy as jnp
from jax import lax
from jax.experimental import pallas as pl
from jax.experimental.pallas import tpu as pltpu

F32 = jnp.float32
BF16 = jnp.bfloat16
I32 = jnp.int32

D_MODEL = 1024
POOL_WINDOWS = (2, 4, 8, 16)
POOL_CH = 128
BRANCH_WIDTH = 512
CONV_K = 31
SGU_CHUNK = 128
SGU_HEADS = 4
N_BRANCH = 3
N_EXPERTS = 32
TOP_K = 4
D_FF = 1024
SWIGLU_LIMIT = 7.0
SWIGLU_ALPHA = 1.702
MOE_BLOCK = 256
PLE_DIM = 256
EPS = 1e-6

V7X_SUBLANES = 8
V7X_LANES = 128
ROW_TILES = D_MODEL // V7X_LANES
V7X_VMEM_BYTES = 64 * 1024 * 1024

MIX_TS = 512
HALO = 32
CONV_ROWS = 32
N_SORTED = TOP_K * MIX_TS

C_POOL = 0
C_CONV = C_POOL + BRANCH_WIDTH
C_SGU_U = C_CONV + 2 * BRANCH_WIDTH
C_SGU_V = C_SGU_U + BRANCH_WIDTH
C_GATE = C_SGU_V + BRANCH_WIDTH
IN_WIDTH = C_GATE + N_BRANCH * D_MODEL


def _rms_norm(x, g):
    return x * lax.rsqrt(jnp.mean(x * x, axis=-1, keepdims=True) + EPS) * g


def _layer_norm(x, g, b):
    mu = jnp.mean(x, axis=-1, keepdims=True)
    xc = x - mu
    var = jnp.mean(xc * xc, axis=-1, keepdims=True)
    return xc * lax.rsqrt(var + EPS) * g + b


def _sigmoid(x):
    return 1.0 / (1.0 + jnp.exp(-x))


def _dot(a, b):
    return jnp.dot(a, b, preferred_element_type=F32)


def _mixer_kernel(x_ref, mix_g_ref, w_in_ref, b_in_ref, pool_w_ref, pool_scale_ref, conv_w_ref,
                  conv_b_ref, cn_g_ref, cn_b_ref, sn_g_ref, sn_b_ref, sgu_w_ref, sgu_bt_ref,
                  branch_w_ref, branch_b_ref, w_out_ref, moe_g_ref, rw_t_ref, rb_ref,
                  x1_ref, h2b_ref, pos_ref, gate_ref, counts_ref,
                  pool_hist, conv_hist, conv_shift):
    ts = MIX_TS
    s = pl.program_id(1)

    @pl.when(s == 0)
    def _():
        pool_hist[0:HALO, :] = jnp.zeros((HALO, BRANCH_WIDTH), F32)
        conv_hist[0:HALO, :] = jnp.zeros((HALO, BRANCH_WIDTH), F32)

    x = x_ref[0]
    h = _rms_norm(x, mix_g_ref[...]).astype(BF16)

    def in_proj(c0, width):
        return _dot(h, w_in_ref[:, c0:c0 + width]) + b_in_ref[:, c0:c0 + width]

    za = in_proj(C_POOL, BRANCH_WIDTH)
    pool_hist[HALO:HALO + ts, :] = za
    row = lax.broadcasted_iota(I32, (ts, 1), 0) + s * ts
    mixed = []
    for g, w in enumerate(POOL_WINDOWS):
        c0 = g * POOL_CH
        cur = za[:, c0:c0 + POOL_CH]
        acc = cur
        for j in range(1, w):
            acc = acc + pool_hist[HALO - j:HALO - j + ts, c0:c0 + POOL_CH]
        count = jnp.minimum(row + 1, w).astype(F32)
        pooled = (acc / count - cur).astype(BF16)
        mixed.append(_dot(pooled, pool_w_ref[g]))
    ya = (jnp.concatenate(mixed, axis=-1) * pool_scale_ref[...]).astype(BF16)
    pool_hist[0:HALO, :] = pool_hist[ts:ts + HALO, :]

    zb = in_proj(C_CONV, 2 * BRANCH_WIDTH)
    conv_hist[HALO:HALO + ts, :] = zb[:, :BRANCH_WIDTH] * _sigmoid(zb[:, BRANCH_WIDTH:])
    n_shift_rows = HALO + ts - V7X_SUBLANES
    for sft in range(1, V7X_SUBLANES):
        conv_shift[sft - 1] = conv_hist[sft:sft + n_shift_rows, :]
    yb_parts = []
    for r0 in range(0, ts, CONV_ROWS):
        acc = jnp.zeros((CONV_ROWS, BRANCH_WIDTH), F32) + conv_b_ref[...]
        for k in range(CONV_K):
            off = HALO - (CONV_K - 1) + k + r0
            base, sft = off - off % V7X_SUBLANES, off % V7X_SUBLANES
            if sft == 0:
                window = conv_hist[base:base + CONV_ROWS, :]
            else:
                window = conv_shift[sft - 1, base:base + CONV_ROWS, :]
            acc = acc + conv_w_ref[k:k + 1, :] * window
        yb_rows = _layer_norm(acc, cn_g_ref[...], cn_b_ref[...])
        yb_parts.append((yb_rows * _sigmoid(yb_rows)).astype(BF16))
    yb = jnp.concatenate(yb_parts, axis=0)
    conv_hist[0:HALO, :] = conv_hist[ts:ts + HALO, :]

    zu = in_proj(C_SGU_U, BRANCH_WIDTH)
    zv = in_proj(C_SGU_V, BRANCH_WIDTH)
    v = _layer_norm(zv, sn_g_ref[...], sn_b_ref[...]).astype(BF16)
    tri = (lax.broadcasted_iota(I32, (SGU_CHUNK, SGU_CHUNK), 0)
           >= lax.broadcasted_iota(I32, (SGU_CHUNK, SGU_CHUNK), 1))
    chunks = []
    for c in range(ts // SGU_CHUNK):
        heads = []
        for hd in range(SGU_HEADS):
            w_tri = jnp.where(tri, sgu_w_ref[hd], 0.0).astype(BF16)
            vv = v[c * SGU_CHUNK:(c + 1) * SGU_CHUNK, hd * 128:(hd + 1) * 128]
            heads.append(_dot(w_tri, vv) + sgu_bt_ref[:, hd:hd + 1])
        chunks.append(jnp.concatenate(heads, axis=-1))
    yc = (zu * jnp.concatenate(chunks, axis=0)).astype(BF16)

    merged = jnp.zeros((ts, D_MODEL), F32)
    for k, yk in enumerate((ya, yb, yc)):
        proj = _dot(yk, branch_w_ref[k]) + branch_b_ref[k:k + 1, :]
        zg = in_proj(C_GATE + k * D_MODEL, D_MODEL)
        merged = merged + _sigmoid(zg) * proj
    x1 = x + _dot(merged.astype(BF16), w_out_ref[...])
    x1_ref[0] = x1

    h2 = _rms_norm(x1, moe_g_ref[...])
    h2b_ref[...] = h2.astype(BF16)
    logits = lax.dot_general(rw_t_ref[...], h2, (((1,), (1,)), ((), ())),
                             precision=lax.Precision.HIGHEST,
                             preferred_element_type=F32) + rb_ref[...]
    e_iota = lax.broadcasted_iota(I32, (N_EXPERTS, ts), 0).astype(F32)
    vals = logits
    top_v, sels = [], []
    for _k in range(TOP_K):
        m = jnp.max(vals, axis=0, keepdims=True)
        idx = jnp.min(jnp.where(vals == m, e_iota, float(N_EXPERTS)), axis=0, keepdims=True)
        sel = e_iota == idx
        vals = jnp.where(sel, -jnp.inf, vals)
        top_v.append(m)
        sels.append(sel)
    exps = [jnp.exp(tv - top_v[0]) for tv in top_v]
    denom = exps[0] + exps[1] + exps[2] + exps[3]
    chosen = jnp.zeros((N_EXPERTS, ts), F32)
    for sel in sels:
        chosen = chosen + jnp.where(sel, 1.0, 0.0)
    chosen_b = chosen.astype(BF16)
    before = (lax.broadcasted_iota(I32, (ts, ts), 0) < lax.broadcasted_iota(I32, (ts, ts), 1))
    prefix = _dot(chosen_b, jnp.where(before, 1.0, 0.0).astype(BF16))
    lower = (lax.broadcasted_iota(I32, (N_EXPERTS, N_EXPERTS), 0)
             > lax.broadcasted_iota(I32, (N_EXPERTS, N_EXPERTS), 1))
    tile_start = jnp.sum(_dot(jnp.where(lower, 1.0, 0.0).astype(BF16), chosen_b), axis=1, keepdims=True)
    base = prefix + tile_start
    zeros4 = jnp.zeros((V7X_SUBLANES - TOP_K, ts), F32)
    pos = [jnp.sum(jnp.where(sel, base, 0.0), axis=0, keepdims=True) for sel in sels]
    pos_ref[...] = jnp.concatenate(pos + [zeros4], axis=0).astype(I32)
    gate_ref[...] = jnp.concatenate([e / denom for e in exps] + [zeros4], axis=0)
    counts_ref[0] = jnp.broadcast_to(jnp.sum(chosen, axis=1, keepdims=True), (N_EXPERTS, V7X_LANES))


def _const_spec(shape):
    nd = len(shape)
    return pl.BlockSpec(shape, lambda *_: (0,) * nd, pipeline_mode=pl.Buffered(1))


def _mixer_call(x, lw):
    B, S, D = x.shape
    ts = MIX_TS
    n_s = S // ts
    T = B * S
    tok_blk = lambda b, s: (0, b * n_s + s)
    consts = [lw["mix_norm"], lw["w_in"], lw["b_in"], lw["pool_w"], lw["pool_scale"], lw["conv_w"],
              lw["conv_b"], lw["conv_norm_g"], lw["conv_norm_b"], lw["sgu_norm_g"], lw["sgu_norm_b"],
              lw["sgu_w"], lw["sgu_bt"], lw["branch_w"], lw["branch_b"], lw["w_out"], lw["moe_norm"],
              lw["router_wt"], lw["router_b"]]
    in_specs = [pl.BlockSpec((1, ts, D), lambda b, s: (b, s, 0))] + [_const_spec(c.shape) for c in consts]
    out_shape = (
        jax.ShapeDtypeStruct((B, S, D), F32),
        jax.ShapeDtypeStruct((T, D), BF16),
        jax.ShapeDtypeStruct((V7X_SUBLANES, T), I32),
        jax.ShapeDtypeStruct((V7X_SUBLANES, T), F32),
        jax.ShapeDtypeStruct((T // ts, N_EXPERTS, V7X_LANES), F32),
    )
    out_specs = (
        pl.BlockSpec((1, ts, D), lambda b, s: (b, s, 0)),
        pl.BlockSpec((ts, D), lambda b, s: (b * n_s + s, 0)),
        pl.BlockSpec((V7X_SUBLANES, ts), tok_blk),
        pl.BlockSpec((V7X_SUBLANES, ts), tok_blk),
        pl.BlockSpec((1, N_EXPERTS, V7X_LANES), lambda b, s: (b * n_s + s, 0, 0)),
    )
    return pl.pallas_call(
        _mixer_kernel,
        grid=(B, n_s),
        in_specs=in_specs,
        out_specs=out_specs,
        out_shape=out_shape,
        scratch_shapes=[pltpu.VMEM((HALO + ts, BRANCH_WIDTH), F32),
                        pltpu.VMEM((HALO + ts, BRANCH_WIDTH), F32),
                        pltpu.VMEM((V7X_SUBLANES - 1, HALO + ts - V7X_SUBLANES, BRANCH_WIDTH), F32)],
        compiler_params=pltpu.CompilerParams(
            dimension_semantics=("arbitrary", "arbitrary"),
            vmem_limit_bytes=V7X_VMEM_BYTES - 8 * 1024 * 1024),
        name="mixer",
    )(x, *consts)


def _row_span(first_row, n_rows):
    return pl.ds(pl.multiple_of(first_row * ROW_TILES, ROW_TILES),
                 pl.multiple_of(n_rows * ROW_TILES, ROW_TILES))


def _dispatch_kernel(run_src, run_n, run_dst, pad_dst, pad_n, n_used_ref, h2b_ref, pos_ref, xs_hbm,
                     stage, zero_rows, run_sem, fill_sem, *, n_blocks):
    i = pl.program_id(0)
    n_tiles = pl.num_programs(0)
    slot = i % 2
    block_rows = MOE_BLOCK * ROW_TILES

    def run_copy(tile, e, buf):
        k = tile * N_EXPERTS + e
        return pltpu.make_async_copy(stage.at[buf, _row_span(run_src[k], run_n[k])],
                                     xs_hbm.at[_row_span(run_dst[k], run_n[k])], run_sem.at[buf])

    def pad_copy(e):
        return pltpu.make_async_copy(zero_rows.at[_row_span(0, pad_n[e])],
                                     xs_hbm.at[_row_span(pad_dst[e], pad_n[e])], fill_sem)

    def idle_block_copy(blk):
        return pltpu.make_async_copy(
            zero_rows, xs_hbm.at[pl.ds(pl.multiple_of(blk * block_rows, block_rows), block_rows)], fill_sem)

    @pl.when(i == 0)
    def _():
        zero_rows[...] = jnp.zeros_like(zero_rows)

        def start_idle(blk, c):
            idle_block_copy(blk).start()
            return c

        def wait_idle(blk, c):
            idle_block_copy(blk).wait()
            return c

        for e in range(N_EXPERTS):
            pad_copy(e).start()
        lax.fori_loop(n_used_ref[0], n_blocks, start_idle, 0)
        for e in range(N_EXPERTS):
            pad_copy(e).wait()
        lax.fori_loop(n_used_ref[0], n_blocks, wait_idle, 0)

    @pl.when(i >= 2)
    def _():
        for e in range(N_EXPERTS):
            run_copy(i - 2, e, slot).wait()

    h2b = h2b_ref[...]
    for a0 in range(0, N_SORTED, MOE_BLOCK):
        a_iota = lax.broadcasted_iota(I32, (MOE_BLOCK, MIX_TS), 0) + a0
        onehot = jnp.zeros((MOE_BLOCK, MIX_TS), F32)
        for k in range(TOP_K):
            onehot = jnp.where(a_iota == pos_ref[k:k + 1, :], 1.0, onehot)
        rows = _dot(onehot.astype(BF16), h2b)
        for j in range(ROW_TILES):
            stage[slot, pl.ds(a0 * ROW_TILES + j, MOE_BLOCK, stride=ROW_TILES), :] = (
                rows[:, j * V7X_LANES:(j + 1) * V7X_LANES])

    for e in range(N_EXPERTS):
        run_copy(i, e, slot).start()

    @pl.when(i == n_tiles - 1)
    def _():
        @pl.when(i >= 1)
        def _():
            for e in range(N_EXPERTS):
                run_copy(i - 1, e, 1 - slot).wait()
        for e in range(N_EXPERTS):
            run_copy(i, e, slot).wait()


def _dispatch_call(h2b, pos8, tables, n_blocks):
    T, D = h2b.shape
    n_tiles = T // MIX_TS
    grid_spec = pltpu.PrefetchScalarGridSpec(
        num_scalar_prefetch=6,
        grid=(n_tiles,),
        in_specs=[pl.BlockSpec((MIX_TS, D), lambda i, *_: (i, 0)),
                  pl.BlockSpec((V7X_SUBLANES, MIX_TS), lambda i, *_: (0, i))],
        out_specs=pl.BlockSpec(memory_space=pl.ANY),
        scratch_shapes=[pltpu.VMEM((2, N_SORTED * ROW_TILES, V7X_LANES), F32),
                        pltpu.VMEM((MOE_BLOCK * ROW_TILES, V7X_LANES), F32),
                        pltpu.SemaphoreType.DMA((2,)),
                        pltpu.SemaphoreType.DMA(())],
    )
    return pl.pallas_call(
        functools.partial(_dispatch_kernel, n_blocks=n_blocks),
        grid_spec=grid_spec,
        out_shape=jax.ShapeDtypeStruct((n_blocks * MOE_BLOCK * ROW_TILES, V7X_LANES), F32),
        compiler_params=pltpu.CompilerParams(
            dimension_semantics=("arbitrary",),
            vmem_limit_bytes=V7X_VMEM_BYTES - 16 * 1024 * 1024),
        name="dispatch",
    )(tables["run_src"], tables["run_n"], tables["run_dst"], tables["pad_dst"], tables["pad_n"],
      tables["n_used"], h2b, pos8)


def _moe_kernel(be_ref, next_e_ref, par_ref, n_used_ref, xs_ref, wup_hbm, bup_ref, wdn_hbm, bdn_ref,
                ys_ref, wup_f32, wdn_f32, wup_bf, wdn_bf, wsem, *, layer):
    blk = pl.program_id(0)
    active = blk < n_used_ref[0]
    prev = jnp.maximum(blk - 1, 0)
    expert = be_ref[blk]
    new_expert = (blk == 0) | (expert != be_ref[prev])
    par = par_ref[blk]

    def weight_copies(e, buf):
        return (pltpu.make_async_copy(wup_hbm.at[layer, e], wup_f32.at[buf], wsem.at[buf, 0]),
                pltpu.make_async_copy(wdn_hbm.at[layer, e], wdn_f32.at[buf], wsem.at[buf, 1]))

    @pl.when(blk == 0)
    def _():
        for c in weight_copies(expert, par):
            c.start()

    @pl.when(active & new_expert)
    def _():
        nxt = next_e_ref[blk]

        @pl.when(nxt != expert)
        def _():
            for c in weight_copies(nxt, 1 - par):
                c.start()

        for c in weight_copies(expert, par):
            c.wait()
        wup_bf[...] = wup_f32[par].astype(BF16)
        wdn_bf[...] = wdn_f32[par].astype(BF16)

    @pl.when(active)
    def _():
        xs = jnp.concatenate(
            [xs_ref[pl.ds(j, MOE_BLOCK, stride=ROW_TILES), :] for j in range(ROW_TILES)],
            axis=-1).astype(BF16)
        gu = _dot(xs, wup_bf[...]) + bup_ref[0, 0]
        x_glu = jnp.minimum(gu[:, :D_FF], SWIGLU_LIMIT)
        x_lin = jnp.clip(gu[:, D_FF:], -SWIGLU_LIMIT, SWIGLU_LIMIT)
        act = x_glu * _sigmoid(SWIGLU_ALPHA * x_glu) * (x_lin + 1.0)
        y = _dot(act.astype(BF16), wdn_bf[...]) + bdn_ref[0, 0]
        for j in range(ROW_TILES):
            ys_ref[pl.ds(j, MOE_BLOCK, stride=ROW_TILES), :] = y[:, j * V7X_LANES:(j + 1) * V7X_LANES]

    @pl.when(jnp.logical_not(active))
    def _():
        ys_ref[...] = jnp.zeros_like(ys_ref)


def _moe_call(layer, xs, tables, w_up, b_up, w_down, b_down):
    n_blocks = tables["block_e"].shape[0]
    block_rows = MOE_BLOCK * ROW_TILES
    b_idx = lambda b, be, *_: (layer, be[b], 0, 0)
    grid_spec = pltpu.PrefetchScalarGridSpec(
        num_scalar_prefetch=4,
        grid=(n_blocks,),
        in_specs=[
            pl.BlockSpec((block_rows, V7X_LANES), lambda b, *_: (b, 0)),
            pl.BlockSpec(memory_space=pl.ANY),
            pl.BlockSpec((1, 1, 1, 2 * D_FF), b_idx),
            pl.BlockSpec(memory_space=pl.ANY),
            pl.BlockSpec((1, 1, 1, D_MODEL), b_idx),
        ],
        out_specs=pl.BlockSpec((block_rows, V7X_LANES), lambda b, *_: (b, 0)),
        scratch_shapes=[pltpu.VMEM((2, D_MODEL, 2 * D_FF), F32), pltpu.VMEM((2, D_FF, D_MODEL), F32),
                        pltpu.VMEM((D_MODEL, 2 * D_FF), BF16), pltpu.VMEM((D_FF, D_MODEL), BF16),
                        pltpu.SemaphoreType.DMA((2, 2))],
    )
    return pl.pallas_call(
        functools.partial(_moe_kernel, layer=layer),
        grid_spec=grid_spec,
        out_shape=jax.ShapeDtypeStruct(xs.shape, F32),
        compiler_params=pltpu.CompilerParams(
            dimension_semantics=("arbitrary",),
            vmem_limit_bytes=V7X_VMEM_BYTES - 8 * 1024 * 1024),
        name="moe",
    )(tables["block_e"], tables["next_e"], tables["group_par"], tables["n_used"], xs, w_up, b_up,
      w_down, b_down)


def _combine_kernel(run_src, run_n, run_dst, x1_ref, pos_ref, gate_ref, p_ref, ple_g_ref, gate_w_ref,
                    proj_w_ref, fin_g_ref, ys_hbm, out_ref, stage, run_sem, *, last):
    i = pl.program_id(0)
    n_tiles = pl.num_programs(0)
    slot = i % 2

    def run_copy(tile, e, buf):
        k = tile * N_EXPERTS + e
        return pltpu.make_async_copy(ys_hbm.at[_row_span(run_dst[k], run_n[k])],
                                     stage.at[buf, _row_span(run_src[k], run_n[k])], run_sem.at[buf])

    @pl.when(i == 0)
    def _():
        for e in range(N_EXPERTS):
            run_copy(0, e, 0).start()

    @pl.when(i + 1 < n_tiles)
    def _():
        for e in range(N_EXPERTS):
            run_copy(i + 1, e, 1 - slot).start()

    for e in range(N_EXPERTS):
        run_copy(i, e, slot).wait()

    x2 = x1_ref[...]
    chunk = MIX_TS
    for a0 in range(0, N_SORTED, chunk):
        a_iota = lax.broadcasted_iota(I32, (MIX_TS, chunk), 1) + a0
        weights = jnp.zeros((MIX_TS, chunk), F32)
        for k in range(TOP_K):
            weights = jnp.where(a_iota == pos_ref[:, k:k + 1], gate_ref[:, k:k + 1], weights)
        y_sorted = jnp.concatenate(
            [stage[slot, pl.ds(a0 * ROW_TILES + j, chunk, stride=ROW_TILES), :] for j in range(ROW_TILES)],
            axis=-1).astype(BF16)
        x2 = x2 + _dot(weights.astype(BF16), y_sorted)
    h3 = _rms_norm(x2, ple_g_ref[...]).astype(BF16)
    g = _sigmoid(_dot(h3, gate_w_ref[...]))
    pp = _dot(p_ref[...].astype(BF16), proj_w_ref[...])
    x3 = x2 + g * pp
    if last:
        x3 = _rms_norm(x3, fin_g_ref[...])
    out_ref[...] = x3


def _combine_call(layer, x1, ys, pos_tm, gate_tm, p2, tables, lw, fin_g, last):
    T, D = x1.shape
    n_tiles = T // MIX_TS
    consts = [lw["ple_norm"], lw["ple_gate_w"], lw["ple_proj_w"], fin_g]
    grid_spec = pltpu.PrefetchScalarGridSpec(
        num_scalar_prefetch=3,
        grid=(n_tiles,),
        in_specs=[pl.BlockSpec((MIX_TS, D), lambda i, *_: (i, 0)),
                  pl.BlockSpec((MIX_TS, V7X_SUBLANES), lambda i, *_: (i, 0)),
                  pl.BlockSpec((MIX_TS, V7X_SUBLANES), lambda i, *_: (i, 0)),
                  pl.BlockSpec((MIX_TS, PLE_DIM), lambda i, *_: (layer * n_tiles + i, 0))]
        + [_const_spec(c.shape) for c in consts] + [pl.BlockSpec(memory_space=pl.ANY)],
        out_specs=pl.BlockSpec((MIX_TS, D), lambda i, *_: (i, 0)),
        scratch_shapes=[pltpu.VMEM((2, N_SORTED * ROW_TILES, V7X_LANES), F32),
                        pltpu.SemaphoreType.DMA((2,))],
    )
    return pl.pallas_call(
        functools.partial(_combine_kernel, last=last),
        grid_spec=grid_spec,
        out_shape=jax.ShapeDtypeStruct((T, D), F32),
        compiler_params=pltpu.CompilerParams(
            dimension_semantics=("arbitrary",),
            vmem_limit_bytes=V7X_VMEM_BYTES - 16 * 1024 * 1024),
        name="combine",
    )(tables["run_src"], tables["run_n"], tables["run_dst"], x1, pos_tm, gate_tm, p2, *consts, ys)


def _routing_tables(tile_counts, n_blocks):
    c = tile_counts[:, :, 0].astype(I32)
    counts = jnp.sum(c, axis=0)
    padded = (counts + MOE_BLOCK - 1) // MOE_BLOCK * MOE_BLOCK
    pad_end = jnp.cumsum(padded)
    pad_start = pad_end - padded
    run_dst = pad_start[None, :] + jnp.cumsum(c, axis=0) - c
    run_src = jnp.cumsum(c, axis=1) - c
    n_used = pad_end[-1] // MOE_BLOCK
    blk0 = jnp.arange(n_blocks, dtype=I32) * MOE_BLOCK
    be = jnp.minimum(jnp.sum(blk0[:, None] >= pad_end[None, :], axis=1), N_EXPERTS - 1)
    be = be[jnp.minimum(jnp.arange(n_blocks), n_used - 1)]
    experts = jnp.arange(N_EXPERTS, dtype=I32)
    has_rows = counts > 0
    later = (experts[None, :] > experts[:, None]) & has_rows[None, :]
    next_e = jnp.min(jnp.where(later, experts[None, :], N_EXPERTS), axis=1)
    next_e = jnp.where(next_e == N_EXPERTS, experts, next_e)
    group_par = (jnp.cumsum(has_rows.astype(I32)) - 1) % 2
    return {
        "next_e": next_e[be].astype(I32), "group_par": group_par[be].astype(I32),
        "run_src": run_src.reshape(-1).astype(I32), "run_n": c.reshape(-1),
        "run_dst": run_dst.reshape(-1).astype(I32),
        "pad_dst": (pad_start + counts).astype(I32), "pad_n": (padded - counts).astype(I32),
        "n_used": n_used.reshape(1).astype(I32), "block_e": be.astype(I32),
    }


def kernel(x, p, mix_norm, w_in, b_in, pool_w, pool_scale, conv_w, conv_b, conv_norm_g, conv_norm_b,
           sgu_norm_g, sgu_norm_b, sgu_w, sgu_b, branch_w, branch_b, w_out, moe_norm, router_w,
           router_b, expert_w_up, expert_b_up, expert_w_down, expert_b_down, ple_norm, ple_gate_w,
           ple_proj_w, final_norm):
    B, S, D = x.shape
    T = B * S
    depth = w_in.shape[0]
    assert D == D_MODEL and S % MIX_TS == 0 and w_in.shape[2] == IN_WIDTH
    n_blocks = -(-(T * TOP_K) // MOE_BLOCK) + N_EXPERTS
    row = lambda a: a.reshape(1, -1)
    p2 = p.reshape(depth * T, PLE_DIM)
    b_up4 = expert_b_up[:, :, None, :]
    b_down4 = expert_b_down[:, :, None, :]
    for i in range(depth):
        lw = {
            "mix_norm": row(mix_norm[i]), "w_in": w_in[i].astype(BF16), "b_in": row(b_in[i]),
            "pool_w": pool_w[i].astype(BF16), "pool_scale": row(pool_scale[i]),
            "conv_w": conv_w[i], "conv_b": row(conv_b[i]),
            "conv_norm_g": row(conv_norm_g[i]), "conv_norm_b": row(conv_norm_b[i]),
            "sgu_norm_g": row(sgu_norm_g[i]), "sgu_norm_b": row(sgu_norm_b[i]),
            "sgu_w": sgu_w[i], "sgu_bt": sgu_b[i].T,
            "branch_w": branch_w[i].astype(BF16), "branch_b": branch_b[i],
            "w_out": w_out[i].astype(BF16), "moe_norm": row(moe_norm[i]),
            "router_wt": router_w[i].T, "router_b": router_b[i].reshape(-1, 1),
            "ple_norm": row(ple_norm[i]), "ple_gate_w": ple_gate_w[i].astype(BF16),
            "ple_proj_w": ple_proj_w[i].astype(BF16),
        }
        x1, h2b, pos8, gate8, tile_counts = _mixer_call(x, lw)
        tables = _routing_tables(tile_counts, n_blocks)
        xs = _dispatch_call(h2b, pos8, tables, n_blocks)
        ys = _moe_call(i, xs, tables, expert_w_up, b_up4, expert_w_down, b_down4)
        x = _combine_call(i, x1.reshape(T, D), ys, pos8.T, gate8.T, p2, tables, lw,
                          row(final_norm), last=(i == depth - 1)).reshape(B, S, D)
    return x
```

```python
import functools

import jax
import jax.numpy as jnp
from jax import lax
from jax.experimental import pallas as pl
from jax.experimental.pallas import tpu as pltpu

F32 = jnp.float32
BF16 = jnp.bfloat16
I32 = jnp.int32

D_MODEL = 1024
POOL_WINDOWS = (2, 4, 8, 16)
POOL_CH = 128
BRANCH_WIDTH = 512
CONV_K = 31
SGU_CHUNK = 128
SGU_HEADS = 4
N_BRANCH = 3
N_EXPERTS = 32
TOP_K = 4
D_FF = 1024
SWIGLU_LIMIT = 7.0
SWIGLU_ALPHA = 1.702
MOE_BLOCK = 256
PLE_DIM = 256
EPS = 1e-6

V7X_SUBLANES = 8
V7X_LANES = 128
ROW_TILES = D_MODEL // V7X_LANES
V7X_VMEM_BYTES = 64 * 1024 * 1024

MIX_TS = 512
HALO = 32
CONV_ROWS = 32
SIDE_CHUNK = 256
N_SORTED = TOP_K * MIX_TS

C_POOL = 0
C_CONV = C_POOL + BRANCH_WIDTH
C_SGU_U = C_CONV + 2 * BRANCH_WIDTH
C_SGU_V = C_SGU_U + BRANCH_WIDTH
C_GATE = C_SGU_V + BRANCH_WIDTH
IN_WIDTH = C_GATE + N_BRANCH * D_MODEL
S_POOL = 0
S_SGU_U = S_POOL + BRANCH_WIDTH
S_SGU_V = S_SGU_U + BRANCH_WIDTH
S_GATE = S_SGU_V + BRANCH_WIDTH
SIDE_WIDTH = S_GATE + N_BRANCH * D_MODEL
assert S_SGU_U % SIDE_CHUNK == 0 and SIDE_WIDTH % SIDE_CHUNK == 0


def _rms_norm(x, g):
    return x * lax.rsqrt(jnp.mean(x * x, axis=-1, keepdims=True) + EPS) * g


def _layer_norm(x, g, b):
    mu = jnp.mean(x, axis=-1, keepdims=True)
    xc = x - mu
    var = jnp.mean(xc * xc, axis=-1, keepdims=True)
    return xc * lax.rsqrt(var + EPS) * g + b


def _sigmoid(x):
    return 0.5 * jnp.tanh(0.5 * x) + 0.5


def _dot(a, b):
    return jnp.dot(a, b, preferred_element_type=F32)


def _mixer_kernel(x_ref, mix_g_ref, w_in_ref, b_in_ref, pool_w_ref, pool_scale_ref, conv_w_ref,
                  conv_b_ref, cn_g_ref, cn_b_ref, sn_g_ref, sn_b_ref, sgu_w_ref, sgu_bt_ref,
                  branch_w_ref, branch_b_ref, w_out_ref, moe_g_ref, rw_t_ref, rb_ref,
                  x1_ref, h2b_ref, pos_ref, gate_ref, counts_ref,
                  pool_hist, conv_hist, conv_shift, side_buf):
    ts = MIX_TS
    s = pl.program_id(1)

    @pl.when(s == 0)
    def _():
        pool_hist[0:HALO, :] = jnp.zeros((HALO, BRANCH_WIDTH), F32)
        conv_hist[0:HALO, :] = jnp.zeros((HALO, BRANCH_WIDTH), F32)

    x = x_ref[0]
    h = _rms_norm(x, mix_g_ref[...]).astype(BF16)

    def in_proj(c0, width):
        return _dot(h, w_in_ref[:, c0:c0 + width]) + b_in_ref[:, c0:c0 + width]

    zb = in_proj(C_CONV, 2 * BRANCH_WIDTH)
    conv_hist[HALO:HALO + ts, :] = zb[:, :BRANCH_WIDTH] * _sigmoid(zb[:, BRANCH_WIDTH:])
    n_shift_rows = HALO + ts - V7X_SUBLANES
    for sft in range(1, V7X_SUBLANES):
        conv_shift[sft - 1] = conv_hist[sft:sft + n_shift_rows, :]
    yb_parts = []
    n_conv_blocks = ts // CONV_ROWS
    n_side_chunks = SIDE_WIDTH // SIDE_CHUNK
    for bi in range(n_conv_blocks):
        r0 = bi * CONV_ROWS
        for ci in range(bi * n_side_chunks // n_conv_blocks, (bi + 1) * n_side_chunks // n_conv_blocks):
            d0 = ci * SIDE_CHUNK
            c0 = d0 if d0 < S_SGU_U else d0 + (C_SGU_U - S_SGU_U)
            side_buf[:, d0:d0 + SIDE_CHUNK] = in_proj(c0, SIDE_CHUNK)
        acc = jnp.zeros((CONV_ROWS, BRANCH_WIDTH), F32) + conv_b_ref[...]
        for k in range(CONV_K):
            off = HALO - (CONV_K - 1) + k + r0
            base, sft = off - off % V7X_SUBLANES, off % V7X_SUBLANES
            if sft == 0:
                window = conv_hist[base:base + CONV_ROWS, :]
            else:
                window = conv_shift[sft - 1, base:base + CONV_ROWS, :]
            acc = acc + conv_w_ref[k:k + 1, :] * window
        yb_rows = _layer_norm(acc, cn_g_ref[...], cn_b_ref[...])
        yb_parts.append((yb_rows * _sigmoid(yb_rows)).astype(BF16))
    yb = jnp.concatenate(yb_parts, axis=0)
    conv_hist[0:HALO, :] = conv_hist[ts:ts + HALO, :]

    za = side_buf[:, S_POOL:S_POOL + BRANCH_WIDTH]
    pool_hist[HALO:HALO + ts, :] = za
    row = lax.broadcasted_iota(I32, (ts, 1), 0) + s * ts
    mixed = []
    for g, w in enumerate(POOL_WINDOWS):
        c0 = g * POOL_CH
        cur = za[:, c0:c0 + POOL_CH]
        acc = cur
        for j in range(1, w):
            acc = acc + pool_hist[HALO - j:HALO - j + ts, c0:c0 + POOL_CH]
        count = jnp.minimum(row + 1, w).astype(F32)
        pooled = (acc / count - cur).astype(BF16)
        mixed.append(_dot(pooled, pool_w_ref[g]))
    ya = (jnp.concatenate(mixed, axis=-1) * pool_scale_ref[...]).astype(BF16)
    pool_hist[0:HALO, :] = pool_hist[ts:ts + HALO, :]

    zu = side_buf[:, S_SGU_U:S_SGU_U + BRANCH_WIDTH]
    zv = side_buf[:, S_SGU_V:S_SGU_V + BRANCH_WIDTH]
    v = _layer_norm(zv, sn_g_ref[...], sn_b_ref[...]).astype(BF16)
    tri = (lax.broadcasted_iota(I32, (SGU_CHUNK, SGU_CHUNK), 0)
           >= lax.broadcasted_iota(I32, (SGU_CHUNK, SGU_CHUNK), 1))
    chunks = []
    for c in range(ts // SGU_CHUNK):
        heads = []
        for hd in range(SGU_HEADS):
            w_tri = jnp.where(tri, sgu_w_ref[hd], 0.0).astype(BF16)
            vv = v[c * SGU_CHUNK:(c + 1) * SGU_CHUNK, hd * 128:(hd + 1) * 128]
            heads.append(_dot(w_tri, vv) + sgu_bt_ref[:, hd:hd + 1])
        chunks.append(jnp.concatenate(heads, axis=-1))
    yc = (zu * jnp.concatenate(chunks, axis=0)).astype(BF16)

    merged = jnp.zeros((ts, D_MODEL), F32)
    for k, yk in enumerate((ya, yb, yc)):
        proj = _dot(yk, branch_w_ref[k]) + branch_b_ref[k:k + 1, :]
        g0 = S_GATE + k * D_MODEL
        merged = merged + _sigmoid(side_buf[:, g0:g0 + D_MODEL]) * proj
    x1 = x + _dot(merged.astype(BF16), w_out_ref[...])
    x1_ref[0] = x1

    h2 = _rms_norm(x1, moe_g_ref[...])
    h2_hi = h2.astype(BF16)
    h2b_ref[...] = h2_hi
    h2_lo = (h2 - h2_hi.astype(F32)).astype(BF16)
    rw = rw_t_ref[...]
    rw_hi = rw.astype(BF16)
    rw_lo = (rw - rw_hi.astype(F32)).astype(BF16)
    nt = (((1,), (1,)), ((), ()))
    logits = (lax.dot_general(rw_hi, h2_hi, nt, preferred_element_type=F32)
              + lax.dot_general(rw_hi, h2_lo, nt, preferred_element_type=F32)
              + lax.dot_general(rw_lo, h2_hi, nt, preferred_element_type=F32)) + rb_ref[...]
    e_iota = lax.broadcasted_iota(I32, (N_EXPERTS, ts), 0).astype(F32)
    vals = logits
    top_v, sels = [], []
    for _k in range(TOP_K):
        m = jnp.max(vals, axis=0, keepdims=True)
        idx = jnp.min(jnp.where(vals == m, e_iota, float(N_EXPERTS)), axis=0, keepdims=True)
        sel = e_iota == idx
        vals = jnp.where(sel, -jnp.inf, vals)
        top_v.append(m)
        sels.append(sel)
    exps = [jnp.exp(tv - top_v[0]) for tv in top_v]
    denom = exps[0] + exps[1] + exps[2] + exps[3]
    chosen = jnp.zeros((N_EXPERTS, ts), F32)
    for sel in sels:
        chosen = chosen + jnp.where(sel, 1.0, 0.0)
    chosen_b = chosen.astype(BF16)
    before = (lax.broadcasted_iota(I32, (ts, ts), 0) < lax.broadcasted_iota(I32, (ts, ts), 1))
    prefix = _dot(chosen_b, jnp.where(before, 1.0, 0.0).astype(BF16))
    lower = (lax.broadcasted_iota(I32, (N_EXPERTS, N_EXPERTS), 0)
             > lax.broadcasted_iota(I32, (N_EXPERTS, N_EXPERTS), 1))
    tile_start = jnp.sum(_dot(jnp.where(lower, 1.0, 0.0).astype(BF16), chosen_b), axis=1, keepdims=True)
    base = prefix + tile_start
    zeros4 = jnp.zeros((V7X_SUBLANES - TOP_K, ts), F32)
    pos = [jnp.sum(jnp.where(sel, base, 0.0), axis=0, keepdims=True) for sel in sels]
    pos_ref[...] = jnp.concatenate(pos + [zeros4], axis=0).astype(I32)
    gate_ref[...] = jnp.concatenate([e / denom for e in exps] + [zeros4], axis=0)
    counts_ref[0] = jnp.broadcast_to(jnp.sum(chosen, axis=1, keepdims=True), (N_EXPERTS, V7X_LANES))


def _const_spec(shape):
    nd = len(shape)
    return pl.BlockSpec(shape, lambda *_: (0,) * nd, pipeline_mode=pl.Buffered(1))


def _mixer_call(x, lw):
    B, S, D = x.shape
    ts = MIX_TS
    n_s = S // ts
    T = B * S
    tok_blk = lambda b, s: (0, b * n_s + s)
    consts = [lw["mix_norm"], lw["w_in"], lw["b_in"], lw["pool_w"], lw["pool_scale"], lw["conv_w"],
              lw["conv_b"], lw["conv_norm_g"], lw["conv_norm_b"], lw["sgu_norm_g"], lw["sgu_norm_b"],
              lw["sgu_w"], lw["sgu_bt"], lw["branch_w"], lw["branch_b"], lw["w_out"], lw["moe_norm"],
              lw["router_wt"], lw["router_b"]]
    in_specs = [pl.BlockSpec((1, ts, D), lambda b, s: (b, s, 0))] + [_const_spec(c.shape) for c in consts]
    out_shape = (
        jax.ShapeDtypeStruct((B, S, D), F32),
        jax.ShapeDtypeStruct((T, D), BF16),
        jax.ShapeDtypeStruct((V7X_SUBLANES, T), I32),
        jax.ShapeDtypeStruct((V7X_SUBLANES, T), F32),
        jax.ShapeDtypeStruct((T // ts, N_EXPERTS, V7X_LANES), F32),
    )
    out_specs = (
        pl.BlockSpec((1, ts, D), lambda b, s: (b, s, 0)),
        pl.BlockSpec((ts, D), lambda b, s: (b * n_s + s, 0)),
        pl.BlockSpec((V7X_SUBLANES, ts), tok_blk),
        pl.BlockSpec((V7X_SUBLANES, ts), tok_blk),
        pl.BlockSpec((1, N_EXPERTS, V7X_LANES), lambda b, s: (b * n_s + s, 0, 0)),
    )
    return pl.pallas_call(
        _mixer_kernel,
        grid=(B, n_s),
        in_specs=in_specs,
        out_specs=out_specs,
        out_shape=out_shape,
        scratch_shapes=[pltpu.VMEM((HALO + ts, BRANCH_WIDTH), F32),
                        pltpu.VMEM((HALO + ts, BRANCH_WIDTH), F32),
                        pltpu.VMEM((V7X_SUBLANES - 1, HALO + ts - V7X_SUBLANES, BRANCH_WIDTH), F32),
                        pltpu.VMEM((ts, SIDE_WIDTH), F32)],
        compiler_params=pltpu.CompilerParams(
            dimension_semantics=("arbitrary", "arbitrary"),
            vmem_limit_bytes=V7X_VMEM_BYTES - 8 * 1024 * 1024),
        name="mixer",
    )(x, *consts)


def _row_span(first_row, n_rows):
    return pl.ds(pl.multiple_of(first_row * ROW_TILES, ROW_TILES),
                 pl.multiple_of(n_rows * ROW_TILES, ROW_TILES))


def _dispatch_kernel(run_src, run_n, run_dst, pad_dst, pad_n, n_used_ref, h2b_ref, pos_ref, xs_hbm,
                     stage, zero_rows, run_sem, fill_sem, *, n_blocks):
    i = pl.program_id(0)
    n_tiles = pl.num_programs(0)
    slot = i % 2
    block_rows = MOE_BLOCK * ROW_TILES

    def run_copy(tile, e, buf):
        k = tile * N_EXPERTS + e
        return pltpu.make_async_copy(stage.at[buf, _row_span(run_src[k], run_n[k])],
                                     xs_hbm.at[_row_span(run_dst[k], run_n[k])], run_sem.at[buf])

    def pad_copy(e):
        return pltpu.make_async_copy(zero_rows.at[_row_span(0, pad_n[e])],
                                     xs_hbm.at[_row_span(pad_dst[e], pad_n[e])], fill_sem)

    def idle_block_copy(blk):
        return pltpu.make_async_copy(
            zero_rows, xs_hbm.at[pl.ds(pl.multiple_of(blk * block_rows, block_rows), block_rows)], fill_sem)

    @pl.when(i == 0)
    def _():
        zero_rows[...] = jnp.zeros_like(zero_rows)

        def start_idle(blk, c):
            idle_block_copy(blk).start()
            return c

        def wait_idle(blk, c):
            idle_block_copy(blk).wait()
            return c

        for e in range(N_EXPERTS):
            pad_copy(e).start()
        lax.fori_loop(n_used_ref[0], n_blocks, start_idle, 0)
        for e in range(N_EXPERTS):
            pad_copy(e).wait()
        lax.fori_loop(n_used_ref[0], n_blocks, wait_idle, 0)

    @pl.when(i >= 2)
    def _():
        for e in range(N_EXPERTS):
            run_copy(i - 2, e, slot).wait()

    h2b = h2b_ref[...]
    for a0 in range(0, N_SORTED, MOE_BLOCK):
        a_iota = lax.broadcasted_iota(I32, (MOE_BLOCK, MIX_TS), 0) + a0
        onehot = jnp.zeros((MOE_BLOCK, MIX_TS), F32)
        for k in range(TOP_K):
            onehot = jnp.where(a_iota == pos_ref[k:k + 1, :], 1.0, onehot)
        rows = _dot(onehot.astype(BF16), h2b)
        for j in range(ROW_TILES):
            stage[slot, pl.ds(a0 * ROW_TILES + j, MOE_BLOCK, stride=ROW_TILES), :] = (
                rows[:, j * V7X_LANES:(j + 1) * V7X_LANES])

    for e in range(N_EXPERTS):
        run_copy(i, e, slot).start()

    @pl.when(i == n_tiles - 1)
    def _():
        @pl.when(i >= 1)
        def _():
            for e in range(N_EXPERTS):
                run_copy(i - 1, e, 1 - slot).wait()
        for e in range(N_EXPERTS):
            run_copy(i, e, slot).wait()


def _dispatch_call(h2b, pos8, tables, n_blocks):
    T, D = h2b.shape
    n_tiles = T // MIX_TS
    grid_spec = pltpu.PrefetchScalarGridSpec(
        num_scalar_prefetch=6,
        grid=(n_tiles,),
        in_specs=[pl.BlockSpec((MIX_TS, D), lambda i, *_: (i, 0)),
                  pl.BlockSpec((V7X_SUBLANES, MIX_TS), lambda i, *_: (0, i))],
        out_specs=pl.BlockSpec(memory_space=pl.ANY),
        scratch_shapes=[pltpu.VMEM((2, N_SORTED * ROW_TILES, V7X_LANES), F32),
                        pltpu.VMEM((MOE_BLOCK * ROW_TILES, V7X_LANES), F32),
                        pltpu.SemaphoreType.DMA((2,)),
                        pltpu.SemaphoreType.DMA(())],
    )
    return pl.pallas_call(
        functools.partial(_dispatch_kernel, n_blocks=n_blocks),
        grid_spec=grid_spec,
        out_shape=jax.ShapeDtypeStruct((n_blocks * MOE_BLOCK * ROW_TILES, V7X_LANES), F32),
        compiler_params=pltpu.CompilerParams(
            dimension_semantics=("arbitrary",),
            vmem_limit_bytes=V7X_VMEM_BYTES - 16 * 1024 * 1024),
        name="dispatch",
    )(tables["run_src"], tables["run_n"], tables["run_dst"], tables["pad_dst"], tables["pad_n"],
      tables["n_used"], h2b, pos8)


def _moe_kernel(be_ref, group_end_ref, n_used_ref, xs_ref, wup_hbm, bup_ref, wdn_hbm, bdn_ref,
                ys_ref, wup_f32, wdn_f32, wup_bf, wdn_bf, group_count, wsem, *, layer):
    blk = pl.program_id(0)
    n_used = n_used_ref[0]
    active = blk < n_used
    prev = jnp.maximum(blk - 1, 0)
    expert = be_ref[blk]
    new_expert = (blk == 0) | (expert != be_ref[prev])

    def weight_copies(e, buf):
        return (pltpu.make_async_copy(wup_hbm.at[layer, e], wup_f32.at[buf], wsem.at[buf, 0]),
                pltpu.make_async_copy(wdn_hbm.at[layer, e], wdn_f32.at[buf], wsem.at[buf, 1]))

    @pl.when(blk == 0)
    def _():
        group_count[0] = 0
        for c in weight_copies(expert, 0):
            c.start()

    @pl.when(active & new_expert)
    def _():
        par = group_count[0] % 2
        group_count[0] = group_count[0] + 1
        next_blk = group_end_ref[expert]

        @pl.when(next_blk < n_used)
        def _():
            for c in weight_copies(be_ref[next_blk], 1 - par):
                c.start()

        for c in weight_copies(expert, par):
            c.wait()
        wup_bf[...] = wup_f32[par].astype(BF16)
        wdn_bf[...] = wdn_f32[par].astype(BF16)

    @pl.when(active)
    def _():
        xs = jnp.concatenate(
            [xs_ref[pl.ds(j, MOE_BLOCK, stride=ROW_TILES), :] for j in range(ROW_TILES)],
            axis=-1).astype(BF16)
        gu = _dot(xs, wup_bf[...]) + bup_ref[0, 0]
        x_glu = jnp.minimum(gu[:, :D_FF], SWIGLU_LIMIT)
        x_lin = jnp.clip(gu[:, D_FF:], -SWIGLU_LIMIT, SWIGLU_LIMIT)
        act = x_glu * _sigmoid(SWIGLU_ALPHA * x_glu) * (x_lin + 1.0)
        y = _dot(act.astype(BF16), wdn_bf[...]) + bdn_ref[0, 0]
        for j in range(ROW_TILES):
            ys_ref[pl.ds(j, MOE_BLOCK, stride=ROW_TILES), :] = y[:, j * V7X_LANES:(j + 1) * V7X_LANES]

    @pl.when(jnp.logical_not(active))
    def _():
        ys_ref[...] = jnp.zeros_like(ys_ref)


def _moe_call(layer, xs, tables, w_up, b_up, w_down, b_down):
    n_blocks = tables["block_e"].shape[0]
    block_rows = MOE_BLOCK * ROW_TILES
    b_idx = lambda b, be, *_: (layer, be[b], 0, 0)
    grid_spec = pltpu.PrefetchScalarGridSpec(
        num_scalar_prefetch=3,
        grid=(n_blocks,),
        in_specs=[
            pl.BlockSpec((block_rows, V7X_LANES), lambda b, *_: (b, 0)),
            pl.BlockSpec(memory_space=pl.ANY),
            pl.BlockSpec((1, 1, 1, 2 * D_FF), b_idx),
            pl.BlockSpec(memory_space=pl.ANY),
            pl.BlockSpec((1, 1, 1, D_MODEL), b_idx),
        ],
        out_specs=pl.BlockSpec((block_rows, V7X_LANES), lambda b, *_: (b, 0)),
        scratch_shapes=[pltpu.VMEM((2, D_MODEL, 2 * D_FF), F32), pltpu.VMEM((2, D_FF, D_MODEL), F32),
                        pltpu.VMEM((D_MODEL, 2 * D_FF), BF16), pltpu.VMEM((D_FF, D_MODEL), BF16),
                        pltpu.SMEM((1,), I32), pltpu.SemaphoreType.DMA((2, 2))],
    )
    return pl.pallas_call(
        functools.partial(_moe_kernel, layer=layer),
        grid_spec=grid_spec,
        out_shape=jax.ShapeDtypeStruct(xs.shape, F32),
        compiler_params=pltpu.CompilerParams(
            dimension_semantics=("arbitrary",),
            vmem_limit_bytes=V7X_VMEM_BYTES - 8 * 1024 * 1024),
        name="moe",
    )(tables["block_e"], tables["group_end"], tables["n_used"], xs, w_up, b_up, w_down, b_down)


def _combine_kernel(run_src, run_n, run_dst, x1_ref, pos_ref, gate_ref, p_ref, ple_g_ref, gate_w_ref,
                    proj_w_ref, fin_g_ref, ys_hbm, out_ref, stage, run_sem, *, last):
    i = pl.program_id(0)
    n_tiles = pl.num_programs(0)
    slot = i % 2

    def run_copy(tile, e, buf):
        k = tile * N_EXPERTS + e
        return pltpu.make_async_copy(ys_hbm.at[_row_span(run_dst[k], run_n[k])],
                                     stage.at[buf, _row_span(run_src[k], run_n[k])], run_sem.at[buf])

    @pl.when(i == 0)
    def _():
        for e in range(N_EXPERTS):
            run_copy(0, e, 0).start()

    @pl.when(i + 1 < n_tiles)
    def _():
        for e in range(N_EXPERTS):
            run_copy(i + 1, e, 1 - slot).start()

    for e in range(N_EXPERTS):
        run_copy(i, e, slot).wait()

    x2 = x1_ref[...]
    chunk = MIX_TS
    for a0 in range(0, N_SORTED, chunk):
        a_iota = lax.broadcasted_iota(I32, (MIX_TS, chunk), 1) + a0
        weights = jnp.zeros((MIX_TS, chunk), F32)
        for k in range(TOP_K):
            weights = jnp.where(a_iota == pos_ref[:, k:k + 1], gate_ref[:, k:k + 1], weights)
        y_sorted = jnp.concatenate(
            [stage[slot, pl.ds(a0 * ROW_TILES + j, chunk, stride=ROW_TILES), :] for j in range(ROW_TILES)],
            axis=-1).astype(BF16)
        x2 = x2 + _dot(weights.astype(BF16), y_sorted)
    h3 = _rms_norm(x2, ple_g_ref[...]).astype(BF16)
    g = _sigmoid(_dot(h3, gate_w_ref[...]))
    pp = _dot(p_ref[0, 0].astype(BF16), proj_w_ref[...])
    x3 = x2 + g * pp
    if last:
        x3 = _rms_norm(x3, fin_g_ref[...])
    out_ref[...] = x3


def _combine_call(layer, x1, ys, pos_tm, gate_tm, p, tables, lw, fin_g, last):
    T, D = x1.shape
    n_tiles = T // MIX_TS
    n_seq = p.shape[2] // MIX_TS
    consts = [lw["ple_norm"], lw["ple_gate_w"], lw["ple_proj_w"], fin_g]
    grid_spec = pltpu.PrefetchScalarGridSpec(
        num_scalar_prefetch=3,
        grid=(n_tiles,),
        in_specs=[pl.BlockSpec((MIX_TS, D), lambda i, *_: (i, 0)),
                  pl.BlockSpec((MIX_TS, V7X_SUBLANES), lambda i, *_: (i, 0)),
                  pl.BlockSpec((MIX_TS, V7X_SUBLANES), lambda i, *_: (i, 0)),
                  pl.BlockSpec((1, 1, MIX_TS, PLE_DIM), lambda i, *_: (layer, i // n_seq, i % n_seq, 0))]
        + [_const_spec(c.shape) for c in consts] + [pl.BlockSpec(memory_space=pl.ANY)],
        out_specs=pl.BlockSpec((MIX_TS, D), lambda i, *_: (i, 0)),
        scratch_shapes=[pltpu.VMEM((2, N_SORTED * ROW_TILES, V7X_LANES), F32),
                        pltpu.SemaphoreType.DMA((2,))],
    )
    return pl.pallas_call(
        functools.partial(_combine_kernel, last=last),
        grid_spec=grid_spec,
        out_shape=jax.ShapeDtypeStruct((T, D), F32),
        compiler_params=pltpu.CompilerParams(
            dimension_semantics=("arbitrary",),
            vmem_limit_bytes=V7X_VMEM_BYTES - 16 * 1024 * 1024),
        name="combine",
    )(tables["run_src"], tables["run_n"], tables["run_dst"], x1, pos_tm, gate_tm, p, *consts, ys)


def _routing_tables(tile_counts, n_blocks):
    c = tile_counts[:, :, 0].astype(I32)
    counts = jnp.sum(c, axis=0)
    padded = (counts + MOE_BLOCK - 1) // MOE_BLOCK * MOE_BLOCK
    pad_end = jnp.cumsum(padded)
    pad_start = pad_end - padded
    run_dst = pad_start[None, :] + jnp.cumsum(c, axis=0) - c
    run_src = jnp.cumsum(c, axis=1) - c
    n_used = pad_end[-1] // MOE_BLOCK
    blk0 = jnp.arange(n_blocks, dtype=I32) * MOE_BLOCK
    be = jnp.minimum(jnp.sum(blk0[:, None] >= pad_end[None, :], axis=1), N_EXPERTS - 1)
    be = be[jnp.minimum(jnp.arange(n_blocks), n_used - 1)]
    return {
        "group_end": (pad_end // MOE_BLOCK).astype(I32),
        "run_src": run_src.reshape(-1).astype(I32), "run_n": c.reshape(-1),
        "run_dst": run_dst.reshape(-1).astype(I32),
        "pad_dst": (pad_start + counts).astype(I32), "pad_n": (padded - counts).astype(I32),
        "n_used": n_used.reshape(1).astype(I32), "block_e": be.astype(I32),
    }


def kernel(x, p, mix_norm, w_in, b_in, pool_w, pool_scale, conv_w, conv_b, conv_norm_g, conv_norm_b,
           sgu_norm_g, sgu_norm_b, sgu_w, sgu_b, branch_w, branch_b, w_out, moe_norm, router_w,
           router_b, expert_w_up, expert_b_up, expert_w_down, expert_b_down, ple_norm, ple_gate_w,
           ple_proj_w, final_norm):
    B, S, D = x.shape
    T = B * S
    depth = w_in.shape[0]
    assert D == D_MODEL and S % MIX_TS == 0 and w_in.shape[2] == IN_WIDTH
    n_blocks = -(-(T * TOP_K) // MOE_BLOCK) + N_EXPERTS
    row = lambda a: a.reshape(1, -1)
    b_up4 = expert_b_up[:, :, None, :]
    b_down4 = expert_b_down[:, :, None, :]
    for i in range(depth):
        lw = {
            "mix_norm": row(mix_norm[i]), "w_in": w_in[i].astype(BF16), "b_in": row(b_in[i]),
            "pool_w": pool_w[i].astype(BF16), "pool_scale": row(pool_scale[i]),
            "conv_w": conv_w[i], "conv_b": row(conv_b[i]),
            "conv_norm_g": row(conv_norm_g[i]), "conv_norm_b": row(conv_norm_b[i]),
            "sgu_norm_g": row(sgu_norm_g[i]), "sgu_norm_b": row(sgu_norm_b[i]),
            "sgu_w": sgu_w[i], "sgu_bt": sgu_b[i].T,
            "branch_w": branch_w[i].astype(BF16), "branch_b": branch_b[i],
            "w_out": w_out[i].astype(BF16), "moe_norm": row(moe_norm[i]),
            "router_wt": router_w[i].T, "router_b": router_b[i].reshape(-1, 1),
            "ple_norm": row(ple_norm[i]), "ple_gate_w": ple_gate_w[i].astype(BF16),
            "ple_proj_w": ple_proj_w[i].astype(BF16),
        }
        x1, h2b, pos8, gate8, tile_counts = _mixer_call(x, lw)
        tables = _routing_tables(tile_counts, n_blocks)
        xs = _dispatch_call(h2b, pos8, tables, n_blocks)
        ys = _moe_call(i, xs, tables, expert_w_up, b_up4, expert_w_down, b_down4)
        x = _combine_call(i, x1.reshape(T, D), ys, pos8.T, gate8.T, p, tables, lw,
                          row(final_norm), last=(i == depth - 1)).reshape(B, S, D)
    return x
```

```python
import functools

import jax
import jax.numpy as jnp
from jax import lax
from jax.experimental import pallas as pl
from jax.experimental.pallas import tpu as pltpu

F32 = jnp.float32
BF16 = jnp.bfloat16
I32 = jnp.int32

D_MODEL = 1024
POOL_WINDOWS = (2, 4, 8, 16)
POOL_CH = 128
BRANCH_WIDTH = 512
CONV_K = 31
SGU_CHUNK = 128
SGU_HEADS = 4
N_BRANCH = 3
N_EXPERTS = 32
TOP_K = 4
D_FF = 1024
SWIGLU_LIMIT = 7.0
SWIGLU_ALPHA = 1.702
MOE_BLOCK = 256
PLE_DIM = 256
EPS = 1e-6

V7X_SUBLANES = 8
V7X_LANES = 128
ROW_TILES = D_MODEL // V7X_LANES
V7X_VMEM_BYTES = 64 * 1024 * 1024

MIX_TS = 512
HALO = 32
CONV_ROWS = 32
SIDE_CHUNK = 256
MOE_TS = 256
N_SORTED = TOP_K * MOE_TS
assert MIX_TS % MOE_TS == 0

C_POOL = 0
C_CONV = C_POOL + BRANCH_WIDTH
C_SGU_U = C_CONV + 2 * BRANCH_WIDTH
C_SGU_V = C_SGU_U + BRANCH_WIDTH
C_GATE = C_SGU_V + BRANCH_WIDTH
IN_WIDTH = C_GATE + N_BRANCH * D_MODEL
S_POOL = 0
S_SGU_U = S_POOL + BRANCH_WIDTH
S_SGU_V = S_SGU_U + BRANCH_WIDTH
S_GATE = S_SGU_V + BRANCH_WIDTH
SIDE_WIDTH = S_GATE + N_BRANCH * D_MODEL
assert S_SGU_U % SIDE_CHUNK == 0 and SIDE_WIDTH % SIDE_CHUNK == 0


def _rms_norm(x, g):
    return x * lax.rsqrt(jnp.mean(x * x, axis=-1, keepdims=True) + EPS) * g


def _layer_norm(x, g, b):
    mu = jnp.mean(x, axis=-1, keepdims=True)
    xc = x - mu
    var = jnp.mean(xc * xc, axis=-1, keepdims=True)
    return xc * lax.rsqrt(var + EPS) * g + b


def _sigmoid(x):
    return 0.5 * jnp.tanh(0.5 * x) + 0.5


def _dot(a, b):
    return jnp.dot(a, b, preferred_element_type=F32)


def _mixer_kernel(x_ref, mix_g_ref, w_in_ref, b_in_ref, pool_w_ref, pool_scale_ref, conv_w_ref,
                  conv_b_ref, cn_g_ref, cn_b_ref, sn_g_ref, sn_b_ref, sgu_w_ref, sgu_bt_ref,
                  branch_w_ref, branch_b_ref, w_out_ref, moe_g_ref, rw_t_ref, rb_ref,
                  x1_ref, h2b_ref, pos_ref, gate_ref, counts_ref,
                  pool_hist, conv_hist, conv_shift, side_buf):
    ts = MIX_TS
    s = pl.program_id(1)

    @pl.when(s == 0)
    def _():
        pool_hist[0:HALO, :] = jnp.zeros((HALO, BRANCH_WIDTH), F32)
        conv_hist[0:HALO, :] = jnp.zeros((HALO, BRANCH_WIDTH), F32)

    x = x_ref[0]
    h = _rms_norm(x, mix_g_ref[...]).astype(BF16)

    def in_proj(c0, width):
        return _dot(h, w_in_ref[:, c0:c0 + width]) + b_in_ref[:, c0:c0 + width]

    zb = in_proj(C_CONV, 2 * BRANCH_WIDTH)
    conv_hist[HALO:HALO + ts, :] = zb[:, :BRANCH_WIDTH] * _sigmoid(zb[:, BRANCH_WIDTH:])
    n_shift_rows = HALO + ts - V7X_SUBLANES
    for sft in range(1, V7X_SUBLANES):
        conv_shift[sft - 1] = conv_hist[sft:sft + n_shift_rows, :]
    yb_parts = []
    n_conv_blocks = ts // CONV_ROWS
    n_side_chunks = SIDE_WIDTH // SIDE_CHUNK
    for bi in range(n_conv_blocks):
        r0 = bi * CONV_ROWS
        for ci in range(bi * n_side_chunks // n_conv_blocks, (bi + 1) * n_side_chunks // n_conv_blocks):
            d0 = ci * SIDE_CHUNK
            c0 = d0 if d0 < S_SGU_U else d0 + (C_SGU_U - S_SGU_U)
            side_buf[:, d0:d0 + SIDE_CHUNK] = in_proj(c0, SIDE_CHUNK)
        acc = jnp.zeros((CONV_ROWS, BRANCH_WIDTH), F32) + conv_b_ref[...]
        for k in range(CONV_K):
            off = HALO - (CONV_K - 1) + k + r0
            base, sft = off - off % V7X_SUBLANES, off % V7X_SUBLANES
            if sft == 0:
                window = conv_hist[base:base + CONV_ROWS, :]
            else:
                window = conv_shift[sft - 1, base:base + CONV_ROWS, :]
            acc = acc + conv_w_ref[k:k + 1, :] * window
        yb_rows = _layer_norm(acc, cn_g_ref[...], cn_b_ref[...])
        yb_parts.append((yb_rows * _sigmoid(yb_rows)).astype(BF16))
    yb = jnp.concatenate(yb_parts, axis=0)
    conv_hist[0:HALO, :] = conv_hist[ts:ts + HALO, :]

    za = side_buf[:, S_POOL:S_POOL + BRANCH_WIDTH]
    pool_hist[HALO:HALO + ts, :] = za
    row = lax.broadcasted_iota(I32, (ts, 1), 0) + s * ts
    mixed = []
    for g, w in enumerate(POOL_WINDOWS):
        c0 = g * POOL_CH
        cur = za[:, c0:c0 + POOL_CH]
        acc = cur
        for j in range(1, w):
            acc = acc + pool_hist[HALO - j:HALO - j + ts, c0:c0 + POOL_CH]
        count = jnp.minimum(row + 1, w).astype(F32)
        pooled = (acc / count - cur).astype(BF16)
        mixed.append(_dot(pooled, pool_w_ref[g]))
    ya = (jnp.concatenate(mixed, axis=-1) * pool_scale_ref[...]).astype(BF16)
    pool_hist[0:HALO, :] = pool_hist[ts:ts + HALO, :]

    zu = side_buf[:, S_SGU_U:S_SGU_U + BRANCH_WIDTH]
    zv = side_buf[:, S_SGU_V:S_SGU_V + BRANCH_WIDTH]
    v = _layer_norm(zv, sn_g_ref[...], sn_b_ref[...]).astype(BF16)
    tri = (lax.broadcasted_iota(I32, (SGU_CHUNK, SGU_CHUNK), 0)
           >= lax.broadcasted_iota(I32, (SGU_CHUNK, SGU_CHUNK), 1))
    chunks = []
    for c in range(ts // SGU_CHUNK):
        heads = []
        for hd in range(SGU_HEADS):
            w_tri = jnp.where(tri, sgu_w_ref[hd], 0.0).astype(BF16)
            vv = v[c * SGU_CHUNK:(c + 1) * SGU_CHUNK, hd * 128:(hd + 1) * 128]
            heads.append(_dot(w_tri, vv) + sgu_bt_ref[:, hd:hd + 1])
        chunks.append(jnp.concatenate(heads, axis=-1))
    yc = (zu * jnp.concatenate(chunks, axis=0)).astype(BF16)

    merged = jnp.zeros((ts, D_MODEL), F32)
    for k, yk in enumerate((ya, yb, yc)):
        proj = _dot(yk, branch_w_ref[k]) + branch_b_ref[k:k + 1, :]
        g0 = S_GATE + k * D_MODEL
        merged = merged + _sigmoid(side_buf[:, g0:g0 + D_MODEL]) * proj
    x1 = x + _dot(merged.astype(BF16), w_out_ref[...])
    x1_ref[0] = x1

    h2 = _rms_norm(x1, moe_g_ref[...])
    h2_hi = h2.astype(BF16)
    h2b_ref[...] = h2_hi
    h2_lo = (h2 - h2_hi.astype(F32)).astype(BF16)
    rw = rw_t_ref[...]
    rw_hi = rw.astype(BF16)
    rw_lo = (rw - rw_hi.astype(F32)).astype(BF16)
    nt = (((1,), (1,)), ((), ()))
    logits = (lax.dot_general(rw_hi, h2_hi, nt, preferred_element_type=F32)
              + lax.dot_general(rw_hi, h2_lo, nt, preferred_element_type=F32)
              + lax.dot_general(rw_lo, h2_hi, nt, preferred_element_type=F32)) + rb_ref[...]
    e_iota = lax.broadcasted_iota(I32, (N_EXPERTS, ts), 0).astype(F32)
    vals = logits
    top_v, sels = [], []
    for _k in range(TOP_K):
        m = jnp.max(vals, axis=0, keepdims=True)
        idx = jnp.min(jnp.where(vals == m, e_iota, float(N_EXPERTS)), axis=0, keepdims=True)
        sel = e_iota == idx
        vals = jnp.where(sel, -jnp.inf, vals)
        top_v.append(m)
        sels.append(sel)
    exps = [jnp.exp(tv - top_v[0]) for tv in top_v]
    denom = exps[0] + exps[1] + exps[2] + exps[3]
    chosen = jnp.zeros((N_EXPERTS, ts), F32)
    for sel in sels:
        chosen = chosen + jnp.where(sel, 1.0, 0.0)
    chosen_b = chosen.astype(BF16)
    t_row = lax.broadcasted_iota(I32, (ts, ts), 0)
    t_col = lax.broadcasted_iota(I32, (ts, ts), 1)
    before = jnp.where(t_row // MOE_TS == t_col // MOE_TS, jnp.where(t_row < t_col, 1.0, 0.0), 0.0)
    prefix = _dot(chosen_b, before.astype(BF16))
    lower = (lax.broadcasted_iota(I32, (N_EXPERTS, N_EXPERTS), 0)
             > lax.broadcasted_iota(I32, (N_EXPERTS, N_EXPERTS), 1))
    below = _dot(jnp.where(lower, 1.0, 0.0).astype(BF16), chosen_b)
    base_parts = []
    for j in range(ts // MOE_TS):
        lanes = slice(j * MOE_TS, (j + 1) * MOE_TS)
        tile_start = jnp.sum(below[:, lanes], axis=1, keepdims=True)
        base_parts.append(prefix[:, lanes] + tile_start)
        counts_ref[j] = jnp.broadcast_to(jnp.sum(chosen[:, lanes], axis=1, keepdims=True),
                                         (N_EXPERTS, V7X_LANES))
    base = jnp.concatenate(base_parts, axis=1)
    zeros4 = jnp.zeros((V7X_SUBLANES - TOP_K, ts), F32)
    pos = [jnp.sum(jnp.where(sel, base, 0.0), axis=0, keepdims=True) for sel in sels]
    pos_ref[...] = jnp.concatenate(pos + [zeros4], axis=0).astype(I32)
    gate_ref[...] = jnp.concatenate([e / denom for e in exps] + [zeros4], axis=0)


def _const_spec(shape):
    nd = len(shape)
    return pl.BlockSpec(shape, lambda *_: (0,) * nd, pipeline_mode=pl.Buffered(1))


def _mixer_call(x, lw):
    B, S, D = x.shape
    ts = MIX_TS
    n_s = S // ts
    T = B * S
    tok_blk = lambda b, s: (0, b * n_s + s)
    consts = [lw["mix_norm"], lw["w_in"], lw["b_in"], lw["pool_w"], lw["pool_scale"], lw["conv_w"],
              lw["conv_b"], lw["conv_norm_g"], lw["conv_norm_b"], lw["sgu_norm_g"], lw["sgu_norm_b"],
              lw["sgu_w"], lw["sgu_bt"], lw["branch_w"], lw["branch_b"], lw["w_out"], lw["moe_norm"],
              lw["router_wt"], lw["router_b"]]
    in_specs = [pl.BlockSpec((1, ts, D), lambda b, s: (b, s, 0))] + [_const_spec(c.shape) for c in consts]
    out_shape = (
        jax.ShapeDtypeStruct((B, S, D), F32),
        jax.ShapeDtypeStruct((T, D), BF16),
        jax.ShapeDtypeStruct((V7X_SUBLANES, T), I32),
        jax.ShapeDtypeStruct((V7X_SUBLANES, T), F32),
        jax.ShapeDtypeStruct((T // MOE_TS, N_EXPERTS, V7X_LANES), F32),
    )
    out_specs = (
        pl.BlockSpec((1, ts, D), lambda b, s: (b, s, 0)),
        pl.BlockSpec((ts, D), lambda b, s: (b * n_s + s, 0)),
        pl.BlockSpec((V7X_SUBLANES, ts), tok_blk),
        pl.BlockSpec((V7X_SUBLANES, ts), tok_blk),
        pl.BlockSpec((ts // MOE_TS, N_EXPERTS, V7X_LANES), lambda b, s: (b * n_s + s, 0, 0)),
    )
    return pl.pallas_call(
        _mixer_kernel,
        grid=(B, n_s),
        in_specs=in_specs,
        out_specs=out_specs,
        out_shape=out_shape,
        scratch_shapes=[pltpu.VMEM((HALO + ts, BRANCH_WIDTH), F32),
                        pltpu.VMEM((HALO + ts, BRANCH_WIDTH), F32),
                        pltpu.VMEM((V7X_SUBLANES - 1, HALO + ts - V7X_SUBLANES, BRANCH_WIDTH), F32),
                        pltpu.VMEM((ts, SIDE_WIDTH), F32)],
        compiler_params=pltpu.CompilerParams(
            dimension_semantics=("arbitrary", "arbitrary"),
            vmem_limit_bytes=V7X_VMEM_BYTES - 8 * 1024 * 1024),
        name="mixer",
    )(x, *consts)


def _row_span(first_row, n_rows):
    return pl.ds(pl.multiple_of(first_row * ROW_TILES, ROW_TILES),
                 pl.multiple_of(n_rows * ROW_TILES, ROW_TILES))


def _dispatch_kernel(run_src, run_n, run_dst, pad_dst, pad_n, n_used_ref, h2b_ref, pos_ref, xs_hbm,
                     stage, zero_rows, run_sem, fill_sem, *, n_blocks):
    i = pl.program_id(0)
    n_tiles = pl.num_programs(0)
    slot = i % 2
    block_rows = MOE_BLOCK * ROW_TILES

    def run_copy(tile, e, buf):
        k = tile * N_EXPERTS + e
        return pltpu.make_async_copy(stage.at[buf, _row_span(run_src[k], run_n[k])],
                                     xs_hbm.at[_row_span(run_dst[k], run_n[k])], run_sem.at[buf])

    def pad_copy(e):
        return pltpu.make_async_copy(zero_rows.at[_row_span(0, pad_n[e])],
                                     xs_hbm.at[_row_span(pad_dst[e], pad_n[e])], fill_sem)

    def idle_block_copy(blk):
        return pltpu.make_async_copy(
            zero_rows, xs_hbm.at[pl.ds(pl.multiple_of(blk * block_rows, block_rows), block_rows)], fill_sem)

    @pl.when(i == 0)
    def _():
        zero_rows[...] = jnp.zeros_like(zero_rows)

        def start_idle(blk, c):
            idle_block_copy(blk).start()
            return c

        def wait_idle(blk, c):
            idle_block_copy(blk).wait()
            return c

        for e in range(N_EXPERTS):
            pad_copy(e).start()
        lax.fori_loop(n_used_ref[0], n_blocks, start_idle, 0)
        for e in range(N_EXPERTS):
            pad_copy(e).wait()
        lax.fori_loop(n_used_ref[0], n_blocks, wait_idle, 0)

    @pl.when(i >= 2)
    def _():
        for e in range(N_EXPERTS):
            run_copy(i - 2, e, slot).wait()

    h2b = h2b_ref[...]
    for a0 in range(0, N_SORTED, MOE_BLOCK):
        a_iota = lax.broadcasted_iota(I32, (MOE_BLOCK, MOE_TS), 0) + a0
        onehot = jnp.zeros((MOE_BLOCK, MOE_TS), F32)
        for k in range(TOP_K):
            onehot = jnp.where(a_iota == pos_ref[k:k + 1, :], 1.0, onehot)
        rows = _dot(onehot.astype(BF16), h2b)
        for j in range(ROW_TILES):
            stage[slot, pl.ds(a0 * ROW_TILES + j, MOE_BLOCK, stride=ROW_TILES), :] = (
                rows[:, j * V7X_LANES:(j + 1) * V7X_LANES])

    for e in range(N_EXPERTS):
        run_copy(i, e, slot).start()

    @pl.when(i == n_tiles - 1)
    def _():
        @pl.when(i >= 1)
        def _():
            for e in range(N_EXPERTS):
                run_copy(i - 1, e, 1 - slot).wait()
        for e in range(N_EXPERTS):
            run_copy(i, e, slot).wait()


def _dispatch_call(h2b, pos8, tables, n_blocks):
    T, D = h2b.shape
    n_tiles = T // MOE_TS
    grid_spec = pltpu.PrefetchScalarGridSpec(
        num_scalar_prefetch=6,
        grid=(n_tiles,),
        in_specs=[pl.BlockSpec((MOE_TS, D), lambda i, *_: (i, 0)),
                  pl.BlockSpec((V7X_SUBLANES, MOE_TS), lambda i, *_: (0, i))],
        out_specs=pl.BlockSpec(memory_space=pl.ANY),
        scratch_shapes=[pltpu.VMEM((2, N_SORTED * ROW_TILES, V7X_LANES), F32),
                        pltpu.VMEM((MOE_BLOCK * ROW_TILES, V7X_LANES), F32),
                        pltpu.SemaphoreType.DMA((2,)),
                        pltpu.SemaphoreType.DMA(())],
    )
    return pl.pallas_call(
        functools.partial(_dispatch_kernel, n_blocks=n_blocks),
        grid_spec=grid_spec,
        out_shape=jax.ShapeDtypeStruct((n_blocks * MOE_BLOCK * ROW_TILES, V7X_LANES), F32),
        compiler_params=pltpu.CompilerParams(
            dimension_semantics=("arbitrary",),
            vmem_limit_bytes=V7X_VMEM_BYTES - 16 * 1024 * 1024),
        name="dispatch",
    )(tables["run_src"], tables["run_n"], tables["run_dst"], tables["pad_dst"], tables["pad_n"],
      tables["n_used"], h2b, pos8)


def _moe_kernel(be_ref, group_end_ref, n_used_ref, xs_ref, wup_hbm, bup_ref, wdn_hbm, bdn_ref,
                ys_ref, wup_f32, wdn_f32, wup_bf, wdn_bf, group_count, wsem, *, layer):
    blk = pl.program_id(0)
    n_used = n_used_ref[0]
    active = blk < n_used
    prev = jnp.maximum(blk - 1, 0)
    expert = be_ref[blk]
    new_expert = (blk == 0) | (expert != be_ref[prev])

    def weight_copies(e, buf):
        return (pltpu.make_async_copy(wup_hbm.at[layer, e], wup_f32.at[buf], wsem.at[buf, 0]),
                pltpu.make_async_copy(wdn_hbm.at[layer, e], wdn_f32.at[buf], wsem.at[buf, 1]))

    @pl.when(blk == 0)
    def _():
        group_count[0] = 0
        for c in weight_copies(expert, 0):
            c.start()

    @pl.when(active & new_expert)
    def _():
        par = group_count[0] % 2
        group_count[0] = group_count[0] + 1
        next_blk = group_end_ref[expert]

        @pl.when(next_blk < n_used)
        def _():
            for c in weight_copies(be_ref[next_blk], 1 - par):
                c.start()

        for c in weight_copies(expert, par):
            c.wait()
        wup_bf[...] = wup_f32[par].astype(BF16)
        wdn_bf[...] = wdn_f32[par].astype(BF16)

    @pl.when(active)
    def _():
        xs = jnp.concatenate(
            [xs_ref[pl.ds(j, MOE_BLOCK, stride=ROW_TILES), :] for j in range(ROW_TILES)],
            axis=-1).astype(BF16)
        gu = _dot(xs, wup_bf[...]) + bup_ref[0, 0]
        x_glu = jnp.minimum(gu[:, :D_FF], SWIGLU_LIMIT)
        x_lin = jnp.clip(gu[:, D_FF:], -SWIGLU_LIMIT, SWIGLU_LIMIT)
        act = x_glu * _sigmoid(SWIGLU_ALPHA * x_glu) * (x_lin + 1.0)
        y = _dot(act.astype(BF16), wdn_bf[...]) + bdn_ref[0, 0]
        for j in range(ROW_TILES):
            ys_ref[pl.ds(j, MOE_BLOCK, stride=ROW_TILES), :] = y[:, j * V7X_LANES:(j + 1) * V7X_LANES]

    @pl.when(jnp.logical_not(active))
    def _():
        ys_ref[...] = jnp.zeros_like(ys_ref)


def _moe_call(layer, xs, tables, w_up, b_up, w_down, b_down):
    n_blocks = tables["block_e"].shape[0]
    block_rows = MOE_BLOCK * ROW_TILES
    b_idx = lambda b, be, *_: (layer, be[b], 0, 0)
    grid_spec = pltpu.PrefetchScalarGridSpec(
        num_scalar_prefetch=3,
        grid=(n_blocks,),
        in_specs=[
            pl.BlockSpec((block_rows, V7X_LANES), lambda b, *_: (b, 0)),
            pl.BlockSpec(memory_space=pl.ANY),
            pl.BlockSpec((1, 1, 1, 2 * D_FF), b_idx),
            pl.BlockSpec(memory_space=pl.ANY),
            pl.BlockSpec((1, 1, 1, D_MODEL), b_idx),
        ],
        out_specs=pl.BlockSpec((block_rows, V7X_LANES), lambda b, *_: (b, 0)),
        scratch_shapes=[pltpu.VMEM((2, D_MODEL, 2 * D_FF), F32), pltpu.VMEM((2, D_FF, D_MODEL), F32),
                        pltpu.VMEM((D_MODEL, 2 * D_FF), BF16), pltpu.VMEM((D_FF, D_MODEL), BF16),
                        pltpu.SMEM((1,), I32), pltpu.SemaphoreType.DMA((2, 2))],
    )
    return pl.pallas_call(
        functools.partial(_moe_kernel, layer=layer),
        grid_spec=grid_spec,
        out_shape=jax.ShapeDtypeStruct(xs.shape, F32),
        compiler_params=pltpu.CompilerParams(
            dimension_semantics=("arbitrary",),
            vmem_limit_bytes=V7X_VMEM_BYTES - 8 * 1024 * 1024),
        name="moe",
    )(tables["block_e"], tables["group_end"], tables["n_used"], xs, w_up, b_up, w_down, b_down)


def _combine_kernel(run_src, run_n, run_dst, x1_ref, pos_ref, gate_ref, p_ref, ple_g_ref, gate_w_ref,
                    proj_w_ref, fin_g_ref, ys_hbm, out_ref, stage, run_sem, *, last):
    i = pl.program_id(0)
    n_tiles = pl.num_programs(0)
    slot = i % 2

    def run_copy(tile, e, buf):
        k = tile * N_EXPERTS + e
        return pltpu.make_async_copy(ys_hbm.at[_row_span(run_dst[k], run_n[k])],
                                     stage.at[buf, _row_span(run_src[k], run_n[k])], run_sem.at[buf])

    @pl.when(i == 0)
    def _():
        for e in range(N_EXPERTS):
            run_copy(0, e, 0).start()

    @pl.when(i + 1 < n_tiles)
    def _():
        for e in range(N_EXPERTS):
            run_copy(i + 1, e, 1 - slot).start()

    for e in range(N_EXPERTS):
        run_copy(i, e, slot).wait()

    x2 = x1_ref[...]
    chunk = 2 * MOE_BLOCK
    for a0 in range(0, N_SORTED, chunk):
        a_iota = lax.broadcasted_iota(I32, (MOE_TS, chunk), 1) + a0
        weights = jnp.zeros((MOE_TS, chunk), F32)
        for k in range(TOP_K):
            weights = jnp.where(a_iota == pos_ref[:, k:k + 1], gate_ref[:, k:k + 1], weights)
        y_sorted = jnp.concatenate(
            [stage[slot, pl.ds(a0 * ROW_TILES + j, chunk, stride=ROW_TILES), :] for j in range(ROW_TILES)],
            axis=-1).astype(BF16)
        x2 = x2 + _dot(weights.astype(BF16), y_sorted)
    h3 = _rms_norm(x2, ple_g_ref[...]).astype(BF16)
    g = _sigmoid(_dot(h3, gate_w_ref[...]))
    pp = _dot(p_ref[0, 0].astype(BF16), proj_w_ref[...])
    x3 = x2 + g * pp
    if last:
        x3 = _rms_norm(x3, fin_g_ref[...])
    out_ref[...] = x3


def _combine_call(layer, x1, ys, pos_tm, gate_tm, p, tables, lw, fin_g, last):
    T, D = x1.shape
    n_tiles = T // MOE_TS
    n_seq = p.shape[2] // MOE_TS
    consts = [lw["ple_norm"], lw["ple_gate_w"], lw["ple_proj_w"], fin_g]
    grid_spec = pltpu.PrefetchScalarGridSpec(
        num_scalar_prefetch=3,
        grid=(n_tiles,),
        in_specs=[pl.BlockSpec((MOE_TS, D), lambda i, *_: (i, 0)),
                  pl.BlockSpec((MOE_TS, V7X_SUBLANES), lambda i, *_: (i, 0)),
                  pl.BlockSpec((MOE_TS, V7X_SUBLANES), lambda i, *_: (i, 0)),
                  pl.BlockSpec((1, 1, MOE_TS, PLE_DIM), lambda i, *_: (layer, i // n_seq, i % n_seq, 0))]
        + [_const_spec(c.shape) for c in consts] + [pl.BlockSpec(memory_space=pl.ANY)],
        out_specs=pl.BlockSpec((MOE_TS, D), lambda i, *_: (i, 0)),
        scratch_shapes=[pltpu.VMEM((2, N_SORTED * ROW_TILES, V7X_LANES), F32),
                        pltpu.SemaphoreType.DMA((2,))],
    )
    return pl.pallas_call(
        functools.partial(_combine_kernel, last=last),
        grid_spec=grid_spec,
        out_shape=jax.ShapeDtypeStruct((T, D), F32),
        compiler_params=pltpu.CompilerParams(
            dimension_semantics=("arbitrary",),
            vmem_limit_bytes=V7X_VMEM_BYTES - 16 * 1024 * 1024),
        name="combine",
    )(tables["run_src"], tables["run_n"], tables["run_dst"], x1, pos_tm, gate_tm, p, *consts, ys)


def _routing_tables(tile_counts, n_blocks):
    c = tile_counts[:, :, 0].astype(I32)
    counts = jnp.sum(c, axis=0)
    padded = (counts + MOE_BLOCK - 1) // MOE_BLOCK * MOE_BLOCK
    pad_end = jnp.cumsum(padded)
    pad_start = pad_end - padded
    run_dst = pad_start[None, :] + jnp.cumsum(c, axis=0) - c
    run_src = jnp.cumsum(c, axis=1) - c
    n_used = pad_end[-1] // MOE_BLOCK
    blk0 = jnp.arange(n_blocks, dtype=I32) * MOE_BLOCK
    be = jnp.minimum(jnp.sum(blk0[:, None] >= pad_end[None, :], axis=1), N_EXPERTS - 1)
    be = be[jnp.minimum(jnp.arange(n_blocks), n_used - 1)]
    return {
        "group_end": (pad_end // MOE_BLOCK).astype(I32),
        "run_src": run_src.reshape(-1).astype(I32), "run_n": c.reshape(-1),
        "run_dst": run_dst.reshape(-1).astype(I32),
        "pad_dst": (pad_start + counts).astype(I32), "pad_n": (padded - counts).astype(I32),
        "n_used": n_used.reshape(1).astype(I32), "block_e": be.astype(I32),
    }


def kernel(x, p, mix_norm, w_in, b_in, pool_w, pool_scale, conv_w, conv_b, conv_norm_g, conv_norm_b,
           sgu_norm_g, sgu_norm_b, sgu_w, sgu_b, branch_w, branch_b, w_out, moe_norm, router_w,
           router_b, expert_w_up, expert_b_up, expert_w_down, expert_b_down, ple_norm, ple_gate_w,
           ple_proj_w, final_norm):
    B, S, D = x.shape
    T = B * S
    depth = w_in.shape[0]
    assert D == D_MODEL and S % MIX_TS == 0 and w_in.shape[2] == IN_WIDTH
    n_blocks = -(-(T * TOP_K) // MOE_BLOCK) + N_EXPERTS
    row = lambda a: a.reshape(1, -1)
    b_up4 = expert_b_up[:, :, None, :]
    b_down4 = expert_b_down[:, :, None, :]
    for i in range(depth):
        lw = {
            "mix_norm": row(mix_norm[i]), "w_in": w_in[i].astype(BF16), "b_in": row(b_in[i]),
            "pool_w": pool_w[i].astype(BF16), "pool_scale": row(pool_scale[i]),
            "conv_w": conv_w[i], "conv_b": row(conv_b[i]),
            "conv_norm_g": row(conv_norm_g[i]), "conv_norm_b": row(conv_norm_b[i]),
            "sgu_norm_g": row(sgu_norm_g[i]), "sgu_norm_b": row(sgu_norm_b[i]),
            "sgu_w": sgu_w[i], "sgu_bt": sgu_b[i].T,
            "branch_w": branch_w[i].astype(BF16), "branch_b": branch_b[i],
            "w_out": w_out[i].astype(BF16), "moe_norm": row(moe_norm[i]),
            "router_wt": router_w[i].T, "router_b": router_b[i].reshape(-1, 1),
            "ple_norm": row(ple_norm[i]), "ple_gate_w": ple_gate_w[i].astype(BF16),
            "ple_proj_w": ple_proj_w[i].astype(BF16),
        }
        x1, h2b, pos8, gate8, tile_counts = _mixer_call(x, lw)
        tables = _routing_tables(tile_counts, n_blocks)
        xs = _dispatch_call(h2b, pos8, tables, n_blocks)
        ys = _moe_call(i, xs, tables, expert_w_up, b_up4, expert_w_down, b_down4)
        x = _combine_call(i, x1.reshape(T, D), ys, pos8.T, gate8.T, p, tables, lw,
                          row(final_norm), last=(i == depth - 1)).reshape(B, S, D)
    return x
```

```python
import functools

import jax
import jax.numpy as jnp
from jax import lax
from jax.experimental import pallas as pl
from jax.experimental.pallas import tpu as pltpu

F32 = jnp.float32
BF16 = jnp.bfloat16
I32 = jnp.int32
U32 = jnp.uint32

D_MODEL = 1024
POOL_WINDOWS = (2, 4, 8, 16)
POOL_CH = 128
BRANCH_WIDTH = 512
CONV_K = 31
SGU_CHUNK = 128
SGU_HEADS = 4
N_BRANCH = 3
N_EXPERTS = 32
TOP_K = 4
D_FF = 1024
SWIGLU_LIMIT = 7.0
SWIGLU_ALPHA = 1.702
MOE_BLOCK = 256
PLE_DIM = 256
EPS = 1e-6

V7X_SUBLANES = 8
V7X_LANES = 128
ROW_WORDS = D_MODEL // 2
ROW_SUB = ROW_WORDS // V7X_LANES
RUN_ALIGN = V7X_SUBLANES // ROW_SUB
V7X_VMEM_BYTES = 64 * 1024 * 1024

MIX_TS = 512
HALO = 32
CONV_ROWS = 32
SIDE_CHUNK = 256
MOE_TS = 512
N_STAGE = TOP_K * MOE_TS + N_EXPERTS * (RUN_ALIGN - 1)
STAGE_CHUNK = N_STAGE // 5
assert MIX_TS % MOE_TS == 0 and STAGE_CHUNK * 5 == N_STAGE and STAGE_CHUNK % V7X_SUBLANES == 0

C_POOL = 0
C_CONV = C_POOL + BRANCH_WIDTH
C_SGU_U = C_CONV + 2 * BRANCH_WIDTH
C_SGU_V = C_SGU_U + BRANCH_WIDTH
C_GATE = C_SGU_V + BRANCH_WIDTH
IN_WIDTH = C_GATE + N_BRANCH * D_MODEL
S_POOL = 0
S_SGU_U = S_POOL + BRANCH_WIDTH
S_SGU_V = S_SGU_U + BRANCH_WIDTH
S_GATE = S_SGU_V + BRANCH_WIDTH
SIDE_WIDTH = S_GATE + N_BRANCH * D_MODEL
assert S_SGU_U % SIDE_CHUNK == 0 and SIDE_WIDTH % SIDE_CHUNK == 0


def _rms_norm(x, g):
    return x * lax.rsqrt(jnp.mean(x * x, axis=-1, keepdims=True) + EPS) * g


def _layer_norm(x, g, b):
    mu = jnp.mean(x, axis=-1, keepdims=True)
    xc = x - mu
    var = jnp.mean(xc * xc, axis=-1, keepdims=True)
    return xc * lax.rsqrt(var + EPS) * g + b


def _sigmoid(x):
    return 0.5 * jnp.tanh(0.5 * x) + 0.5


def _dot(a, b):
    return jnp.dot(a, b, preferred_element_type=F32)


def _pack_rows(v):
    bits = lax.bitcast_convert_type(v, U32)
    return (bits[:, :ROW_WORDS] >> 16) | (bits[:, ROW_WORDS:] & jnp.uint32(0xFFFF0000))


def _unpack_rows(w):
    low = lax.bitcast_convert_type(w << 16, F32)
    high = lax.bitcast_convert_type(w & jnp.uint32(0xFFFF0000), F32)
    return jnp.concatenate([low, high], axis=-1).astype(BF16)


def _store_rows(ref_2d, first_row, words):
    n = words.shape[0]
    for j in range(ROW_SUB):
        ref_2d[pl.ds(first_row * ROW_SUB + j, n, stride=ROW_SUB), :] = words[:, j * V7X_LANES:(j + 1) * V7X_LANES]


def _load_rows(ref_2d, first_row, n):
    return jnp.concatenate(
        [ref_2d[pl.ds(first_row * ROW_SUB + j, n, stride=ROW_SUB), :] for j in range(ROW_SUB)], axis=-1)


def _mixer_kernel(x_ref, mix_g_ref, w_in_ref, b_in_ref, pool_w_ref, pool_scale_ref, conv_w_ref,
                  conv_b_ref, cn_g_ref, cn_b_ref, sn_g_ref, sn_b_ref, sgu_w_ref, sgu_bt_ref,
                  branch_w_ref, branch_b_ref, w_out_ref, moe_g_ref, rw_t_ref, rb_ref,
                  x1_ref, h2b_ref, pos_ref, gate_ref, counts_ref,
                  pool_hist, conv_hist, conv_shift, side_buf):
    ts = MIX_TS
    s = pl.program_id(1)

    @pl.when(s == 0)
    def _():
        pool_hist[0:HALO, :] = jnp.zeros((HALO, BRANCH_WIDTH), F32)
        conv_hist[0:HALO, :] = jnp.zeros((HALO, BRANCH_WIDTH), F32)

    x = x_ref[0]
    h = _rms_norm(x, mix_g_ref[...]).astype(BF16)

    def in_proj(c0, width):
        return _dot(h, w_in_ref[:, c0:c0 + width]) + b_in_ref[:, c0:c0 + width]

    zb = in_proj(C_CONV, 2 * BRANCH_WIDTH)
    conv_hist[HALO:HALO + ts, :] = zb[:, :BRANCH_WIDTH] * _sigmoid(zb[:, BRANCH_WIDTH:])
    n_shift_rows = HALO + ts - V7X_SUBLANES
    for sft in range(1, V7X_SUBLANES):
        conv_shift[sft - 1] = conv_hist[sft:sft + n_shift_rows, :]
    yb_parts = []
    n_conv_blocks = ts // CONV_ROWS
    n_side_chunks = SIDE_WIDTH // SIDE_CHUNK
    for bi in range(n_conv_blocks):
        r0 = bi * CONV_ROWS
        for ci in range(bi * n_side_chunks // n_conv_blocks, (bi + 1) * n_side_chunks // n_conv_blocks):
            d0 = ci * SIDE_CHUNK
            c0 = d0 if d0 < S_SGU_U else d0 + (C_SGU_U - S_SGU_U)
            side_buf[:, d0:d0 + SIDE_CHUNK] = in_proj(c0, SIDE_CHUNK)
        acc = jnp.zeros((CONV_ROWS, BRANCH_WIDTH), F32) + conv_b_ref[...]
        for k in range(CONV_K):
            off = HALO - (CONV_K - 1) + k + r0
            base, sft = off - off % V7X_SUBLANES, off % V7X_SUBLANES
            if sft == 0:
                window = conv_hist[base:base + CONV_ROWS, :]
            else:
                window = conv_shift[sft - 1, base:base + CONV_ROWS, :]
            acc = acc + conv_w_ref[k:k + 1, :] * window
        yb_rows = _layer_norm(acc, cn_g_ref[...], cn_b_ref[...])
        yb_parts.append((yb_rows * _sigmoid(yb_rows)).astype(BF16))
    yb = jnp.concatenate(yb_parts, axis=0)
    conv_hist[0:HALO, :] = conv_hist[ts:ts + HALO, :]

    za = side_buf[:, S_POOL:S_POOL + BRANCH_WIDTH]
    pool_hist[HALO:HALO + ts, :] = za
    row = lax.broadcasted_iota(I32, (ts, 1), 0) + s * ts
    mixed = []
    for g, w in enumerate(POOL_WINDOWS):
        c0 = g * POOL_CH
        cur = za[:, c0:c0 + POOL_CH]
        acc = cur
        for j in range(1, w):
            acc = acc + pool_hist[HALO - j:HALO - j + ts, c0:c0 + POOL_CH]
        count = jnp.minimum(row + 1, w).astype(F32)
        pooled = (acc / count - cur).astype(BF16)
        mixed.append(_dot(pooled, pool_w_ref[g]))
    ya = (jnp.concatenate(mixed, axis=-1) * pool_scale_ref[...]).astype(BF16)
    pool_hist[0:HALO, :] = pool_hist[ts:ts + HALO, :]

    zu = side_buf[:, S_SGU_U:S_SGU_U + BRANCH_WIDTH]
    zv = side_buf[:, S_SGU_V:S_SGU_V + BRANCH_WIDTH]
    v = _layer_norm(zv, sn_g_ref[...], sn_b_ref[...]).astype(BF16)
    tri = (lax.broadcasted_iota(I32, (SGU_CHUNK, SGU_CHUNK), 0)
           >= lax.broadcasted_iota(I32, (SGU_CHUNK, SGU_CHUNK), 1))
    chunks = []
    for c in range(ts // SGU_CHUNK):
        heads = []
        for hd in range(SGU_HEADS):
            w_tri = jnp.where(tri, sgu_w_ref[hd], 0.0).astype(BF16)
            vv = v[c * SGU_CHUNK:(c + 1) * SGU_CHUNK, hd * 128:(hd + 1) * 128]
            heads.append(_dot(w_tri, vv) + sgu_bt_ref[:, hd:hd + 1])
        chunks.append(jnp.concatenate(heads, axis=-1))
    yc = (zu * jnp.concatenate(chunks, axis=0)).astype(BF16)

    merged = jnp.zeros((ts, D_MODEL), F32)
    for k, yk in enumerate((ya, yb, yc)):
        proj = _dot(yk, branch_w_ref[k]) + branch_b_ref[k:k + 1, :]
        g0 = S_GATE + k * D_MODEL
        merged = merged + _sigmoid(side_buf[:, g0:g0 + D_MODEL]) * proj
    x1 = x + _dot(merged.astype(BF16), w_out_ref[...])
    x1_ref[0] = x1

    h2 = _rms_norm(x1, moe_g_ref[...])
    h2_hi = h2.astype(BF16)
    h2b_ref[...] = h2_hi
    h2_lo = (h2 - h2_hi.astype(F32)).astype(BF16)
    rw = rw_t_ref[...]
    rw_hi = rw.astype(BF16)
    rw_lo = (rw - rw_hi.astype(F32)).astype(BF16)
    nt = (((1,), (1,)), ((), ()))
    logits = (lax.dot_general(rw_hi, h2_hi, nt, preferred_element_type=F32)
              + lax.dot_general(rw_hi, h2_lo, nt, preferred_element_type=F32)
              + lax.dot_general(rw_lo, h2_hi, nt, preferred_element_type=F32)) + rb_ref[...]
    e_iota = lax.broadcasted_iota(I32, (N_EXPERTS, ts), 0).astype(F32)
    vals = logits
    top_v, sels = [], []
    for _k in range(TOP_K):
        m = jnp.max(vals, axis=0, keepdims=True)
        idx = jnp.min(jnp.where(vals == m, e_iota, float(N_EXPERTS)), axis=0, keepdims=True)
        sel = e_iota == idx
        vals = jnp.where(sel, -jnp.inf, vals)
        top_v.append(m)
        sels.append(sel)
    exps = [jnp.exp(tv - top_v[0]) for tv in top_v]
    denom = exps[0] + exps[1] + exps[2] + exps[3]
    chosen = jnp.zeros((N_EXPERTS, ts), F32)
    for sel in sels:
        chosen = chosen + jnp.where(sel, 1.0, 0.0)
    chosen_b = chosen.astype(BF16)
    t_row = lax.broadcasted_iota(I32, (ts, ts), 0)
    t_col = lax.broadcasted_iota(I32, (ts, ts), 1)
    before = jnp.where(t_row // MOE_TS == t_col // MOE_TS, jnp.where(t_row < t_col, 1.0, 0.0), 0.0)
    prefix = _dot(chosen_b, before.astype(BF16))
    lower = jnp.where(lax.broadcasted_iota(I32, (N_EXPERTS, N_EXPERTS), 0)
                      > lax.broadcasted_iota(I32, (N_EXPERTS, N_EXPERTS), 1), 1.0, 0.0).astype(BF16)
    base_parts = []
    for j in range(ts // MOE_TS):
        lanes = slice(j * MOE_TS, (j + 1) * MOE_TS)
        count = jnp.sum(chosen[:, lanes], axis=1, keepdims=True)
        run_rows = jnp.floor((count + (RUN_ALIGN - 1)) * (1.0 / RUN_ALIGN)) * RUN_ALIGN
        run_start = _dot(lower, jnp.broadcast_to(run_rows, (N_EXPERTS, V7X_LANES)).astype(BF16))[:, 0:1]
        base_parts.append(prefix[:, lanes] + run_start)
        counts_ref[j] = jnp.broadcast_to(count, (N_EXPERTS, V7X_LANES))
    base = jnp.concatenate(base_parts, axis=1)
    zeros4 = jnp.zeros((V7X_SUBLANES - TOP_K, ts), F32)
    pos = [jnp.sum(jnp.where(sel, base, 0.0), axis=0, keepdims=True) for sel in sels]
    pos_ref[...] = jnp.concatenate(pos + [zeros4], axis=0).astype(I32)
    gate_ref[...] = jnp.concatenate([e / denom for e in exps] + [zeros4], axis=0)


def _const_spec(shape):
    nd = len(shape)
    return pl.BlockSpec(shape, lambda *_: (0,) * nd, pipeline_mode=pl.Buffered(1))


def _mixer_call(x, lw):
    B, S, D = x.shape
    ts = MIX_TS
    n_s = S // ts
    T = B * S
    tok_blk = lambda b, s: (0, b * n_s + s)
    consts = [lw["mix_norm"], lw["w_in"], lw["b_in"], lw["pool_w"], lw["pool_scale"], lw["conv_w"],
              lw["conv_b"], lw["conv_norm_g"], lw["conv_norm_b"], lw["sgu_norm_g"], lw["sgu_norm_b"],
              lw["sgu_w"], lw["sgu_bt"], lw["branch_w"], lw["branch_b"], lw["w_out"], lw["moe_norm"],
              lw["router_wt"], lw["router_b"]]
    in_specs = [pl.BlockSpec((1, ts, D), lambda b, s: (b, s, 0))] + [_const_spec(c.shape) for c in consts]
    out_shape = (
        jax.ShapeDtypeStruct((B, S, D), F32),
        jax.ShapeDtypeStruct((T, D), BF16),
        jax.ShapeDtypeStruct((V7X_SUBLANES, T), I32),
        jax.ShapeDtypeStruct((V7X_SUBLANES, T), F32),
        jax.ShapeDtypeStruct((T // MOE_TS, N_EXPERTS, V7X_LANES), F32),
    )
    out_specs = (
        pl.BlockSpec((1, ts, D), lambda b, s: (b, s, 0)),
        pl.BlockSpec((ts, D), lambda b, s: (b * n_s + s, 0)),
        pl.BlockSpec((V7X_SUBLANES, ts), tok_blk),
        pl.BlockSpec((V7X_SUBLANES, ts), tok_blk),
        pl.BlockSpec((ts // MOE_TS, N_EXPERTS, V7X_LANES), lambda b, s: (b * n_s + s, 0, 0)),
    )
    return pl.pallas_call(
        _mixer_kernel,
        grid=(B, n_s),
        in_specs=in_specs,
        out_specs=out_specs,
        out_shape=out_shape,
        scratch_shapes=[pltpu.VMEM((HALO + ts, BRANCH_WIDTH), F32),
                        pltpu.VMEM((HALO + ts, BRANCH_WIDTH), F32),
                        pltpu.VMEM((V7X_SUBLANES - 1, HALO + ts - V7X_SUBLANES, BRANCH_WIDTH), F32),
                        pltpu.VMEM((ts, SIDE_WIDTH), F32)],
        compiler_params=pltpu.CompilerParams(
            dimension_semantics=("arbitrary", "arbitrary"),
            vmem_limit_bytes=V7X_VMEM_BYTES - 8 * 1024 * 1024),
        name="mixer",
    )(x, *consts)


def _row_span(first_row, n_rows):
    return pl.ds(pl.multiple_of(first_row * ROW_SUB, V7X_SUBLANES),
                 pl.multiple_of(n_rows * ROW_SUB, V7X_SUBLANES))


def _dispatch_kernel(run_src, run_n, run_dst, pad_dst, pad_n, n_used_ref, h2b_ref, pos_ref, xs_hbm,
                     stage, zero_rows, run_sem, fill_sem, *, n_blocks):
    i = pl.program_id(0)
    n_tiles = pl.num_programs(0)
    slot = i % 2
    block_rows = MOE_BLOCK * ROW_SUB

    def run_copy(tile, e, buf):
        k = tile * N_EXPERTS + e
        return pltpu.make_async_copy(stage.at[buf, _row_span(run_src[k], run_n[k])],
                                     xs_hbm.at[_row_span(run_dst[k], run_n[k])], run_sem.at[buf])

    def pad_copy(e):
        return pltpu.make_async_copy(zero_rows.at[_row_span(0, pad_n[e])],
                                     xs_hbm.at[_row_span(pad_dst[e], pad_n[e])], fill_sem)

    def idle_block_copy(blk):
        return pltpu.make_async_copy(
            zero_rows, xs_hbm.at[pl.ds(pl.multiple_of(blk * block_rows, block_rows), block_rows)], fill_sem)

    @pl.when(i == 0)
    def _():
        zero_rows[...] = jnp.zeros_like(zero_rows)

        def start_idle(blk, c):
            idle_block_copy(blk).start()
            return c

        def wait_idle(blk, c):
            idle_block_copy(blk).wait()
            return c

        for e in range(N_EXPERTS):
            pad_copy(e).start()
        lax.fori_loop(n_used_ref[0], n_blocks, start_idle, 0)
        for e in range(N_EXPERTS):
            pad_copy(e).wait()
        lax.fori_loop(n_used_ref[0], n_blocks, wait_idle, 0)

    @pl.when(i >= 2)
    def _():
        for e in range(N_EXPERTS):
            run_copy(i - 2, e, slot).wait()

    h2b = h2b_ref[...]
    for a0 in range(0, N_STAGE, STAGE_CHUNK):
        a_iota = lax.broadcasted_iota(I32, (STAGE_CHUNK, MOE_TS), 0) + a0
        onehot = jnp.zeros((STAGE_CHUNK, MOE_TS), F32)
        for k in range(TOP_K):
            onehot = jnp.where(a_iota == pos_ref[k:k + 1, :], 1.0, onehot)
        _store_rows(stage.at[slot], a0, _pack_rows(_dot(onehot.astype(BF16), h2b)))

    for e in range(N_EXPERTS):
        run_copy(i, e, slot).start()

    @pl.when(i == n_tiles - 1)
    def _():
        @pl.when(i >= 1)
        def _():
            for e in range(N_EXPERTS):
                run_copy(i - 1, e, 1 - slot).wait()
        for e in range(N_EXPERTS):
            run_copy(i, e, slot).wait()


def _dispatch_call(h2b, pos8, tables, n_blocks):
    T, D = h2b.shape
    n_tiles = T // MOE_TS
    grid_spec = pltpu.PrefetchScalarGridSpec(
        num_scalar_prefetch=6,
        grid=(n_tiles,),
        in_specs=[pl.BlockSpec((MOE_TS, D), lambda i, *_: (i, 0)),
                  pl.BlockSpec((V7X_SUBLANES, MOE_TS), lambda i, *_: (0, i))],
        out_specs=pl.BlockSpec(memory_space=pl.ANY),
        scratch_shapes=[pltpu.VMEM((2, N_STAGE * ROW_SUB, V7X_LANES), U32),
                        pltpu.VMEM((MOE_BLOCK * ROW_SUB, V7X_LANES), U32),
                        pltpu.SemaphoreType.DMA((2,)),
                        pltpu.SemaphoreType.DMA(())],
    )
    return pl.pallas_call(
        functools.partial(_dispatch_kernel, n_blocks=n_blocks),
        grid_spec=grid_spec,
        out_shape=jax.ShapeDtypeStruct((n_blocks * MOE_BLOCK * ROW_SUB, V7X_LANES), U32),
        compiler_params=pltpu.CompilerParams(
            dimension_semantics=("arbitrary",),
            vmem_limit_bytes=V7X_VMEM_BYTES - 16 * 1024 * 1024),
        name="dispatch",
    )(tables["run_src"], tables["run_n"], tables["run_dst"], tables["pad_dst"], tables["pad_n"],
      tables["n_used"], h2b, pos8)


def _moe_kernel(be_ref, group_end_ref, n_used_ref, xs_ref, wup_hbm, bup_ref, wdn_hbm, bdn_ref,
                ys_ref, wup_f32, wdn_f32, wup_bf, wdn_bf, group_count, wsem, *, layer):
    blk = pl.program_id(0)
    n_used = n_used_ref[0]
    active = blk < n_used
    prev = jnp.maximum(blk - 1, 0)
    expert = be_ref[blk]
    new_expert = (blk == 0) | (expert != be_ref[prev])

    def weight_copies(e, buf):
        return (pltpu.make_async_copy(wup_hbm.at[layer, e], wup_f32.at[buf], wsem.at[buf, 0]),
                pltpu.make_async_copy(wdn_hbm.at[layer, e], wdn_f32.at[buf], wsem.at[buf, 1]))

    @pl.when(blk == 0)
    def _():
        group_count[0] = 0
        for c in weight_copies(expert, 0):
            c.start()

    @pl.when(active & new_expert)
    def _():
        par = group_count[0] % 2
        group_count[0] = group_count[0] + 1
        next_blk = group_end_ref[expert]

        @pl.when(next_blk < n_used)
        def _():
            for c in weight_copies(be_ref[next_blk], 1 - par):
                c.start()

        for c in weight_copies(expert, par):
            c.wait()
        wup_bf[...] = wup_f32[par].astype(BF16)
        wdn_bf[...] = wdn_f32[par].astype(BF16)

    @pl.when(active)
    def _():
        xs = _unpack_rows(_load_rows(xs_ref, 0, MOE_BLOCK))
        gu = _dot(xs, wup_bf[...]) + bup_ref[0, 0]
        x_glu = jnp.minimum(gu[:, :D_FF], SWIGLU_LIMIT)
        x_lin = jnp.clip(gu[:, D_FF:], -SWIGLU_LIMIT, SWIGLU_LIMIT)
        act = x_glu * _sigmoid(SWIGLU_ALPHA * x_glu) * (x_lin + 1.0)
        y = _dot(act.astype(BF16), wdn_bf[...]) + bdn_ref[0, 0]
        _store_rows(ys_ref, 0, _pack_rows(y.astype(BF16).astype(F32)))

    @pl.when(jnp.logical_not(active))
    def _():
        ys_ref[...] = jnp.zeros_like(ys_ref)


def _moe_call(layer, xs, tables, w_up, b_up, w_down, b_down):
    n_blocks = tables["block_e"].shape[0]
    block_rows = MOE_BLOCK * ROW_SUB
    b_idx = lambda b, be, *_: (layer, be[b], 0, 0)
    grid_spec = pltpu.PrefetchScalarGridSpec(
        num_scalar_prefetch=3,
        grid=(n_blocks,),
        in_specs=[
            pl.BlockSpec((block_rows, V7X_LANES), lambda b, *_: (b, 0)),
            pl.BlockSpec(memory_space=pl.ANY),
            pl.BlockSpec((1, 1, 1, 2 * D_FF), b_idx),
            pl.BlockSpec(memory_space=pl.ANY),
            pl.BlockSpec((1, 1, 1, D_MODEL), b_idx),
        ],
        out_specs=pl.BlockSpec((block_rows, V7X_LANES), lambda b, *_: (b, 0)),
        scratch_shapes=[pltpu.VMEM((2, D_MODEL, 2 * D_FF), F32), pltpu.VMEM((2, D_FF, D_MODEL), F32),
                        pltpu.VMEM((D_MODEL, 2 * D_FF), BF16), pltpu.VMEM((D_FF, D_MODEL), BF16),
                        pltpu.SMEM((1,), I32), pltpu.SemaphoreType.DMA((2, 2))],
    )
    return pl.pallas_call(
        functools.partial(_moe_kernel, layer=layer),
        grid_spec=grid_spec,
        out_shape=jax.ShapeDtypeStruct(xs.shape, U32),
        compiler_params=pltpu.CompilerParams(
            dimension_semantics=("arbitrary",),
            vmem_limit_bytes=V7X_VMEM_BYTES - 8 * 1024 * 1024),
        name="moe",
    )(tables["block_e"], tables["group_end"], tables["n_used"], xs, w_up, b_up, w_down, b_down)


def _combine_kernel(run_src, run_n, run_dst, x1_ref, pos_ref, gate_ref, p_ref, ple_g_ref, gate_w_ref,
                    proj_w_ref, fin_g_ref, ys_hbm, out_ref, stage, run_sem, *, last):
    i = pl.program_id(0)
    n_tiles = pl.num_programs(0)
    slot = i % 2

    def run_copy(tile, e, buf):
        k = tile * N_EXPERTS + e
        return pltpu.make_async_copy(ys_hbm.at[_row_span(run_dst[k], run_n[k])],
                                     stage.at[buf, _row_span(run_src[k], run_n[k])], run_sem.at[buf])

    @pl.when(i == 0)
    def _():
        stage[...] = jnp.zeros_like(stage)
        for e in range(N_EXPERTS):
            run_copy(0, e, 0).start()

    @pl.when(i + 1 < n_tiles)
    def _():
        for e in range(N_EXPERTS):
            run_copy(i + 1, e, 1 - slot).start()

    for e in range(N_EXPERTS):
        run_copy(i, e, slot).wait()

    x2 = x1_ref[...]
    for a0 in range(0, N_STAGE, STAGE_CHUNK):
        a_iota = lax.broadcasted_iota(I32, (MOE_TS, STAGE_CHUNK), 1) + a0
        weights = jnp.zeros((MOE_TS, STAGE_CHUNK), F32)
        for k in range(TOP_K):
            weights = jnp.where(a_iota == pos_ref[:, k:k + 1], gate_ref[:, k:k + 1], weights)
        y_sorted = _unpack_rows(_load_rows(stage.at[slot], a0, STAGE_CHUNK))
        x2 = x2 + _dot(weights.astype(BF16), y_sorted)
    h3 = _rms_norm(x2, ple_g_ref[...]).astype(BF16)
    g = _sigmoid(_dot(h3, gate_w_ref[...]))
    pp = _dot(p_ref[0, 0].astype(BF16), proj_w_ref[...])
    x3 = x2 + g * pp
    if last:
        x3 = _rms_norm(x3, fin_g_ref[...])
    out_ref[...] = x3


def _combine_call(layer, x1, ys, pos_tm, gate_tm, p, tables, lw, fin_g, last):
    T, D = x1.shape
    n_tiles = T // MOE_TS
    n_seq = p.shape[2] // MOE_TS
    consts = [lw["ple_norm"], lw["ple_gate_w"], lw["ple_proj_w"], fin_g]
    grid_spec = pltpu.PrefetchScalarGridSpec(
        num_scalar_prefetch=3,
        grid=(n_tiles,),
        in_specs=[pl.BlockSpec((MOE_TS, D), lambda i, *_: (i, 0)),
                  pl.BlockSpec((MOE_TS, V7X_SUBLANES), lambda i, *_: (i, 0)),
                  pl.BlockSpec((MOE_TS, V7X_SUBLANES), lambda i, *_: (i, 0)),
                  pl.BlockSpec((1, 1, MOE_TS, PLE_DIM), lambda i, *_: (layer, i // n_seq, i % n_seq, 0))]
        + [_const_spec(c.shape) for c in consts] + [pl.BlockSpec(memory_space=pl.ANY)],
        out_specs=pl.BlockSpec((MOE_TS, D), lambda i, *_: (i, 0)),
        scratch_shapes=[pltpu.VMEM((2, N_STAGE * ROW_SUB, V7X_LANES), U32),
                        pltpu.SemaphoreType.DMA((2,))],
    )
    return pl.pallas_call(
        functools.partial(_combine_kernel, last=last),
        grid_spec=grid_spec,
        out_shape=jax.ShapeDtypeStruct((T, D), F32),
        compiler_params=pltpu.CompilerParams(
            dimension_semantics=("arbitrary",),
            vmem_limit_bytes=V7X_VMEM_BYTES - 16 * 1024 * 1024),
        name="combine",
    )(tables["run_src"], tables["run_n"], tables["run_dst"], x1, pos_tm, gate_tm, p, *consts, ys)


def _routing_tables(tile_counts, n_blocks):
    c = tile_counts[:, :, 0].astype(I32)
    c = (c + RUN_ALIGN - 1) // RUN_ALIGN * RUN_ALIGN
    counts = jnp.sum(c, axis=0)
    padded = (counts + MOE_BLOCK - 1) // MOE_BLOCK * MOE_BLOCK
    pad_end = jnp.cumsum(padded)
    pad_start = pad_end - padded
    run_dst = pad_start[None, :] + jnp.cumsum(c, axis=0) - c
    run_src = jnp.cumsum(c, axis=1) - c
    n_used = pad_end[-1] // MOE_BLOCK
    blk0 = jnp.arange(n_blocks, dtype=I32) * MOE_BLOCK
    be = jnp.minimum(jnp.sum(blk0[:, None] >= pad_end[None, :], axis=1), N_EXPERTS - 1)
    be = be[jnp.minimum(jnp.arange(n_blocks), n_used - 1)]
    return {
        "group_end": (pad_end // MOE_BLOCK).astype(I32),
        "run_src": run_src.reshape(-1).astype(I32), "run_n": c.reshape(-1),
        "run_dst": run_dst.reshape(-1).astype(I32),
        "pad_dst": (pad_start + counts).astype(I32), "pad_n": (padded - counts).astype(I32),
        "n_used": n_used.reshape(1).astype(I32), "block_e": be.astype(I32),
    }


def kernel(x, p, mix_norm, w_in, b_in, pool_w, pool_scale, conv_w, conv_b, conv_norm_g, conv_norm_b,
           sgu_norm_g, sgu_norm_b, sgu_w, sgu_b, branch_w, branch_b, w_out, moe_norm, router_w,
           router_b, expert_w_up, expert_b_up, expert_w_down, expert_b_down, ple_norm, ple_gate_w,
           ple_proj_w, final_norm):
    B, S, D = x.shape
    T = B * S
    depth = w_in.shape[0]
    assert D == D_MODEL and S % MIX_TS == 0 and w_in.shape[2] == IN_WIDTH
    max_rows = T * TOP_K + (T // MOE_TS) * N_EXPERTS * (RUN_ALIGN - 1)
    n_blocks = -(-max_rows // MOE_BLOCK) + N_EXPERTS
    row = lambda a: a.reshape(1, -1)
    b_up4 = expert_b_up[:, :, None, :]
    b_down4 = expert_b_down[:, :, None, :]
    for i in range(depth):
        lw = {
            "mix_norm": row(mix_norm[i]), "w_in": w_in[i].astype(BF16), "b_in": row(b_in[i]),
            "pool_w": pool_w[i].astype(BF16), "pool_scale": row(pool_scale[i]),
            "conv_w": conv_w[i], "conv_b": row(conv_b[i]),
            "conv_norm_g": row(conv_norm_g[i]), "conv_norm_b": row(conv_norm_b[i]),
            "sgu_norm_g": row(sgu_norm_g[i]), "sgu_norm_b": row(sgu_norm_b[i]),
            "sgu_w": sgu_w[i], "sgu_bt": sgu_b[i].T,
            "branch_w": branch_w[i].astype(BF16), "branch_b": branch_b[i],
            "w_out": w_out[i].astype(BF16), "moe_norm": row(moe_norm[i]),
            "router_wt": router_w[i].T, "router_b": router_b[i].reshape(-1, 1),
            "ple_norm": row(ple_norm[i]), "ple_gate_w": ple_gate_w[i].astype(BF16),
            "ple_proj_w": ple_proj_w[i].astype(BF16),
        }
        x1, h2b, pos8, gate8, tile_counts = _mixer_call(x, lw)
        tables = _routing_tables(tile_counts, n_blocks)
        xs = _dispatch_call(h2b, pos8, tables, n_blocks)
        ys = _moe_call(i, xs, tables, expert_w_up, b_up4, expert_w_down, b_down4)
        x = _combine_call(i, x1.reshape(T, D), ys, pos8.T, gate8.T, p, tables, lw,
                          row(final_norm), last=(i == depth - 1)).reshape(B, S, D)
    return x
```

```python
import functools

import jax
import jax.numpy as jnp
from jax import lax
from jax.experimental import pallas as pl
from jax.experimental.pallas import tpu as pltpu

F32 = jnp.float32
BF16 = jnp.bfloat16
I32 = jnp.int32
U32 = jnp.uint32

D_MODEL = 1024
POOL_WINDOWS = (2, 4, 8, 16)
POOL_CH = 128
BRANCH_WIDTH = 512
CONV_K = 31
SGU_CHUNK = 128
SGU_HEADS = 4
N_BRANCH = 3
N_EXPERTS = 32
TOP_K = 4
D_FF = 1024
SWIGLU_LIMIT = 7.0
SWIGLU_ALPHA = 1.702
MOE_BLOCK = 256
PLE_DIM = 256
EPS = 1e-6

V7X_SUBLANES = 8
V7X_LANES = 128
ROW_WORDS = D_MODEL // 2
ROW_SUB = ROW_WORDS // V7X_LANES
RUN_ALIGN = V7X_SUBLANES // ROW_SUB
V7X_VMEM_BYTES = 64 * 1024 * 1024

MIX_TS = 512
HALO = 32
CONV_ROWS = 32
SIDE_CHUNK = 256
MOE_TS = 256
N_STAGE = TOP_K * MOE_TS + N_EXPERTS * (RUN_ALIGN - 1)
STAGE_CHUNK = N_STAGE // 3
assert MIX_TS % MOE_TS == 0 and STAGE_CHUNK * 3 == N_STAGE and STAGE_CHUNK % V7X_SUBLANES == 0

C_POOL = 0
C_CONV = C_POOL + BRANCH_WIDTH
C_SGU_U = C_CONV + 2 * BRANCH_WIDTH
C_SGU_V = C_SGU_U + BRANCH_WIDTH
C_GATE = C_SGU_V + BRANCH_WIDTH
IN_WIDTH = C_GATE + N_BRANCH * D_MODEL
S_POOL = 0
S_SGU_U = S_POOL + BRANCH_WIDTH
S_SGU_V = S_SGU_U + BRANCH_WIDTH
S_GATE = S_SGU_V + BRANCH_WIDTH
SIDE_WIDTH = S_GATE + N_BRANCH * D_MODEL
assert S_SGU_U % SIDE_CHUNK == 0 and SIDE_WIDTH % SIDE_CHUNK == 0


def _rms_norm(x, g):
    return x * lax.rsqrt(jnp.mean(x * x, axis=-1, keepdims=True) + EPS) * g


def _layer_norm(x, g, b):
    mu = jnp.mean(x, axis=-1, keepdims=True)
    xc = x - mu
    var = jnp.mean(xc * xc, axis=-1, keepdims=True)
    return xc * lax.rsqrt(var + EPS) * g + b


def _sigmoid(x):
    return 0.5 * jnp.tanh(0.5 * x) + 0.5


def _dot(a, b):
    return jnp.dot(a, b, preferred_element_type=F32)


def _pack_rows(v):
    bits = lax.bitcast_convert_type(v, U32)
    return (bits[:, :ROW_WORDS] >> 16) | (bits[:, ROW_WORDS:] & jnp.uint32(0xFFFF0000))


def _unpack_rows(w):
    low = lax.bitcast_convert_type(w << 16, F32)
    high = lax.bitcast_convert_type(w & jnp.uint32(0xFFFF0000), F32)
    return jnp.concatenate([low, high], axis=-1).astype(BF16)


def _store_rows(ref_2d, first_row, words):
    n = words.shape[0]
    for j in range(ROW_SUB):
        ref_2d[pl.ds(first_row * ROW_SUB + j, n, stride=ROW_SUB), :] = words[:, j * V7X_LANES:(j + 1) * V7X_LANES]


def _load_rows(ref_2d, first_row, n):
    return jnp.concatenate(
        [ref_2d[pl.ds(first_row * ROW_SUB + j, n, stride=ROW_SUB), :] for j in range(ROW_SUB)], axis=-1)


def _mixer_kernel(x_ref, mix_g_ref, w_in_ref, b_in_ref, pool_w_ref, pool_scale_ref, conv_w_ref,
                  conv_b_ref, cn_g_ref, cn_b_ref, sn_g_ref, sn_b_ref, sgu_w_ref, sgu_bt_ref,
                  branch_w_ref, branch_b_ref, w_out_ref, moe_g_ref, rw_t_ref, rb_ref,
                  x1_ref, h2b_ref, pos_ref, gate_ref, counts_ref,
                  pool_hist, conv_hist, conv_shift, side_buf):
    ts = MIX_TS
    s = pl.program_id(1)

    @pl.when(s == 0)
    def _():
        pool_hist[0:HALO, :] = jnp.zeros((HALO, BRANCH_WIDTH), F32)
        conv_hist[0:HALO, :] = jnp.zeros((HALO, BRANCH_WIDTH), F32)

    x = x_ref[0]
    h = _rms_norm(x, mix_g_ref[...]).astype(BF16)

    def in_proj(c0, width):
        return _dot(h, w_in_ref[:, c0:c0 + width]) + b_in_ref[:, c0:c0 + width]

    zb = in_proj(C_CONV, 2 * BRANCH_WIDTH)
    conv_hist[HALO:HALO + ts, :] = zb[:, :BRANCH_WIDTH] * _sigmoid(zb[:, BRANCH_WIDTH:])
    n_shift_rows = HALO + ts - V7X_SUBLANES
    for sft in range(1, V7X_SUBLANES):
        conv_shift[sft - 1] = conv_hist[sft:sft + n_shift_rows, :]
    yb_parts = []
    n_conv_blocks = ts // CONV_ROWS
    n_side_chunks = SIDE_WIDTH // SIDE_CHUNK
    for bi in range(n_conv_blocks):
        r0 = bi * CONV_ROWS
        for ci in range(bi * n_side_chunks // n_conv_blocks, (bi + 1) * n_side_chunks // n_conv_blocks):
            d0 = ci * SIDE_CHUNK
            c0 = d0 if d0 < S_SGU_U else d0 + (C_SGU_U - S_SGU_U)
            side_buf[:, d0:d0 + SIDE_CHUNK] = in_proj(c0, SIDE_CHUNK)
        acc = jnp.zeros((CONV_ROWS, BRANCH_WIDTH), F32) + conv_b_ref[...]
        for k in range(CONV_K):
            off = HALO - (CONV_K - 1) + k + r0
            base, sft = off - off % V7X_SUBLANES, off % V7X_SUBLANES
            if sft == 0:
                window = conv_hist[base:base + CONV_ROWS, :]
            else:
                window = conv_shift[sft - 1, base:base + CONV_ROWS, :]
            acc = acc + conv_w_ref[k:k + 1, :] * window
        yb_rows = _layer_norm(acc, cn_g_ref[...], cn_b_ref[...])
        yb_parts.append((yb_rows * _sigmoid(yb_rows)).astype(BF16))
    yb = jnp.concatenate(yb_parts, axis=0)
    conv_hist[0:HALO, :] = conv_hist[ts:ts + HALO, :]

    za = side_buf[:, S_POOL:S_POOL + BRANCH_WIDTH]
    pool_hist[HALO:HALO + ts, :] = za
    row = lax.broadcasted_iota(I32, (ts, 1), 0) + s * ts
    mixed = []
    for g, w in enumerate(POOL_WINDOWS):
        c0 = g * POOL_CH
        cur = za[:, c0:c0 + POOL_CH]
        acc = cur
        for j in range(1, w):
            acc = acc + pool_hist[HALO - j:HALO - j + ts, c0:c0 + POOL_CH]
        count = jnp.minimum(row + 1, w).astype(F32)
        pooled = (acc / count - cur).astype(BF16)
        mixed.append(_dot(pooled, pool_w_ref[g]))
    ya = (jnp.concatenate(mixed, axis=-1) * pool_scale_ref[...]).astype(BF16)
    pool_hist[0:HALO, :] = pool_hist[ts:ts + HALO, :]

    zu = side_buf[:, S_SGU_U:S_SGU_U + BRANCH_WIDTH]
    zv = side_buf[:, S_SGU_V:S_SGU_V + BRANCH_WIDTH]
    v = _layer_norm(zv, sn_g_ref[...], sn_b_ref[...]).astype(BF16)
    tri = (lax.broadcasted_iota(I32, (SGU_CHUNK, SGU_CHUNK), 0)
           >= lax.broadcasted_iota(I32, (SGU_CHUNK, SGU_CHUNK), 1))
    chunks = []
    for c in range(ts // SGU_CHUNK):
        heads = []
        for hd in range(SGU_HEADS):
            w_tri = jnp.where(tri, sgu_w_ref[hd], 0.0).astype(BF16)
            vv = v[c * SGU_CHUNK:(c + 1) * SGU_CHUNK, hd * 128:(hd + 1) * 128]
            heads.append(_dot(w_tri, vv) + sgu_bt_ref[:, hd:hd + 1])
        chunks.append(jnp.concatenate(heads, axis=-1))
    yc = (zu * jnp.concatenate(chunks, axis=0)).astype(BF16)

    merged = jnp.zeros((ts, D_MODEL), F32)
    for k, yk in enumerate((ya, yb, yc)):
        proj = _dot(yk, branch_w_ref[k]) + branch_b_ref[k:k + 1, :]
        g0 = S_GATE + k * D_MODEL
        merged = merged + _sigmoid(side_buf[:, g0:g0 + D_MODEL]) * proj
    x1 = x + _dot(merged.astype(BF16), w_out_ref[...])
    x1_ref[0] = x1

    h2 = _rms_norm(x1, moe_g_ref[...])
    h2_hi = h2.astype(BF16)
    h2b_ref[...] = h2_hi
    h2_lo = (h2 - h2_hi.astype(F32)).astype(BF16)
    rw = rw_t_ref[...]
    rw_hi = rw.astype(BF16)
    rw_lo = (rw - rw_hi.astype(F32)).astype(BF16)
    nt = (((1,), (1,)), ((), ()))
    logits = (lax.dot_general(rw_hi, h2_hi, nt, preferred_element_type=F32)
              + lax.dot_general(rw_hi, h2_lo, nt, preferred_element_type=F32)
              + lax.dot_general(rw_lo, h2_hi, nt, preferred_element_type=F32)) + rb_ref[...]
    e_iota = lax.broadcasted_iota(I32, (N_EXPERTS, ts), 0).astype(F32)
    vals = logits
    top_v, sels = [], []
    for _k in range(TOP_K):
        m = jnp.max(vals, axis=0, keepdims=True)
        idx = jnp.min(jnp.where(vals == m, e_iota, float(N_EXPERTS)), axis=0, keepdims=True)
        sel = e_iota == idx
        vals = jnp.where(sel, -jnp.inf, vals)
        top_v.append(m)
        sels.append(sel)
    exps = [jnp.exp(tv - top_v[0]) for tv in top_v]
    denom = exps[0] + exps[1] + exps[2] + exps[3]
    chosen = jnp.zeros((N_EXPERTS, ts), F32)
    for sel in sels:
        chosen = chosen + jnp.where(sel, 1.0, 0.0)
    chosen_b = chosen.astype(BF16)
    t_row = lax.broadcasted_iota(I32, (ts, ts), 0)
    t_col = lax.broadcasted_iota(I32, (ts, ts), 1)
    before = jnp.where(t_row // MOE_TS == t_col // MOE_TS, jnp.where(t_row < t_col, 1.0, 0.0), 0.0)
    prefix = _dot(chosen_b, before.astype(BF16))
    lower = jnp.where(lax.broadcasted_iota(I32, (N_EXPERTS, N_EXPERTS), 0)
                      > lax.broadcasted_iota(I32, (N_EXPERTS, N_EXPERTS), 1), 1.0, 0.0).astype(BF16)
    base_parts = []
    for j in range(ts // MOE_TS):
        lanes = slice(j * MOE_TS, (j + 1) * MOE_TS)
        count = jnp.sum(chosen[:, lanes], axis=1, keepdims=True)
        run_rows = jnp.floor((count + (RUN_ALIGN - 1)) * (1.0 / RUN_ALIGN)) * RUN_ALIGN
        run_start = _dot(lower, jnp.broadcast_to(run_rows, (N_EXPERTS, V7X_LANES)).astype(BF16))[:, 0:1]
        base_parts.append(prefix[:, lanes] + run_start)
        counts_ref[j] = jnp.broadcast_to(count, (N_EXPERTS, V7X_LANES))
    base = jnp.concatenate(base_parts, axis=1)
    zeros4 = jnp.zeros((V7X_SUBLANES - TOP_K, ts), F32)
    pos = [jnp.sum(jnp.where(sel, base, 0.0), axis=0, keepdims=True) for sel in sels]
    pos_ref[...] = jnp.concatenate(pos + [zeros4], axis=0).astype(I32)
    gate_ref[...] = jnp.concatenate([e / denom for e in exps] + [zeros4], axis=0)


def _const_spec(shape):
    nd = len(shape)
    return pl.BlockSpec(shape, lambda *_: (0,) * nd, pipeline_mode=pl.Buffered(1))


def _mixer_call(x, lw):
    B, S, D = x.shape
    ts = MIX_TS
    n_s = S // ts
    T = B * S
    tok_blk = lambda b, s: (0, b * n_s + s)
    consts = [lw["mix_norm"], lw["w_in"], lw["b_in"], lw["pool_w"], lw["pool_scale"], lw["conv_w"],
              lw["conv_b"], lw["conv_norm_g"], lw["conv_norm_b"], lw["sgu_norm_g"], lw["sgu_norm_b"],
              lw["sgu_w"], lw["sgu_bt"], lw["branch_w"], lw["branch_b"], lw["w_out"], lw["moe_norm"],
              lw["router_wt"], lw["router_b"]]
    in_specs = [pl.BlockSpec((1, ts, D), lambda b, s: (b, s, 0))] + [_const_spec(c.shape) for c in consts]
    out_shape = (
        jax.ShapeDtypeStruct((B, S, D), F32),
        jax.ShapeDtypeStruct((T, D), BF16),
        jax.ShapeDtypeStruct((V7X_SUBLANES, T), I32),
        jax.ShapeDtypeStruct((V7X_SUBLANES, T), F32),
        jax.ShapeDtypeStruct((T // MOE_TS, N_EXPERTS, V7X_LANES), F32),
    )
    out_specs = (
        pl.BlockSpec((1, ts, D), lambda b, s: (b, s, 0)),
        pl.BlockSpec((ts, D), lambda b, s: (b * n_s + s, 0)),
        pl.BlockSpec((V7X_SUBLANES, ts), tok_blk),
        pl.BlockSpec((V7X_SUBLANES, ts), tok_blk),
        pl.BlockSpec((ts // MOE_TS, N_EXPERTS, V7X_LANES), lambda b, s: (b * n_s + s, 0, 0)),
    )
    return pl.pallas_call(
        _mixer_kernel,
        grid=(B, n_s),
        in_specs=in_specs,
        out_specs=out_specs,
        out_shape=out_shape,
        scratch_shapes=[pltpu.VMEM((HALO + ts, BRANCH_WIDTH), F32),
                        pltpu.VMEM((HALO + ts, BRANCH_WIDTH), F32),
                        pltpu.VMEM((V7X_SUBLANES - 1, HALO + ts - V7X_SUBLANES, BRANCH_WIDTH), F32),
                        pltpu.VMEM((ts, SIDE_WIDTH), F32)],
        compiler_params=pltpu.CompilerParams(
            dimension_semantics=("arbitrary", "arbitrary"),
            vmem_limit_bytes=V7X_VMEM_BYTES - 8 * 1024 * 1024),
        name="mixer",
    )(x, *consts)


def _row_span(first_row, n_rows):
    return pl.ds(pl.multiple_of(first_row * ROW_SUB, V7X_SUBLANES),
                 pl.multiple_of(n_rows * ROW_SUB, V7X_SUBLANES))


def _dispatch_kernel(run_src, run_n, run_dst, pad_dst, pad_n, n_used_ref, h2b_ref, pos_ref, xs_hbm,
                     stage, zero_rows, run_sem, fill_sem, *, n_blocks):
    i = pl.program_id(0)
    n_tiles = pl.num_programs(0)
    slot = i % 2
    block_rows = MOE_BLOCK * ROW_SUB

    def run_copy(tile, e, buf):
        k = tile * N_EXPERTS + e
        return pltpu.make_async_copy(stage.at[buf, _row_span(run_src[k], run_n[k])],
                                     xs_hbm.at[_row_span(run_dst[k], run_n[k])], run_sem.at[buf])

    def pad_copy(e):
        return pltpu.make_async_copy(zero_rows.at[_row_span(0, pad_n[e])],
                                     xs_hbm.at[_row_span(pad_dst[e], pad_n[e])], fill_sem)

    def idle_block_copy(blk):
        return pltpu.make_async_copy(
            zero_rows, xs_hbm.at[pl.ds(pl.multiple_of(blk * block_rows, block_rows), block_rows)], fill_sem)

    @pl.when(i == 0)
    def _():
        zero_rows[...] = jnp.zeros_like(zero_rows)

        def start_idle(blk, c):
            idle_block_copy(blk).start()
            return c

        def wait_idle(blk, c):
            idle_block_copy(blk).wait()
            return c

        for e in range(N_EXPERTS):
            pad_copy(e).start()
        lax.fori_loop(n_used_ref[0], n_blocks, start_idle, 0)
        for e in range(N_EXPERTS):
            pad_copy(e).wait()
        lax.fori_loop(n_used_ref[0], n_blocks, wait_idle, 0)

    @pl.when(i >= 2)
    def _():
        for e in range(N_EXPERTS):
            run_copy(i - 2, e, slot).wait()

    h2b = h2b_ref[...]
    for a0 in range(0, N_STAGE, STAGE_CHUNK):
        a_iota = lax.broadcasted_iota(I32, (STAGE_CHUNK, MOE_TS), 0) + a0
        onehot = jnp.zeros((STAGE_CHUNK, MOE_TS), F32)
        for k in range(TOP_K):
            onehot = jnp.where(a_iota == pos_ref[k:k + 1, :], 1.0, onehot)
        _store_rows(stage.at[slot], a0, _pack_rows(_dot(onehot.astype(BF16), h2b)))

    for e in range(N_EXPERTS):
        run_copy(i, e, slot).start()

    @pl.when(i == n_tiles - 1)
    def _():
        @pl.when(i >= 1)
        def _():
            for e in range(N_EXPERTS):
                run_copy(i - 1, e, 1 - slot).wait()
        for e in range(N_EXPERTS):
            run_copy(i, e, slot).wait()


def _dispatch_call(h2b, pos8, tables, n_blocks):
    T, D = h2b.shape
    n_tiles = T // MOE_TS
    grid_spec = pltpu.PrefetchScalarGridSpec(
        num_scalar_prefetch=6,
        grid=(n_tiles,),
        in_specs=[pl.BlockSpec((MOE_TS, D), lambda i, *_: (i, 0)),
                  pl.BlockSpec((V7X_SUBLANES, MOE_TS), lambda i, *_: (0, i))],
        out_specs=pl.BlockSpec(memory_space=pl.ANY),
        scratch_shapes=[pltpu.VMEM((2, N_STAGE * ROW_SUB, V7X_LANES), U32),
                        pltpu.VMEM((MOE_BLOCK * ROW_SUB, V7X_LANES), U32),
                        pltpu.SemaphoreType.DMA((2,)),
                        pltpu.SemaphoreType.DMA(())],
    )
    return pl.pallas_call(
        functools.partial(_dispatch_kernel, n_blocks=n_blocks),
        grid_spec=grid_spec,
        out_shape=jax.ShapeDtypeStruct((n_blocks * MOE_BLOCK * ROW_SUB, V7X_LANES), U32),
        compiler_params=pltpu.CompilerParams(
            dimension_semantics=("arbitrary",),
            vmem_limit_bytes=V7X_VMEM_BYTES - 16 * 1024 * 1024),
        name="dispatch",
    )(tables["run_src"], tables["run_n"], tables["run_dst"], tables["pad_dst"], tables["pad_n"],
      tables["n_used"], h2b, pos8)


def _moe_kernel(be_ref, group_end_ref, n_used_ref, xs_ref, wup_hbm, bup_ref, wdn_hbm, bdn_ref,
                ys_ref, wup_f32, wdn_f32, wup_bf, wdn_bf, group_count, wsem, *, layer):
    blk = pl.program_id(0)
    n_used = n_used_ref[0]
    active = blk < n_used
    prev = jnp.maximum(blk - 1, 0)
    expert = be_ref[blk]
    new_expert = (blk == 0) | (expert != be_ref[prev])

    def weight_copies(e, buf):
        return (pltpu.make_async_copy(wup_hbm.at[layer, e], wup_f32.at[buf], wsem.at[buf, 0]),
                pltpu.make_async_copy(wdn_hbm.at[layer, e], wdn_f32.at[buf], wsem.at[buf, 1]))

    @pl.when(blk == 0)
    def _():
        group_count[0] = 0
        for c in weight_copies(expert, 0):
            c.start()

    @pl.when(active & new_expert)
    def _():
        par = group_count[0] % 2
        group_count[0] = group_count[0] + 1
        next_blk = group_end_ref[expert]

        @pl.when(next_blk < n_used)
        def _():
            for c in weight_copies(be_ref[next_blk], 1 - par):
                c.start()

        for c in weight_copies(expert, par):
            c.wait()
        wup_bf[...] = wup_f32[par].astype(BF16)
        wdn_bf[...] = wdn_f32[par].astype(BF16)

    @pl.when(active)
    def _():
        xs = _unpack_rows(_load_rows(xs_ref, 0, MOE_BLOCK))
        gu = _dot(xs, wup_bf[...]) + bup_ref[0, 0]
        x_glu = jnp.minimum(gu[:, :D_FF], SWIGLU_LIMIT)
        x_lin = jnp.clip(gu[:, D_FF:], -SWIGLU_LIMIT, SWIGLU_LIMIT)
        act = x_glu * _sigmoid(SWIGLU_ALPHA * x_glu) * (x_lin + 1.0)
        y = _dot(act.astype(BF16), wdn_bf[...]) + bdn_ref[0, 0]
        _store_rows(ys_ref, 0, _pack_rows(y.astype(BF16).astype(F32)))

    @pl.when(jnp.logical_not(active))
    def _():
        ys_ref[...] = jnp.zeros_like(ys_ref)


def _moe_call(layer, xs, tables, w_up, b_up, w_down, b_down):
    n_blocks = tables["block_e"].shape[0]
    block_rows = MOE_BLOCK * ROW_SUB
    b_idx = lambda b, be, *_: (layer, be[b], 0, 0)
    grid_spec = pltpu.PrefetchScalarGridSpec(
        num_scalar_prefetch=3,
        grid=(n_blocks,),
        in_specs=[
            pl.BlockSpec((block_rows, V7X_LANES), lambda b, *_: (b, 0)),
            pl.BlockSpec(memory_space=pl.ANY),
            pl.BlockSpec((1, 1, 1, 2 * D_FF), b_idx),
            pl.BlockSpec(memory_space=pl.ANY),
            pl.BlockSpec((1, 1, 1, D_MODEL), b_idx),
        ],
        out_specs=pl.BlockSpec((block_rows, V7X_LANES), lambda b, *_: (b, 0)),
        scratch_shapes=[pltpu.VMEM((2, D_MODEL, 2 * D_FF), F32), pltpu.VMEM((2, D_FF, D_MODEL), F32),
                        pltpu.VMEM((D_MODEL, 2 * D_FF), BF16), pltpu.VMEM((D_FF, D_MODEL), BF16),
                        pltpu.SMEM((1,), I32), pltpu.SemaphoreType.DMA((2, 2))],
    )
    return pl.pallas_call(
        functools.partial(_moe_kernel, layer=layer),
        grid_spec=grid_spec,
        out_shape=jax.ShapeDtypeStruct(xs.shape, U32),
        compiler_params=pltpu.CompilerParams(
            dimension_semantics=("arbitrary",),
            vmem_limit_bytes=V7X_VMEM_BYTES - 8 * 1024 * 1024),
        name="moe",
    )(tables["block_e"], tables["group_end"], tables["n_used"], xs, w_up, b_up, w_down, b_down)


def _combine_kernel(run_src, run_n, run_dst, x1_ref, pos_ref, gate_ref, p_ref, ple_g_ref, gate_w_ref,
                    proj_w_ref, fin_g_ref, ys_hbm, out_ref, stage, run_sem, *, last):
    i = pl.program_id(0)
    n_tiles = pl.num_programs(0)
    slot = i % 2

    def run_copy(tile, e, buf):
        k = tile * N_EXPERTS + e
        return pltpu.make_async_copy(ys_hbm.at[_row_span(run_dst[k], run_n[k])],
                                     stage.at[buf, _row_span(run_src[k], run_n[k])], run_sem.at[buf])

    @pl.when(i == 0)
    def _():
        stage[...] = jnp.zeros_like(stage)
        for e in range(N_EXPERTS):
            run_copy(0, e, 0).start()

    @pl.when(i + 1 < n_tiles)
    def _():
        for e in range(N_EXPERTS):
            run_copy(i + 1, e, 1 - slot).start()

    for e in range(N_EXPERTS):
        run_copy(i, e, slot).wait()

    x2 = x1_ref[...]
    for a0 in range(0, N_STAGE, STAGE_CHUNK):
        a_iota = lax.broadcasted_iota(I32, (MOE_TS, STAGE_CHUNK), 1) + a0
        weights = jnp.zeros((MOE_TS, STAGE_CHUNK), F32)
        for k in range(TOP_K):
            weights = jnp.where(a_iota == pos_ref[:, k:k + 1], gate_ref[:, k:k + 1], weights)
        y_sorted = _unpack_rows(_load_rows(stage.at[slot], a0, STAGE_CHUNK))
        x2 = x2 + _dot(weights.astype(BF16), y_sorted)
    h3 = _rms_norm(x2, ple_g_ref[...]).astype(BF16)
    g = _sigmoid(_dot(h3, gate_w_ref[...]))
    pp = _dot(p_ref[0, 0].astype(BF16), proj_w_ref[...])
    x3 = x2 + g * pp
    if last:
        x3 = _rms_norm(x3, fin_g_ref[...])
    out_ref[...] = x3


def _combine_call(layer, x1, ys, pos_tm, gate_tm, p, tables, lw, fin_g, last):
    T, D = x1.shape
    n_tiles = T // MOE_TS
    n_seq = p.shape[2] // MOE_TS
    consts = [lw["ple_norm"], lw["ple_gate_w"], lw["ple_proj_w"], fin_g]
    grid_spec = pltpu.PrefetchScalarGridSpec(
        num_scalar_prefetch=3,
        grid=(n_tiles,),
        in_specs=[pl.BlockSpec((MOE_TS, D), lambda i, *_: (i, 0)),
                  pl.BlockSpec((MOE_TS, V7X_SUBLANES), lambda i, *_: (i, 0)),
                  pl.BlockSpec((MOE_TS, V7X_SUBLANES), lambda i, *_: (i, 0)),
                  pl.BlockSpec((1, 1, MOE_TS, PLE_DIM), lambda i, *_: (layer, i // n_seq, i % n_seq, 0))]
        + [_const_spec(c.shape) for c in consts] + [pl.BlockSpec(memory_space=pl.ANY)],
        out_specs=pl.BlockSpec((MOE_TS, D), lambda i, *_: (i, 0)),
        scratch_shapes=[pltpu.VMEM((2, N_STAGE * ROW_SUB, V7X_LANES), U32),
                        pltpu.SemaphoreType.DMA((2,))],
    )
    return pl.pallas_call(
        functools.partial(_combine_kernel, last=last),
        grid_spec=grid_spec,
        out_shape=jax.ShapeDtypeStruct((T, D), F32),
        compiler_params=pltpu.CompilerParams(
            dimension_semantics=("arbitrary",),
            vmem_limit_bytes=V7X_VMEM_BYTES - 16 * 1024 * 1024),
        name="combine",
    )(tables["run_src"], tables["run_n"], tables["run_dst"], x1, pos_tm, gate_tm, p, *consts, ys)


def _routing_tables(tile_counts, n_blocks):
    c = tile_counts[:, :, 0].astype(I32)
    c = (c + RUN_ALIGN - 1) // RUN_ALIGN * RUN_ALIGN
    counts = jnp.sum(c, axis=0)
    padded = (counts + MOE_BLOCK - 1) // MOE_BLOCK * MOE_BLOCK
    pad_end = jnp.cumsum(padded)
    pad_start = pad_end - padded
    run_dst = pad_start[None, :] + jnp.cumsum(c, axis=0) - c
    run_src = jnp.cumsum(c, axis=1) - c
    n_used = pad_end[-1] // MOE_BLOCK
    blk0 = jnp.arange(n_blocks, dtype=I32) * MOE_BLOCK
    be = jnp.minimum(jnp.sum(blk0[:, None] >= pad_end[None, :], axis=1), N_EXPERTS - 1)
    be = be[jnp.minimum(jnp.arange(n_blocks), n_used - 1)]
    return {
        "group_end": (pad_end // MOE_BLOCK).astype(I32),
        "run_src": run_src.reshape(-1).astype(I32), "run_n": c.reshape(-1),
        "run_dst": run_dst.reshape(-1).astype(I32),
        "pad_dst": (pad_start + counts).astype(I32), "pad_n": (padded - counts).astype(I32),
        "n_used": n_used.reshape(1).astype(I32), "block_e": be.astype(I32),
    }


def kernel(x, p, mix_norm, w_in, b_in, pool_w, pool_scale, conv_w, conv_b, conv_norm_g, conv_norm_b,
           sgu_norm_g, sgu_norm_b, sgu_w, sgu_b, branch_w, branch_b, w_out, moe_norm, router_w,
           router_b, expert_w_up, expert_b_up, expert_w_down, expert_b_down, ple_norm, ple_gate_w,
           ple_proj_w, final_norm):
    B, S, D = x.shape
    T = B * S
    depth = w_in.shape[0]
    assert D == D_MODEL and S % MIX_TS == 0 and w_in.shape[2] == IN_WIDTH
    max_rows = T * TOP_K + (T // MOE_TS) * N_EXPERTS * (RUN_ALIGN - 1)
    n_blocks = -(-max_rows // MOE_BLOCK) + N_EXPERTS
    row = lambda a: a.reshape(1, -1)
    b_up4 = expert_b_up[:, :, None, :]
    b_down4 = expert_b_down[:, :, None, :]
    for i in range(depth):
        lw = {
            "mix_norm": row(mix_norm[i]), "w_in": w_in[i].astype(BF16), "b_in": row(b_in[i]),
            "pool_w": pool_w[i].astype(BF16), "pool_scale": row(pool_scale[i]),
            "conv_w": conv_w[i], "conv_b": row(conv_b[i]),
            "conv_norm_g": row(conv_norm_g[i]), "conv_norm_b": row(conv_norm_b[i]),
            "sgu_norm_g": row(sgu_norm_g[i]), "sgu_norm_b": row(sgu_norm_b[i]),
            "sgu_w": sgu_w[i], "sgu_bt": sgu_b[i].T,
            "branch_w": branch_w[i].astype(BF16), "branch_b": branch_b[i],
            "w_out": w_out[i].astype(BF16), "moe_norm": row(moe_norm[i]),
            "router_wt": router_w[i].T, "router_b": router_b[i].reshape(-1, 1),
            "ple_norm": row(ple_norm[i]), "ple_gate_w": ple_gate_w[i].astype(BF16),
            "ple_proj_w": ple_proj_w[i].astype(BF16),
        }
        x1, h2b, pos8, gate8, tile_counts = _mixer_call(x, lw)
        tables = _routing_tables(tile_counts, n_blocks)
        xs = _dispatch_call(h2b, pos8, tables, n_blocks)
        ys = _moe_call(i, xs, tables, expert_w_up, b_up4, expert_w_down, b_down4)
        x = _combine_call(i, x1.reshape(T, D), ys, pos8.T, gate8.T, p, tables, lw,
                          row(final_norm), last=(i == depth - 1)).reshape(B, S, D)
    return x
```

```python
import functools

import jax
import jax.numpy as jnp
from jax import lax
from jax.experimental import pallas as pl
from jax.experimental.pallas import tpu as pltpu

F32 = jnp.float32
BF16 = jnp.bfloat16
I32 = jnp.int32
U32 = jnp.uint32

D_MODEL = 1024
POOL_WINDOWS = (2, 4, 8, 16)
POOL_CH = 128
BRANCH_WIDTH = 512
CONV_K = 31
SGU_CHUNK = 128
SGU_HEADS = 4
N_BRANCH = 3
N_EXPERTS = 32
TOP_K = 4
D_FF = 1024
SWIGLU_LIMIT = 7.0
SWIGLU_ALPHA = 1.702
MOE_BLOCK = 256
PLE_DIM = 256
EPS = 1e-6

V7X_SUBLANES = 8
V7X_LANES = 128
ROW_WORDS = D_MODEL // 2
ROW_SUB = ROW_WORDS // V7X_LANES
RUN_ALIGN = V7X_SUBLANES // ROW_SUB
V7X_VMEM_BYTES = 64 * 1024 * 1024

MIX_TS = 512
HALO = 32
CONV_ROWS = 32
SIDE_CHUNK = 256
MOE_TS = 256
N_STAGE = TOP_K * MOE_TS + N_EXPERTS * (RUN_ALIGN - 1)
STAGE_CHUNK = N_STAGE // 3
assert MIX_TS % MOE_TS == 0 and STAGE_CHUNK * 3 == N_STAGE and STAGE_CHUNK % V7X_SUBLANES == 0

C_POOL = 0
C_CONV = C_POOL + BRANCH_WIDTH
C_SGU_U = C_CONV + 2 * BRANCH_WIDTH
C_SGU_V = C_SGU_U + BRANCH_WIDTH
C_GATE = C_SGU_V + BRANCH_WIDTH
IN_WIDTH = C_GATE + N_BRANCH * D_MODEL
S_POOL = 0
S_SGU_U = S_POOL + BRANCH_WIDTH
S_SGU_V = S_SGU_U + BRANCH_WIDTH
S_GATE = S_SGU_V + BRANCH_WIDTH
SIDE_WIDTH = S_GATE + N_BRANCH * D_MODEL
assert S_SGU_U % SIDE_CHUNK == 0 and SIDE_WIDTH % SIDE_CHUNK == 0


def _rms_norm(x, g):
    return x * lax.rsqrt(jnp.mean(x * x, axis=-1, keepdims=True) + EPS) * g


def _layer_norm(x, g, b):
    mu = jnp.mean(x, axis=-1, keepdims=True)
    xc = x - mu
    var = jnp.mean(xc * xc, axis=-1, keepdims=True)
    return xc * lax.rsqrt(var + EPS) * g + b


def _sigmoid(x):
    return 0.5 * jnp.tanh(0.5 * x) + 0.5


def _dot(a, b):
    return jnp.dot(a, b, preferred_element_type=F32)


def _pack_rows(v):
    bits = lax.bitcast_convert_type(v, U32)
    return (bits[:, :ROW_WORDS] >> 16) | (bits[:, ROW_WORDS:] & jnp.uint32(0xFFFF0000))


def _unpack_rows(w):
    low = lax.bitcast_convert_type(w << 16, F32)
    high = lax.bitcast_convert_type(w & jnp.uint32(0xFFFF0000), F32)
    return jnp.concatenate([low, high], axis=-1).astype(BF16)


def _store_rows(ref_2d, first_row, words):
    n = words.shape[0]
    for j in range(ROW_SUB):
        ref_2d[pl.ds(first_row * ROW_SUB + j, n, stride=ROW_SUB), :] = words[:, j * V7X_LANES:(j + 1) * V7X_LANES]


def _load_rows(ref_2d, first_row, n):
    return jnp.concatenate(
        [ref_2d[pl.ds(first_row * ROW_SUB + j, n, stride=ROW_SUB), :] for j in range(ROW_SUB)], axis=-1)


def _mixer_kernel(x_ref, mix_g_ref, w_in_ref, b_in_ref, pool_w_ref, pool_scale_ref, conv_w_ref,
                  conv_b_ref, cn_g_ref, cn_b_ref, sn_g_ref, sn_b_ref, sgu_w_ref, sgu_bt_ref,
                  branch_w_ref, branch_b_ref, w_out_ref, moe_g_ref, rw_t_ref, rb_ref,
                  x1_ref, h2b_ref, pos_ref, gate_ref, counts_ref,
                  pool_hist, conv_hist, conv_shift, side_buf):
    ts = MIX_TS
    s = pl.program_id(1)

    @pl.when(s == 0)
    def _():
        pool_hist[0:HALO, :] = jnp.zeros((HALO, BRANCH_WIDTH), F32)
        conv_hist[0:HALO, :] = jnp.zeros((HALO, BRANCH_WIDTH), F32)

    x = x_ref[0]
    h = _rms_norm(x, mix_g_ref[...]).astype(BF16)

    def in_proj(c0, width):
        return _dot(h, w_in_ref[:, c0:c0 + width]) + b_in_ref[:, c0:c0 + width]

    zb = in_proj(C_CONV, 2 * BRANCH_WIDTH)
    conv_hist[HALO:HALO + ts, :] = zb[:, :BRANCH_WIDTH] * _sigmoid(zb[:, BRANCH_WIDTH:])
    n_shift_rows = HALO + ts - V7X_SUBLANES
    for sft in range(1, V7X_SUBLANES):
        conv_shift[sft - 1] = conv_hist[sft:sft + n_shift_rows, :]
    yb_parts = []
    n_conv_blocks = ts // CONV_ROWS
    n_side_chunks = SIDE_WIDTH // SIDE_CHUNK
    for bi in range(n_conv_blocks):
        r0 = bi * CONV_ROWS
        for ci in range(bi * n_side_chunks // n_conv_blocks, (bi + 1) * n_side_chunks // n_conv_blocks):
            d0 = ci * SIDE_CHUNK
            c0 = d0 if d0 < S_SGU_U else d0 + (C_SGU_U - S_SGU_U)
            side_buf[:, d0:d0 + SIDE_CHUNK] = in_proj(c0, SIDE_CHUNK)
        acc = jnp.zeros((CONV_ROWS, BRANCH_WIDTH), F32) + conv_b_ref[...]
        for k in range(CONV_K):
            off = HALO - (CONV_K - 1) + k + r0
            base, sft = off - off % V7X_SUBLANES, off % V7X_SUBLANES
            if sft == 0:
                window = conv_hist[base:base + CONV_ROWS, :]
            else:
                window = conv_shift[sft - 1, base:base + CONV_ROWS, :]
            acc = acc + conv_w_ref[k:k + 1, :] * window
        yb_rows = _layer_norm(acc, cn_g_ref[...], cn_b_ref[...])
        yb_parts.append((yb_rows * _sigmoid(yb_rows)).astype(BF16))
    yb = jnp.concatenate(yb_parts, axis=0)
    conv_hist[0:HALO, :] = conv_hist[ts:ts + HALO, :]

    za = side_buf[:, S_POOL:S_POOL + BRANCH_WIDTH]
    pool_hist[HALO:HALO + ts, :] = za
    row = lax.broadcasted_iota(I32, (ts, 1), 0) + s * ts
    mixed = []
    for g, w in enumerate(POOL_WINDOWS):
        c0 = g * POOL_CH
        cur = za[:, c0:c0 + POOL_CH]
        acc = cur
        for j in range(1, w):
            acc = acc + pool_hist[HALO - j:HALO - j + ts, c0:c0 + POOL_CH]
        count = jnp.minimum(row + 1, w).astype(F32)
        pooled = (acc / count - cur).astype(BF16)
        mixed.append(_dot(pooled, pool_w_ref[g]))
    ya = (jnp.concatenate(mixed, axis=-1) * pool_scale_ref[...]).astype(BF16)
    pool_hist[0:HALO, :] = pool_hist[ts:ts + HALO, :]

    zu = side_buf[:, S_SGU_U:S_SGU_U + BRANCH_WIDTH]
    zv = side_buf[:, S_SGU_V:S_SGU_V + BRANCH_WIDTH]
    v = _layer_norm(zv, sn_g_ref[...], sn_b_ref[...]).astype(BF16)
    tri = (lax.broadcasted_iota(I32, (SGU_CHUNK, SGU_CHUNK), 0)
           >= lax.broadcasted_iota(I32, (SGU_CHUNK, SGU_CHUNK), 1))
    chunks = []
    for c in range(ts // SGU_CHUNK):
        heads = []
        for hd in range(SGU_HEADS):
            w_tri = jnp.where(tri, sgu_w_ref[hd], 0.0).astype(BF16)
            vv = v[c * SGU_CHUNK:(c + 1) * SGU_CHUNK, hd * 128:(hd + 1) * 128]
            heads.append(_dot(w_tri, vv) + sgu_bt_ref[:, hd:hd + 1])
        chunks.append(jnp.concatenate(heads, axis=-1))
    yc = (zu * jnp.concatenate(chunks, axis=0)).astype(BF16)

    merged = jnp.zeros((ts, D_MODEL), F32)
    for k, yk in enumerate((ya, yb, yc)):
        proj = _dot(yk, branch_w_ref[k]) + branch_b_ref[k:k + 1, :]
        g0 = S_GATE + k * D_MODEL
        merged = merged + _sigmoid(side_buf[:, g0:g0 + D_MODEL]) * proj
    x1 = x + _dot(merged.astype(BF16), w_out_ref[...])
    x1_ref[0] = x1

    h2 = _rms_norm(x1, moe_g_ref[...])
    h2_hi = h2.astype(BF16)
    h2b_ref[...] = h2_hi
    h2_lo = (h2 - h2_hi.astype(F32)).astype(BF16)
    rw = rw_t_ref[...]
    rw_hi = rw.astype(BF16)
    rw_lo = (rw - rw_hi.astype(F32)).astype(BF16)
    nt = (((1,), (1,)), ((), ()))
    logits = (lax.dot_general(rw_hi, h2_hi, nt, preferred_element_type=F32)
              + lax.dot_general(rw_hi, h2_lo, nt, preferred_element_type=F32)
              + lax.dot_general(rw_lo, h2_hi, nt, preferred_element_type=F32)) + rb_ref[...]
    e_iota = lax.broadcasted_iota(I32, (N_EXPERTS, ts), 0).astype(F32)
    vals = logits
    top_v, sels = [], []
    for _k in range(TOP_K):
        m = jnp.max(vals, axis=0, keepdims=True)
        idx = jnp.min(jnp.where(vals == m, e_iota, float(N_EXPERTS)), axis=0, keepdims=True)
        sel = e_iota == idx
        vals = jnp.where(sel, -jnp.inf, vals)
        top_v.append(m)
        sels.append(sel)
    exps = [jnp.exp(tv - top_v[0]) for tv in top_v]
    denom = exps[0] + exps[1] + exps[2] + exps[3]
    chosen = jnp.zeros((N_EXPERTS, ts), F32)
    for sel in sels:
        chosen = chosen + jnp.where(sel, 1.0, 0.0)
    chosen_b = chosen.astype(BF16)
    t_row = lax.broadcasted_iota(I32, (ts, ts), 0)
    t_col = lax.broadcasted_iota(I32, (ts, ts), 1)
    before = jnp.where(t_row // MOE_TS == t_col // MOE_TS, jnp.where(t_row < t_col, 1.0, 0.0), 0.0)
    prefix = _dot(chosen_b, before.astype(BF16))
    lower = jnp.where(lax.broadcasted_iota(I32, (N_EXPERTS, N_EXPERTS), 0)
                      > lax.broadcasted_iota(I32, (N_EXPERTS, N_EXPERTS), 1), 1.0, 0.0).astype(BF16)
    base_parts = []
    for j in range(ts // MOE_TS):
        lanes = slice(j * MOE_TS, (j + 1) * MOE_TS)
        count = jnp.sum(chosen[:, lanes], axis=1, keepdims=True)
        run_rows = jnp.floor((count + (RUN_ALIGN - 1)) * (1.0 / RUN_ALIGN)) * RUN_ALIGN
        run_start = _dot(lower, jnp.broadcast_to(run_rows, (N_EXPERTS, V7X_LANES)).astype(BF16))[:, 0:1]
        base_parts.append(prefix[:, lanes] + run_start)
        counts_ref[j] = jnp.broadcast_to(count, (N_EXPERTS, V7X_LANES))
    base = jnp.concatenate(base_parts, axis=1)
    zeros4 = jnp.zeros((V7X_SUBLANES - TOP_K, ts), F32)
    pos = [jnp.sum(jnp.where(sel, base, 0.0), axis=0, keepdims=True) for sel in sels]
    pos_ref[...] = jnp.concatenate(pos + [zeros4], axis=0).astype(I32)
    gate_ref[...] = jnp.concatenate([e / denom for e in exps] + [zeros4], axis=0)


def _const_spec(shape):
    nd = len(shape)
    return pl.BlockSpec(shape, lambda *_: (0,) * nd, pipeline_mode=pl.Buffered(1))


def _mixer_call(x, lw):
    B, S, D = x.shape
    ts = MIX_TS
    n_s = S // ts
    T = B * S
    tok_blk = lambda b, s: (0, b * n_s + s)
    consts = [lw["mix_norm"], lw["w_in"], lw["b_in"], lw["pool_w"], lw["pool_scale"], lw["conv_w"],
              lw["conv_b"], lw["conv_norm_g"], lw["conv_norm_b"], lw["sgu_norm_g"], lw["sgu_norm_b"],
              lw["sgu_w"], lw["sgu_bt"], lw["branch_w"], lw["branch_b"], lw["w_out"], lw["moe_norm"],
              lw["router_wt"], lw["router_b"]]
    in_specs = [pl.BlockSpec((1, ts, D), lambda b, s: (b, s, 0))] + [_const_spec(c.shape) for c in consts]
    out_shape = (
        jax.ShapeDtypeStruct((B, S, D), F32),
        jax.ShapeDtypeStruct((T, D), BF16),
        jax.ShapeDtypeStruct((V7X_SUBLANES, T), I32),
        jax.ShapeDtypeStruct((V7X_SUBLANES, T), F32),
        jax.ShapeDtypeStruct((T // MOE_TS, N_EXPERTS, V7X_LANES), F32),
    )
    out_specs = (
        pl.BlockSpec((1, ts, D), lambda b, s: (b, s, 0)),
        pl.BlockSpec((ts, D), lambda b, s: (b * n_s + s, 0)),
        pl.BlockSpec((V7X_SUBLANES, ts), tok_blk),
        pl.BlockSpec((V7X_SUBLANES, ts), tok_blk),
        pl.BlockSpec((ts // MOE_TS, N_EXPERTS, V7X_LANES), lambda b, s: (b * n_s + s, 0, 0)),
    )
    return pl.pallas_call(
        _mixer_kernel,
        grid=(B, n_s),
        in_specs=in_specs,
        out_specs=out_specs,
        out_shape=out_shape,
        scratch_shapes=[pltpu.VMEM((HALO + ts, BRANCH_WIDTH), F32),
                        pltpu.VMEM((HALO + ts, BRANCH_WIDTH), F32),
                        pltpu.VMEM((V7X_SUBLANES - 1, HALO + ts - V7X_SUBLANES, BRANCH_WIDTH), F32),
                        pltpu.VMEM((ts, SIDE_WIDTH), F32)],
        compiler_params=pltpu.CompilerParams(
            dimension_semantics=("arbitrary", "arbitrary"),
            vmem_limit_bytes=V7X_VMEM_BYTES - 8 * 1024 * 1024),
        name="mixer",
    )(x, *consts)


def _row_span(first_row, n_rows):
    return pl.ds(pl.multiple_of(first_row * ROW_SUB, V7X_SUBLANES),
                 pl.multiple_of(n_rows * ROW_SUB, V7X_SUBLANES))


def _dispatch_kernel(run_src, run_n, run_dst, tile_rows, pad_dst, pad_n, n_used_ref, h2b_ref, pos_ref,
                     xs_hbm, stage, zero_rows, run_sem, fill_sem, *, n_blocks):
    i = pl.program_id(0)
    n_tiles = pl.num_programs(0)
    slot = i % 2
    block_rows = MOE_BLOCK * ROW_SUB

    def run_copy(tile, e, buf, live=1):
        k = tile * N_EXPERTS + e
        n = run_n[k] * live
        return pltpu.make_async_copy(stage.at[buf, _row_span(run_src[k], n)],
                                     xs_hbm.at[_row_span(run_dst[k], n)], run_sem.at[buf])

    def wait_runs(tile, buf):
        n = tile_rows[tile]
        pltpu.make_async_copy(stage.at[buf, _row_span(0, n)], xs_hbm.at[_row_span(0, n)],
                              run_sem.at[buf]).wait()

    def pad_copy(e):
        return pltpu.make_async_copy(zero_rows.at[_row_span(0, pad_n[e])],
                                     xs_hbm.at[_row_span(pad_dst[e], pad_n[e])], fill_sem)

    def idle_block_copy(blk):
        return pltpu.make_async_copy(
            zero_rows, xs_hbm.at[pl.ds(pl.multiple_of(blk * block_rows, block_rows), block_rows)], fill_sem)

    @pl.when(i == 0)
    def _():
        zero_rows[...] = jnp.zeros_like(zero_rows)

        def start_idle(blk, c):
            idle_block_copy(blk).start()
            return c

        def wait_idle(blk, c):
            idle_block_copy(blk).wait()
            return c

        for e in range(N_EXPERTS):
            pad_copy(e).start()
        lax.fori_loop(n_used_ref[0], n_blocks, start_idle, 0)
        for e in range(N_EXPERTS):
            pad_copy(e).wait()
        lax.fori_loop(n_used_ref[0], n_blocks, wait_idle, 0)

    @pl.when(i >= 2)
    def _():
        wait_runs(i - 2, slot)

    prev_live = jnp.minimum(i, 1)
    for e in range(N_EXPERTS):
        run_copy(jnp.maximum(i - 1, 0), e, 1 - slot, prev_live).start()

    h2b = h2b_ref[...]
    for a0 in range(0, N_STAGE, STAGE_CHUNK):
        a_iota = lax.broadcasted_iota(I32, (STAGE_CHUNK, MOE_TS), 0) + a0
        onehot = jnp.zeros((STAGE_CHUNK, MOE_TS), F32)
        for k in range(TOP_K):
            onehot = jnp.where(a_iota == pos_ref[k:k + 1, :], 1.0, onehot)
        _store_rows(stage.at[slot], a0, _pack_rows(_dot(onehot.astype(BF16), h2b)))

    @pl.when(i == n_tiles - 1)
    def _():
        for e in range(N_EXPERTS):
            run_copy(i, e, slot).start()

        @pl.when(i >= 1)
        def _():
            wait_runs(i - 1, 1 - slot)
        wait_runs(i, slot)


def _dispatch_call(h2b, pos8, tables, n_blocks):
    T, D = h2b.shape
    n_tiles = T // MOE_TS
    grid_spec = pltpu.PrefetchScalarGridSpec(
        num_scalar_prefetch=7,
        grid=(n_tiles,),
        in_specs=[pl.BlockSpec((MOE_TS, D), lambda i, *_: (i, 0)),
                  pl.BlockSpec((V7X_SUBLANES, MOE_TS), lambda i, *_: (0, i))],
        out_specs=pl.BlockSpec(memory_space=pl.ANY),
        scratch_shapes=[pltpu.VMEM((2, N_STAGE * ROW_SUB, V7X_LANES), U32),
                        pltpu.VMEM((MOE_BLOCK * ROW_SUB, V7X_LANES), U32),
                        pltpu.SemaphoreType.DMA((2,)),
                        pltpu.SemaphoreType.DMA(())],
    )
    return pl.pallas_call(
        functools.partial(_dispatch_kernel, n_blocks=n_blocks),
        grid_spec=grid_spec,
        out_shape=jax.ShapeDtypeStruct((n_blocks * MOE_BLOCK * ROW_SUB, V7X_LANES), U32),
        compiler_params=pltpu.CompilerParams(
            dimension_semantics=("arbitrary",),
            vmem_limit_bytes=V7X_VMEM_BYTES - 16 * 1024 * 1024),
        name="dispatch",
    )(tables["run_src"], tables["run_n"], tables["run_dst"], tables["tile_rows"], tables["pad_dst"],
      tables["pad_n"], tables["n_used"], h2b, pos8)


def _moe_kernel(be_ref, group_end_ref, n_used_ref, xs_ref, wup_hbm, bup_ref, wdn_hbm, bdn_ref,
                ys_ref, wup_f32, wdn_f32, wup_bf, wdn_bf, group_count, wsem, *, layer):
    blk = pl.program_id(0)
    n_used = n_used_ref[0]
    active = blk < n_used
    prev = jnp.maximum(blk - 1, 0)
    expert = be_ref[blk]
    new_expert = (blk == 0) | (expert != be_ref[prev])

    def weight_copies(e, buf):
        return (pltpu.make_async_copy(wup_hbm.at[layer, e], wup_f32.at[buf], wsem.at[buf, 0]),
                pltpu.make_async_copy(wdn_hbm.at[layer, e], wdn_f32.at[buf], wsem.at[buf, 1]))

    @pl.when(blk == 0)
    def _():
        group_count[0] = 0
        for c in weight_copies(expert, 0):
            c.start()

    @pl.when(active & new_expert)
    def _():
        par = group_count[0] % 2
        group_count[0] = group_count[0] + 1
        next_blk = group_end_ref[expert]

        @pl.when(next_blk < n_used)
        def _():
            for c in weight_copies(be_ref[next_blk], 1 - par):
                c.start()

        for c in weight_copies(expert, par):
            c.wait()
        wup_bf[...] = wup_f32[par].astype(BF16)
        wdn_bf[...] = wdn_f32[par].astype(BF16)

    @pl.when(active)
    def _():
        xs = _unpack_rows(_load_rows(xs_ref, 0, MOE_BLOCK))
        gu = _dot(xs, wup_bf[...]) + bup_ref[0, 0]
        x_glu = jnp.minimum(gu[:, :D_FF], SWIGLU_LIMIT)
        x_lin = jnp.clip(gu[:, D_FF:], -SWIGLU_LIMIT, SWIGLU_LIMIT)
        act = x_glu * _sigmoid(SWIGLU_ALPHA * x_glu) * (x_lin + 1.0)
        y = _dot(act.astype(BF16), wdn_bf[...]) + bdn_ref[0, 0]
        _store_rows(ys_ref, 0, _pack_rows(y.astype(BF16).astype(F32)))

    @pl.when(jnp.logical_not(active))
    def _():
        ys_ref[...] = jnp.zeros_like(ys_ref)


def _moe_call(layer, xs, tables, w_up, b_up, w_down, b_down):
    n_blocks = tables["block_e"].shape[0]
    block_rows = MOE_BLOCK * ROW_SUB
    b_idx = lambda b, be, *_: (layer, be[b], 0, 0)
    grid_spec = pltpu.PrefetchScalarGridSpec(
        num_scalar_prefetch=3,
        grid=(n_blocks,),
        in_specs=[
            pl.BlockSpec((block_rows, V7X_LANES), lambda b, *_: (b, 0)),
            pl.BlockSpec(memory_space=pl.ANY),
            pl.BlockSpec((1, 1, 1, 2 * D_FF), b_idx),
            pl.BlockSpec(memory_space=pl.ANY),
            pl.BlockSpec((1, 1, 1, D_MODEL), b_idx),
        ],
        out_specs=pl.BlockSpec((block_rows, V7X_LANES), lambda b, *_: (b, 0)),
        scratch_shapes=[pltpu.VMEM((2, D_MODEL, 2 * D_FF), F32), pltpu.VMEM((2, D_FF, D_MODEL), F32),
                        pltpu.VMEM((D_MODEL, 2 * D_FF), BF16), pltpu.VMEM((D_FF, D_MODEL), BF16),
                        pltpu.SMEM((1,), I32), pltpu.SemaphoreType.DMA((2, 2))],
    )
    return pl.pallas_call(
        functools.partial(_moe_kernel, layer=layer),
        grid_spec=grid_spec,
        out_shape=jax.ShapeDtypeStruct(xs.shape, U32),
        compiler_params=pltpu.CompilerParams(
            dimension_semantics=("arbitrary",),
            vmem_limit_bytes=V7X_VMEM_BYTES - 8 * 1024 * 1024),
        name="moe",
    )(tables["block_e"], tables["group_end"], tables["n_used"], xs, w_up, b_up, w_down, b_down)


def _combine_kernel(run_src, run_n, run_dst, tile_rows, x1_ref, pos_ref, gate_ref, p_ref, ple_g_ref,
                    gate_w_ref, proj_w_ref, fin_g_ref, ys_hbm, out_ref, stage, run_sem, *, last):
    i = pl.program_id(0)
    n_tiles = pl.num_programs(0)
    slot = i % 2

    def run_copy(tile, e, buf, live=1):
        k = tile * N_EXPERTS + e
        n = run_n[k] * live
        return pltpu.make_async_copy(ys_hbm.at[_row_span(run_dst[k], n)],
                                     stage.at[buf, _row_span(run_src[k], n)], run_sem.at[buf])

    @pl.when(i == 0)
    def _():
        stage[...] = jnp.zeros_like(stage)
        for e in range(N_EXPERTS):
            run_copy(0, e, 0).start()

    n_rows = tile_rows[i]
    pltpu.make_async_copy(ys_hbm.at[_row_span(0, n_rows)], stage.at[slot, _row_span(0, n_rows)],
                          run_sem.at[slot]).wait()

    next_live = jnp.where(i + 1 < n_tiles, 1, 0)
    for e in range(N_EXPERTS):
        run_copy(jnp.minimum(i + 1, n_tiles - 1), e, 1 - slot, next_live).start()

    x2 = x1_ref[...]
    for a0 in range(0, N_STAGE, STAGE_CHUNK):
        a_iota = lax.broadcasted_iota(I32, (MOE_TS, STAGE_CHUNK), 1) + a0
        weights = jnp.zeros((MOE_TS, STAGE_CHUNK), F32)
        for k in range(TOP_K):
            weights = jnp.where(a_iota == pos_ref[:, k:k + 1], gate_ref[:, k:k + 1], weights)
        y_sorted = _unpack_rows(_load_rows(stage.at[slot], a0, STAGE_CHUNK))
        x2 = x2 + _dot(weights.astype(BF16), y_sorted)
    h3 = _rms_norm(x2, ple_g_ref[...]).astype(BF16)
    g = _sigmoid(_dot(h3, gate_w_ref[...]))
    pp = _dot(p_ref[0, 0].astype(BF16), proj_w_ref[...])
    x3 = x2 + g * pp
    if last:
        x3 = _rms_norm(x3, fin_g_ref[...])
    out_ref[...] = x3


def _combine_call(layer, x1, ys, pos_tm, gate_tm, p, tables, lw, fin_g, last):
    T, D = x1.shape
    n_tiles = T // MOE_TS
    n_seq = p.shape[2] // MOE_TS
    consts = [lw["ple_norm"], lw["ple_gate_w"], lw["ple_proj_w"], fin_g]
    grid_spec = pltpu.PrefetchScalarGridSpec(
        num_scalar_prefetch=4,
        grid=(n_tiles,),
        in_specs=[pl.BlockSpec((MOE_TS, D), lambda i, *_: (i, 0)),
                  pl.BlockSpec((MOE_TS, V7X_SUBLANES), lambda i, *_: (i, 0)),
                  pl.BlockSpec((MOE_TS, V7X_SUBLANES), lambda i, *_: (i, 0)),
                  pl.BlockSpec((1, 1, MOE_TS, PLE_DIM), lambda i, *_: (layer, i // n_seq, i % n_seq, 0))]
        + [_const_spec(c.shape) for c in consts] + [pl.BlockSpec(memory_space=pl.ANY)],
        out_specs=pl.BlockSpec((MOE_TS, D), lambda i, *_: (i, 0)),
        scratch_shapes=[pltpu.VMEM((2, N_STAGE * ROW_SUB, V7X_LANES), U32),
                        pltpu.SemaphoreType.DMA((2,))],
    )
    return pl.pallas_call(
        functools.partial(_combine_kernel, last=last),
        grid_spec=grid_spec,
        out_shape=jax.ShapeDtypeStruct((T, D), F32),
        compiler_params=pltpu.CompilerParams(
            dimension_semantics=("arbitrary",),
            vmem_limit_bytes=V7X_VMEM_BYTES - 16 * 1024 * 1024),
        name="combine",
    )(tables["run_src"], tables["run_n"], tables["run_dst"], tables["tile_rows"], x1, pos_tm, gate_tm, p,
      *consts, ys)


def _routing_tables(tile_counts, n_blocks):
    c = tile_counts[:, :, 0].astype(I32)
    c = (c + RUN_ALIGN - 1) // RUN_ALIGN * RUN_ALIGN
    counts = jnp.sum(c, axis=0)
    padded = (counts + MOE_BLOCK - 1) // MOE_BLOCK * MOE_BLOCK
    pad_end = jnp.cumsum(padded)
    pad_start = pad_end - padded
    run_dst = pad_start[None, :] + jnp.cumsum(c, axis=0) - c
    run_src = jnp.cumsum(c, axis=1) - c
    n_used = pad_end[-1] // MOE_BLOCK
    blk0 = jnp.arange(n_blocks, dtype=I32) * MOE_BLOCK
    be = jnp.minimum(jnp.sum(blk0[:, None] >= pad_end[None, :], axis=1), N_EXPERTS - 1)
    be = be[jnp.minimum(jnp.arange(n_blocks), n_used - 1)]
    return {
        "group_end": (pad_end // MOE_BLOCK).astype(I32),
        "run_src": run_src.reshape(-1).astype(I32), "run_n": c.reshape(-1),
        "run_dst": run_dst.reshape(-1).astype(I32), "tile_rows": jnp.sum(c, axis=1).astype(I32),
        "pad_dst": (pad_start + counts).astype(I32), "pad_n": (padded - counts).astype(I32),
        "n_used": n_used.reshape(1).astype(I32), "block_e": be.astype(I32),
    }


def kernel(x, p, mix_norm, w_in, b_in, pool_w, pool_scale, conv_w, conv_b, conv_norm_g, conv_norm_b,
           sgu_norm_g, sgu_norm_b, sgu_w, sgu_b, branch_w, branch_b, w_out, moe_norm, router_w,
           router_b, expert_w_up, expert_b_up, expert_w_down, expert_b_down, ple_norm, ple_gate_w,
           ple_proj_w, final_norm):
    B, S, D = x.shape
    T = B * S
    depth = w_in.shape[0]
    assert D == D_MODEL and S % MIX_TS == 0 and w_in.shape[2] == IN_WIDTH
    max_rows = T * TOP_K + (T // MOE_TS) * N_EXPERTS * (RUN_ALIGN - 1)
    n_blocks = -(-max_rows // MOE_BLOCK) + N_EXPERTS
    row = lambda a: a.reshape(1, -1)
    b_up4 = expert_b_up[:, :, None, :]
    b_down4 = expert_b_down[:, :, None, :]
    for i in range(depth):
        lw = {
            "mix_norm": row(mix_norm[i]), "w_in": w_in[i].astype(BF16), "b_in": row(b_in[i]),
            "pool_w": pool_w[i].astype(BF16), "pool_scale": row(pool_scale[i]),
            "conv_w": conv_w[i], "conv_b": row(conv_b[i]),
            "conv_norm_g": row(conv_norm_g[i]), "conv_norm_b": row(conv_norm_b[i]),
            "sgu_norm_g": row(sgu_norm_g[i]), "sgu_norm_b": row(sgu_norm_b[i]),
            "sgu_w": sgu_w[i], "sgu_bt": sgu_b[i].T,
            "branch_w": branch_w[i].astype(BF16), "branch_b": branch_b[i],
            "w_out": w_out[i].astype(BF16), "moe_norm": row(moe_norm[i]),
            "router_wt": router_w[i].T, "router_b": router_b[i].reshape(-1, 1),
            "ple_norm": row(ple_norm[i]), "ple_gate_w": ple_gate_w[i].astype(BF16),
            "ple_proj_w": ple_proj_w[i].astype(BF16),
        }
        x1, h2b, pos8, gate8, tile_counts = _mixer_call(x, lw)
        tables = _routing_tables(tile_counts, n_blocks)
        xs = _dispatch_call(h2b, pos8, tables, n_blocks)
        ys = _moe_call(i, xs, tables, expert_w_up, b_up4, expert_w_down, b_down4)
        x = _combine_call(i, x1.reshape(T, D), ys, pos8.T, gate8.T, p, tables, lw,
                          row(final_norm), last=(i == depth - 1)).reshape(B, S, D)
    return x
```

```python
import functools

import jax
import jax.numpy as jnp
from jax import lax
from jax.experimental import pallas as pl
from jax.experimental.pallas import tpu as pltpu

F32 = jnp.float32
BF16 = jnp.bfloat16
I32 = jnp.int32
U32 = jnp.uint32

D_MODEL = 1024
POOL_WINDOWS = (2, 4, 8, 16)
POOL_CH = 128
BRANCH_WIDTH = 512
CONV_K = 31
SGU_CHUNK = 128
SGU_HEADS = 4
N_BRANCH = 3
N_EXPERTS = 32
TOP_K = 4
D_FF = 1024
SWIGLU_LIMIT = 7.0
SWIGLU_ALPHA = 1.702
MOE_BLOCK = 256
PLE_DIM = 256
EPS = 1e-6

V7X_SUBLANES = 8
V7X_LANES = 128
ROW_WORDS = D_MODEL // 2
ROW_SUB = ROW_WORDS // V7X_LANES
RUN_ALIGN = V7X_SUBLANES // ROW_SUB
V7X_VMEM_BYTES = 64 * 1024 * 1024

MIX_TS = 512
HALO = 32
CONV_ROWS = 32
SIDE_CHUNK = 256
MOE_TS = 256
N_STAGE = TOP_K * MOE_TS + N_EXPERTS * (RUN_ALIGN - 1)
STAGE_CHUNK = N_STAGE // 3
assert MIX_TS % MOE_TS == 0 and STAGE_CHUNK * 3 == N_STAGE and STAGE_CHUNK % V7X_SUBLANES == 0

C_POOL = 0
C_CONV = C_POOL + BRANCH_WIDTH
C_SGU_U = C_CONV + 2 * BRANCH_WIDTH
C_SGU_V = C_SGU_U + BRANCH_WIDTH
C_GATE = C_SGU_V + BRANCH_WIDTH
IN_WIDTH = C_GATE + N_BRANCH * D_MODEL
S_POOL = 0
S_SGU_U = S_POOL + BRANCH_WIDTH
S_SGU_V = S_SGU_U + BRANCH_WIDTH
S_GATE = S_SGU_V + BRANCH_WIDTH
SIDE_WIDTH = S_GATE + N_BRANCH * D_MODEL
assert S_SGU_U % SIDE_CHUNK == 0 and SIDE_WIDTH % SIDE_CHUNK == 0


def _rms_norm(x, g):
    return x * lax.rsqrt(jnp.mean(x * x, axis=-1, keepdims=True) + EPS) * g


def _layer_norm(x, g, b):
    mu = jnp.mean(x, axis=-1, keepdims=True)
    xc = x - mu
    var = jnp.mean(xc * xc, axis=-1, keepdims=True)
    return xc * lax.rsqrt(var + EPS) * g + b


def _sigmoid(x):
    return 0.5 * jnp.tanh(0.5 * x) + 0.5


def _dot(a, b):
    return jnp.dot(a, b, preferred_element_type=F32)


def _pack_rows(v):
    bits = lax.bitcast_convert_type(v, U32)
    return (bits[:, :ROW_WORDS] >> 16) | (bits[:, ROW_WORDS:] & jnp.uint32(0xFFFF0000))


def _unpack_rows(w):
    low = lax.bitcast_convert_type(w << 16, F32)
    high = lax.bitcast_convert_type(w & jnp.uint32(0xFFFF0000), F32)
    return jnp.concatenate([low, high], axis=-1).astype(BF16)


def _store_rows(ref_2d, first_row, words):
    n = words.shape[0]
    for j in range(ROW_SUB):
        ref_2d[pl.ds(first_row * ROW_SUB + j, n, stride=ROW_SUB), :] = words[:, j * V7X_LANES:(j + 1) * V7X_LANES]


def _load_rows(ref_2d, first_row, n):
    return jnp.concatenate(
        [ref_2d[pl.ds(first_row * ROW_SUB + j, n, stride=ROW_SUB), :] for j in range(ROW_SUB)], axis=-1)


def _mixer_kernel(x_ref, mix_g_ref, w_in_ref, b_in_ref, pool_w_ref, pool_scale_ref, conv_w_ref,
                  conv_b_ref, cn_g_ref, cn_b_ref, sn_g_ref, sn_b_ref, sgu_w_ref, sgu_bt_ref,
                  branch_w_ref, branch_b_ref, w_out_ref, moe_g_ref, rw_t_ref, rb_ref,
                  x1_ref, h2b_ref, pos_ref, gate_ref, counts_ref,
                  pool_hist, conv_hist, conv_shift, side_buf, x1_prev, *, n_seq):
    i = pl.program_id(0)
    n_tiles = pl.num_programs(0) - 1
    route_refs = (moe_g_ref, rw_t_ref, rb_ref, h2b_ref, pos_ref, gate_ref, counts_ref)

    @pl.when(i == 0)
    def _():
        x1_prev[...] = jnp.zeros_like(x1_prev)

    @pl.when(i % n_seq == 0)
    def _():
        pool_hist[0:HALO, :] = jnp.zeros((HALO, BRANCH_WIDTH), F32)
        conv_hist[0:HALO, :] = jnp.zeros((HALO, BRANCH_WIDTH), F32)

    @pl.when(i < n_tiles)
    def _():
        _mix_tile(i % n_seq, x_ref, mix_g_ref, w_in_ref, b_in_ref, pool_w_ref, pool_scale_ref, conv_w_ref,
                  conv_b_ref, cn_g_ref, cn_b_ref, sn_g_ref, sn_b_ref, sgu_w_ref, sgu_bt_ref, branch_w_ref,
                  branch_b_ref, w_out_ref, x1_ref, pool_hist, conv_hist, conv_shift, side_buf, x1_prev,
                  functools.partial(_route_tile, x1_prev, *route_refs))

    @pl.when(i == n_tiles)
    def _():
        _route_tile(x1_prev, *route_refs)


def _mix_tile(s, x_ref, mix_g_ref, w_in_ref, b_in_ref, pool_w_ref, pool_scale_ref, conv_w_ref,
              conv_b_ref, cn_g_ref, cn_b_ref, sn_g_ref, sn_b_ref, sgu_w_ref, sgu_bt_ref, branch_w_ref,
              branch_b_ref, w_out_ref, x1_ref, pool_hist, conv_hist, conv_shift, side_buf, x1_prev,
              route_previous):
    ts = MIX_TS
    x = x_ref[0]
    h = _rms_norm(x, mix_g_ref[...]).astype(BF16)

    def in_proj(c0, width):
        return _dot(h, w_in_ref[:, c0:c0 + width]) + b_in_ref[:, c0:c0 + width]

    zb = in_proj(C_CONV, 2 * BRANCH_WIDTH)
    conv_hist[HALO:HALO + ts, :] = zb[:, :BRANCH_WIDTH] * _sigmoid(zb[:, BRANCH_WIDTH:])
    n_shift_rows = HALO + ts - V7X_SUBLANES
    for sft in range(1, V7X_SUBLANES):
        conv_shift[sft - 1] = conv_hist[sft:sft + n_shift_rows, :]
    yb_parts = []
    n_conv_blocks = ts // CONV_ROWS
    n_side_chunks = SIDE_WIDTH // SIDE_CHUNK
    for bi in range(n_conv_blocks):
        r0 = bi * CONV_ROWS
        for ci in range(bi * n_side_chunks // n_conv_blocks, (bi + 1) * n_side_chunks // n_conv_blocks):
            d0 = ci * SIDE_CHUNK
            c0 = d0 if d0 < S_SGU_U else d0 + (C_SGU_U - S_SGU_U)
            side_buf[:, d0:d0 + SIDE_CHUNK] = in_proj(c0, SIDE_CHUNK)
        acc = jnp.zeros((CONV_ROWS, BRANCH_WIDTH), F32) + conv_b_ref[...]
        for k in range(CONV_K):
            off = HALO - (CONV_K - 1) + k + r0
            base, sft = off - off % V7X_SUBLANES, off % V7X_SUBLANES
            if sft == 0:
                window = conv_hist[base:base + CONV_ROWS, :]
            else:
                window = conv_shift[sft - 1, base:base + CONV_ROWS, :]
            acc = acc + conv_w_ref[k:k + 1, :] * window
        yb_rows = _layer_norm(acc, cn_g_ref[...], cn_b_ref[...])
        yb_parts.append((yb_rows * _sigmoid(yb_rows)).astype(BF16))
    yb = jnp.concatenate(yb_parts, axis=0)
    conv_hist[0:HALO, :] = conv_hist[ts:ts + HALO, :]

    za = side_buf[:, S_POOL:S_POOL + BRANCH_WIDTH]
    pool_hist[HALO:HALO + ts, :] = za
    row = lax.broadcasted_iota(I32, (ts, 1), 0) + s * ts
    mixed = []
    for g, w in enumerate(POOL_WINDOWS):
        c0 = g * POOL_CH
        cur = za[:, c0:c0 + POOL_CH]
        acc = cur
        for j in range(1, w):
            acc = acc + pool_hist[HALO - j:HALO - j + ts, c0:c0 + POOL_CH]
        count = jnp.minimum(row + 1, w).astype(F32)
        pooled = (acc / count - cur).astype(BF16)
        mixed.append(_dot(pooled, pool_w_ref[g]))
    ya = (jnp.concatenate(mixed, axis=-1) * pool_scale_ref[...]).astype(BF16)
    pool_hist[0:HALO, :] = pool_hist[ts:ts + HALO, :]

    zu = side_buf[:, S_SGU_U:S_SGU_U + BRANCH_WIDTH]
    zv = side_buf[:, S_SGU_V:S_SGU_V + BRANCH_WIDTH]
    v = _layer_norm(zv, sn_g_ref[...], sn_b_ref[...]).astype(BF16)
    tri = (lax.broadcasted_iota(I32, (SGU_CHUNK, SGU_CHUNK), 0)
           >= lax.broadcasted_iota(I32, (SGU_CHUNK, SGU_CHUNK), 1))
    chunks = []
    for c in range(ts // SGU_CHUNK):
        heads = []
        for hd in range(SGU_HEADS):
            w_tri = jnp.where(tri, sgu_w_ref[hd], 0.0).astype(BF16)
            vv = v[c * SGU_CHUNK:(c + 1) * SGU_CHUNK, hd * 128:(hd + 1) * 128]
            heads.append(_dot(w_tri, vv) + sgu_bt_ref[:, hd:hd + 1])
        chunks.append(jnp.concatenate(heads, axis=-1))
    yc = (zu * jnp.concatenate(chunks, axis=0)).astype(BF16)

    route_previous()

    merged = jnp.zeros((ts, D_MODEL), F32)
    for k, yk in enumerate((ya, yb, yc)):
        proj = _dot(yk, branch_w_ref[k]) + branch_b_ref[k:k + 1, :]
        g0 = S_GATE + k * D_MODEL
        merged = merged + _sigmoid(side_buf[:, g0:g0 + D_MODEL]) * proj
    x1 = x + _dot(merged.astype(BF16), w_out_ref[...])
    x1_ref[0] = x1
    x1_prev[...] = x1


def _route_tile(x1_ref, moe_g_ref, rw_t_ref, rb_ref, h2b_ref, pos_ref, gate_ref, counts_ref):
    ts = MIX_TS
    h2 = _rms_norm(x1_ref[...], moe_g_ref[...])
    h2_hi = h2.astype(BF16)
    h2b_ref[...] = h2_hi
    h2_lo = (h2 - h2_hi.astype(F32)).astype(BF16)
    rw = rw_t_ref[...]
    rw_hi = rw.astype(BF16)
    rw_lo = (rw - rw_hi.astype(F32)).astype(BF16)
    nt = (((1,), (1,)), ((), ()))
    logits = (lax.dot_general(rw_hi, h2_hi, nt, preferred_element_type=F32)
              + lax.dot_general(rw_hi, h2_lo, nt, preferred_element_type=F32)
              + lax.dot_general(rw_lo, h2_hi, nt, preferred_element_type=F32)) + rb_ref[...]
    e_iota = lax.broadcasted_iota(I32, (N_EXPERTS, ts), 0).astype(F32)
    vals = logits
    top_v, sels = [], []
    for _k in range(TOP_K):
        m = jnp.max(vals, axis=0, keepdims=True)
        idx = jnp.min(jnp.where(vals == m, e_iota, float(N_EXPERTS)), axis=0, keepdims=True)
        sel = e_iota == idx
        vals = jnp.where(sel, -jnp.inf, vals)
        top_v.append(m)
        sels.append(sel)
    exps = [jnp.exp(tv - top_v[0]) for tv in top_v]
    denom = exps[0] + exps[1] + exps[2] + exps[3]
    chosen = jnp.zeros((N_EXPERTS, ts), F32)
    for sel in sels:
        chosen = chosen + jnp.where(sel, 1.0, 0.0)
    chosen_b = chosen.astype(BF16)
    t_row = lax.broadcasted_iota(I32, (ts, ts), 0)
    t_col = lax.broadcasted_iota(I32, (ts, ts), 1)
    before = jnp.where(t_row // MOE_TS == t_col // MOE_TS, jnp.where(t_row < t_col, 1.0, 0.0), 0.0)
    prefix = _dot(chosen_b, before.astype(BF16))
    lower = jnp.where(lax.broadcasted_iota(I32, (N_EXPERTS, N_EXPERTS), 0)
                      > lax.broadcasted_iota(I32, (N_EXPERTS, N_EXPERTS), 1), 1.0, 0.0).astype(BF16)
    base_parts = []
    for j in range(ts // MOE_TS):
        lanes = slice(j * MOE_TS, (j + 1) * MOE_TS)
        count = jnp.sum(chosen[:, lanes], axis=1, keepdims=True)
        run_rows = jnp.floor((count + (RUN_ALIGN - 1)) * (1.0 / RUN_ALIGN)) * RUN_ALIGN
        run_start = _dot(lower, jnp.broadcast_to(run_rows, (N_EXPERTS, V7X_LANES)).astype(BF16))[:, 0:1]
        base_parts.append(prefix[:, lanes] + run_start)
        counts_ref[j] = jnp.broadcast_to(count, (N_EXPERTS, V7X_LANES))
    base = jnp.concatenate(base_parts, axis=1)
    zeros4 = jnp.zeros((V7X_SUBLANES - TOP_K, ts), F32)
    pos = [jnp.sum(jnp.where(sel, base, 0.0), axis=0, keepdims=True) for sel in sels]
    pos_ref[...] = jnp.concatenate(pos + [zeros4], axis=0).astype(I32)
    gate_ref[...] = jnp.concatenate([e / denom for e in exps] + [zeros4], axis=0)


def _const_spec(shape):
    nd = len(shape)
    return pl.BlockSpec(shape, lambda *_: (0,) * nd, pipeline_mode=pl.Buffered(1))


def _mixer_call(x, lw):
    B, S, D = x.shape
    ts = MIX_TS
    n_s = S // ts
    T = B * S
    n_tiles = B * n_s
    mixed = lambda i: jnp.minimum(i, n_tiles - 1)
    routed = lambda i: jnp.maximum(i - 1, 0)
    consts = [lw["mix_norm"], lw["w_in"], lw["b_in"], lw["pool_w"], lw["pool_scale"], lw["conv_w"],
              lw["conv_b"], lw["conv_norm_g"], lw["conv_norm_b"], lw["sgu_norm_g"], lw["sgu_norm_b"],
              lw["sgu_w"], lw["sgu_bt"], lw["branch_w"], lw["branch_b"], lw["w_out"], lw["moe_norm"],
              lw["router_wt"], lw["router_b"]]
    x_spec = pl.BlockSpec((1, ts, D), lambda i: (mixed(i) // n_s, mixed(i) % n_s, 0))
    in_specs = [x_spec] + [_const_spec(c.shape) for c in consts]
    out_shape = (
        jax.ShapeDtypeStruct((B, S, D), F32),
        jax.ShapeDtypeStruct((T, D), BF16),
        jax.ShapeDtypeStruct((V7X_SUBLANES, T), I32),
        jax.ShapeDtypeStruct((V7X_SUBLANES, T), F32),
        jax.ShapeDtypeStruct((T // MOE_TS, N_EXPERTS, V7X_LANES), F32),
    )
    out_specs = (
        x_spec,
        pl.BlockSpec((ts, D), lambda i: (routed(i), 0)),
        pl.BlockSpec((V7X_SUBLANES, ts), lambda i: (0, routed(i))),
        pl.BlockSpec((V7X_SUBLANES, ts), lambda i: (0, routed(i))),
        pl.BlockSpec((ts // MOE_TS, N_EXPERTS, V7X_LANES), lambda i: (routed(i), 0, 0)),
    )
    return pl.pallas_call(
        functools.partial(_mixer_kernel, n_seq=n_s),
        grid=(n_tiles + 1,),
        in_specs=in_specs,
        out_specs=out_specs,
        out_shape=out_shape,
        scratch_shapes=[pltpu.VMEM((HALO + ts, BRANCH_WIDTH), F32),
                        pltpu.VMEM((HALO + ts, BRANCH_WIDTH), F32),
                        pltpu.VMEM((V7X_SUBLANES - 1, HALO + ts - V7X_SUBLANES, BRANCH_WIDTH), F32),
                        pltpu.VMEM((ts, SIDE_WIDTH), F32),
                        pltpu.VMEM((ts, D), F32)],
        compiler_params=pltpu.CompilerParams(
            dimension_semantics=("arbitrary",),
            vmem_limit_bytes=V7X_VMEM_BYTES - 8 * 1024 * 1024),
        name="mixer",
    )(x, *consts)


def _row_span(first_row, n_rows):
    return pl.ds(pl.multiple_of(first_row * ROW_SUB, V7X_SUBLANES),
                 pl.multiple_of(n_rows * ROW_SUB, V7X_SUBLANES))


def _dispatch_kernel(run_src, run_n, run_dst, tile_rows, pad_dst, pad_n, n_used_ref, h2b_ref, pos_ref,
                     xs_hbm, stage, zero_rows, run_sem, fill_sem, *, n_blocks):
    i = pl.program_id(0)
    n_tiles = pl.num_programs(0)
    slot = i % 2
    block_rows = MOE_BLOCK * ROW_SUB

    def run_copy(tile, e, buf, live=1):
        k = tile * N_EXPERTS + e
        n = run_n[k] * live
        return pltpu.make_async_copy(stage.at[buf, _row_span(run_src[k], n)],
                                     xs_hbm.at[_row_span(run_dst[k], n)], run_sem.at[buf])

    def wait_runs(tile, buf):
        n = tile_rows[tile]
        pltpu.make_async_copy(stage.at[buf, _row_span(0, n)], xs_hbm.at[_row_span(0, n)],
                              run_sem.at[buf]).wait()

    def pad_copy(e):
        return pltpu.make_async_copy(zero_rows.at[_row_span(0, pad_n[e])],
                                     xs_hbm.at[_row_span(pad_dst[e], pad_n[e])], fill_sem)

    def idle_block_copy(blk):
        return pltpu.make_async_copy(
            zero_rows, xs_hbm.at[pl.ds(pl.multiple_of(blk * block_rows, block_rows), block_rows)], fill_sem)

    @pl.when(i == 0)
    def _():
        zero_rows[...] = jnp.zeros_like(zero_rows)

        def start_idle(blk, c):
            idle_block_copy(blk).start()
            return c

        def wait_idle(blk, c):
            idle_block_copy(blk).wait()
            return c

        for e in range(N_EXPERTS):
            pad_copy(e).start()
        lax.fori_loop(n_used_ref[0], n_blocks, start_idle, 0)
        for e in range(N_EXPERTS):
            pad_copy(e).wait()
        lax.fori_loop(n_used_ref[0], n_blocks, wait_idle, 0)

    @pl.when(i >= 2)
    def _():
        wait_runs(i - 2, slot)

    prev_live = jnp.minimum(i, 1)
    for e in range(N_EXPERTS):
        run_copy(jnp.maximum(i - 1, 0), e, 1 - slot, prev_live).start()

    h2b = h2b_ref[...]
    for a0 in range(0, N_STAGE, STAGE_CHUNK):
        a_iota = lax.broadcasted_iota(I32, (STAGE_CHUNK, MOE_TS), 0) + a0
        onehot = jnp.zeros((STAGE_CHUNK, MOE_TS), F32)
        for k in range(TOP_K):
            onehot = jnp.where(a_iota == pos_ref[k:k + 1, :], 1.0, onehot)
        _store_rows(stage.at[slot], a0, _pack_rows(_dot(onehot.astype(BF16), h2b)))

    @pl.when(i == n_tiles - 1)
    def _():
        for e in range(N_EXPERTS):
            run_copy(i, e, slot).start()

        @pl.when(i >= 1)
        def _():
            wait_runs(i - 1, 1 - slot)
        wait_runs(i, slot)


def _dispatch_call(h2b, pos8, tables, n_blocks):
    T, D = h2b.shape
    n_tiles = T // MOE_TS
    grid_spec = pltpu.PrefetchScalarGridSpec(
        num_scalar_prefetch=7,
        grid=(n_tiles,),
        in_specs=[pl.BlockSpec((MOE_TS, D), lambda i, *_: (i, 0)),
                  pl.BlockSpec((V7X_SUBLANES, MOE_TS), lambda i, *_: (0, i))],
        out_specs=pl.BlockSpec(memory_space=pl.ANY),
        scratch_shapes=[pltpu.VMEM((2, N_STAGE * ROW_SUB, V7X_LANES), U32),
                        pltpu.VMEM((MOE_BLOCK * ROW_SUB, V7X_LANES), U32),
                        pltpu.SemaphoreType.DMA((2,)),
                        pltpu.SemaphoreType.DMA(())],
    )
    return pl.pallas_call(
        functools.partial(_dispatch_kernel, n_blocks=n_blocks),
        grid_spec=grid_spec,
        out_shape=jax.ShapeDtypeStruct((n_blocks * MOE_BLOCK * ROW_SUB, V7X_LANES), U32),
        compiler_params=pltpu.CompilerParams(
            dimension_semantics=("arbitrary",),
            vmem_limit_bytes=V7X_VMEM_BYTES - 16 * 1024 * 1024),
        name="dispatch",
    )(tables["run_src"], tables["run_n"], tables["run_dst"], tables["tile_rows"], tables["pad_dst"],
      tables["pad_n"], tables["n_used"], h2b, pos8)


def _moe_kernel(be_ref, group_end_ref, n_used_ref, xs_ref, wup_hbm, bup_ref, wdn_hbm, bdn_ref,
                ys_ref, wup_f32, wdn_f32, wup_bf, wdn_bf, group_count, wsem, *, layer):
    blk = pl.program_id(0)
    n_used = n_used_ref[0]
    active = blk < n_used
    prev = jnp.maximum(blk - 1, 0)
    expert = be_ref[blk]
    new_expert = (blk == 0) | (expert != be_ref[prev])

    def weight_copies(e, buf):
        return (pltpu.make_async_copy(wup_hbm.at[layer, e], wup_f32.at[buf], wsem.at[buf, 0]),
                pltpu.make_async_copy(wdn_hbm.at[layer, e], wdn_f32.at[buf], wsem.at[buf, 1]))

    @pl.when(blk == 0)
    def _():
        group_count[0] = 0
        for c in weight_copies(expert, 0):
            c.start()

    @pl.when(active & new_expert)
    def _():
        par = group_count[0] % 2
        group_count[0] = group_count[0] + 1
        next_blk = group_end_ref[expert]

        @pl.when(next_blk < n_used)
        def _():
            for c in weight_copies(be_ref[next_blk], 1 - par):
                c.start()

        for c in weight_copies(expert, par):
            c.wait()
        wup_bf[...] = wup_f32[par].astype(BF16)
        wdn_bf[...] = wdn_f32[par].astype(BF16)

    @pl.when(active)
    def _():
        xs = _unpack_rows(_load_rows(xs_ref, 0, MOE_BLOCK))
        gu = _dot(xs, wup_bf[...]) + bup_ref[0, 0]
        x_glu = jnp.minimum(gu[:, :D_FF], SWIGLU_LIMIT)
        x_lin = jnp.clip(gu[:, D_FF:], -SWIGLU_LIMIT, SWIGLU_LIMIT)
        act = x_glu * _sigmoid(SWIGLU_ALPHA * x_glu) * (x_lin + 1.0)
        y = _dot(act.astype(BF16), wdn_bf[...]) + bdn_ref[0, 0]
        _store_rows(ys_ref, 0, _pack_rows(y.astype(BF16).astype(F32)))

    @pl.when(jnp.logical_not(active))
    def _():
        ys_ref[...] = jnp.zeros_like(ys_ref)


def _moe_call(layer, xs, tables, w_up, b_up, w_down, b_down):
    n_blocks = tables["block_e"].shape[0]
    block_rows = MOE_BLOCK * ROW_SUB
    b_idx = lambda b, be, *_: (layer, be[b], 0, 0)
    grid_spec = pltpu.PrefetchScalarGridSpec(
        num_scalar_prefetch=3,
        grid=(n_blocks,),
        in_specs=[
            pl.BlockSpec((block_rows, V7X_LANES), lambda b, *_: (b, 0)),
            pl.BlockSpec(memory_space=pl.ANY),
            pl.BlockSpec((1, 1, 1, 2 * D_FF), b_idx),
            pl.BlockSpec(memory_space=pl.ANY),
            pl.BlockSpec((1, 1, 1, D_MODEL), b_idx),
        ],
        out_specs=pl.BlockSpec((block_rows, V7X_LANES), lambda b, *_: (b, 0)),
        scratch_shapes=[pltpu.VMEM((2, D_MODEL, 2 * D_FF), F32), pltpu.VMEM((2, D_FF, D_MODEL), F32),
                        pltpu.VMEM((D_MODEL, 2 * D_FF), BF16), pltpu.VMEM((D_FF, D_MODEL), BF16),
                        pltpu.SMEM((1,), I32), pltpu.SemaphoreType.DMA((2, 2))],
    )
    return pl.pallas_call(
        functools.partial(_moe_kernel, layer=layer),
        grid_spec=grid_spec,
        out_shape=jax.ShapeDtypeStruct(xs.shape, U32),
        compiler_params=pltpu.CompilerParams(
            dimension_semantics=("arbitrary",),
            vmem_limit_bytes=V7X_VMEM_BYTES - 8 * 1024 * 1024),
        name="moe",
    )(tables["block_e"], tables["group_end"], tables["n_used"], xs, w_up, b_up, w_down, b_down)


def _combine_kernel(run_src, run_n, run_dst, tile_rows, x1_ref, pos_ref, gate_ref, p_ref, ple_g_ref,
                    gate_w_ref, proj_w_ref, fin_g_ref, ys_hbm, out_ref, stage, run_sem, *, last):
    i = pl.program_id(0)
    n_tiles = pl.num_programs(0)
    slot = i % 2

    def run_copy(tile, e, buf, live=1):
        k = tile * N_EXPERTS + e
        n = run_n[k] * live
        return pltpu.make_async_copy(ys_hbm.at[_row_span(run_dst[k], n)],
                                     stage.at[buf, _row_span(run_src[k], n)], run_sem.at[buf])

    @pl.when(i == 0)
    def _():
        stage[...] = jnp.zeros_like(stage)
        for e in range(N_EXPERTS):
            run_copy(0, e, 0).start()

    n_rows = tile_rows[i]
    pltpu.make_async_copy(ys_hbm.at[_row_span(0, n_rows)], stage.at[slot, _row_span(0, n_rows)],
                          run_sem.at[slot]).wait()

    next_live = jnp.where(i + 1 < n_tiles, 1, 0)
    for e in range(N_EXPERTS):
        run_copy(jnp.minimum(i + 1, n_tiles - 1), e, 1 - slot, next_live).start()

    x2 = x1_ref[...]
    for a0 in range(0, N_STAGE, STAGE_CHUNK):
        a_iota = lax.broadcasted_iota(I32, (MOE_TS, STAGE_CHUNK), 1) + a0
        weights = jnp.zeros((MOE_TS, STAGE_CHUNK), F32)
        for k in range(TOP_K):
            weights = jnp.where(a_iota == pos_ref[:, k:k + 1], gate_ref[:, k:k + 1], weights)
        y_sorted = _unpack_rows(_load_rows(stage.at[slot], a0, STAGE_CHUNK))
        x2 = x2 + _dot(weights.astype(BF16), y_sorted)
    h3 = _rms_norm(x2, ple_g_ref[...]).astype(BF16)
    g = _sigmoid(_dot(h3, gate_w_ref[...]))
    pp = _dot(p_ref[0, 0].astype(BF16), proj_w_ref[...])
    x3 = x2 + g * pp
    if last:
        x3 = _rms_norm(x3, fin_g_ref[...])
    out_ref[...] = x3


def _combine_call(layer, x1, ys, pos_tm, gate_tm, p, tables, lw, fin_g, last):
    T, D = x1.shape
    n_tiles = T // MOE_TS
    n_seq = p.shape[2] // MOE_TS
    consts = [lw["ple_norm"], lw["ple_gate_w"], lw["ple_proj_w"], fin_g]
    grid_spec = pltpu.PrefetchScalarGridSpec(
        num_scalar_prefetch=4,
        grid=(n_tiles,),
        in_specs=[pl.BlockSpec((MOE_TS, D), lambda i, *_: (i, 0)),
                  pl.BlockSpec((MOE_TS, V7X_SUBLANES), lambda i, *_: (i, 0)),
                  pl.BlockSpec((MOE_TS, V7X_SUBLANES), lambda i, *_: (i, 0)),
                  pl.BlockSpec((1, 1, MOE_TS, PLE_DIM), lambda i, *_: (layer, i // n_seq, i % n_seq, 0))]
        + [_const_spec(c.shape) for c in consts] + [pl.BlockSpec(memory_space=pl.ANY)],
        out_specs=pl.BlockSpec((MOE_TS, D), lambda i, *_: (i, 0)),
        scratch_shapes=[pltpu.VMEM((2, N_STAGE * ROW_SUB, V7X_LANES), U32),
                        pltpu.SemaphoreType.DMA((2,))],
    )
    return pl.pallas_call(
        functools.partial(_combine_kernel, last=last),
        grid_spec=grid_spec,
        out_shape=jax.ShapeDtypeStruct((T, D), F32),
        compiler_params=pltpu.CompilerParams(
            dimension_semantics=("arbitrary",),
            vmem_limit_bytes=V7X_VMEM_BYTES - 16 * 1024 * 1024),
        name="combine",
    )(tables["run_src"], tables["run_n"], tables["run_dst"], tables["tile_rows"], x1, pos_tm, gate_tm, p,
      *consts, ys)


def _routing_tables(tile_counts, n_blocks):
    c = tile_counts[:, :, 0].astype(I32)
    c = (c + RUN_ALIGN - 1) // RUN_ALIGN * RUN_ALIGN
    counts = jnp.sum(c, axis=0)
    padded = (counts + MOE_BLOCK - 1) // MOE_BLOCK * MOE_BLOCK
    pad_end = jnp.cumsum(padded)
    pad_start = pad_end - padded
    run_dst = pad_start[None, :] + jnp.cumsum(c, axis=0) - c
    run_src = jnp.cumsum(c, axis=1) - c
    n_used = pad_end[-1] // MOE_BLOCK
    blk0 = jnp.arange(n_blocks, dtype=I32) * MOE_BLOCK
    be = jnp.minimum(jnp.sum(blk0[:, None] >= pad_end[None, :], axis=1), N_EXPERTS - 1)
    be = be[jnp.minimum(jnp.arange(n_blocks), n_used - 1)]
    return {
        "group_end": (pad_end // MOE_BLOCK).astype(I32),
        "run_src": run_src.reshape(-1).astype(I32), "run_n": c.reshape(-1),
        "run_dst": run_dst.reshape(-1).astype(I32), "tile_rows": jnp.sum(c, axis=1).astype(I32),
        "pad_dst": (pad_start + counts).astype(I32), "pad_n": (padded - counts).astype(I32),
        "n_used": n_used.reshape(1).astype(I32), "block_e": be.astype(I32),
    }


def kernel(x, p, mix_norm, w_in, b_in, pool_w, pool_scale, conv_w, conv_b, conv_norm_g, conv_norm_b,
           sgu_norm_g, sgu_norm_b, sgu_w, sgu_b, branch_w, branch_b, w_out, moe_norm, router_w,
           router_b, expert_w_up, expert_b_up, expert_w_down, expert_b_down, ple_norm, ple_gate_w,
           ple_proj_w, final_norm):
    B, S, D = x.shape
    T = B * S
    depth = w_in.shape[0]
    assert D == D_MODEL and S % MIX_TS == 0 and w_in.shape[2] == IN_WIDTH
    max_rows = T * TOP_K + (T // MOE_TS) * N_EXPERTS * (RUN_ALIGN - 1)
    n_blocks = -(-max_rows // MOE_BLOCK) + N_EXPERTS
    row = lambda a: a.reshape(1, -1)
    b_up4 = expert_b_up[:, :, None, :]
    b_down4 = expert_b_down[:, :, None, :]
    for i in range(depth):
        lw = {
            "mix_norm": row(mix_norm[i]), "w_in": w_in[i].astype(BF16), "b_in": row(b_in[i]),
            "pool_w": pool_w[i].astype(BF16), "pool_scale": row(pool_scale[i]),
            "conv_w": conv_w[i], "conv_b": row(conv_b[i]),
            "conv_norm_g": row(conv_norm_g[i]), "conv_norm_b": row(conv_norm_b[i]),
            "sgu_norm_g": row(sgu_norm_g[i]), "sgu_norm_b": row(sgu_norm_b[i]),
            "sgu_w": sgu_w[i], "sgu_bt": sgu_b[i].T,
            "branch_w": branch_w[i].astype(BF16), "branch_b": branch_b[i],
            "w_out": w_out[i].astype(BF16), "moe_norm": row(moe_norm[i]),
            "router_wt": router_w[i].T, "router_b": router_b[i].reshape(-1, 1),
            "ple_norm": row(ple_norm[i]), "ple_gate_w": ple_gate_w[i].astype(BF16),
            "ple_proj_w": ple_proj_w[i].astype(BF16),
        }
        x1, h2b, pos8, gate8, tile_counts = _mixer_call(x, lw)
        tables = _routing_tables(tile_counts, n_blocks)
        xs = _dispatch_call(h2b, pos8, tables, n_blocks)
        ys = _moe_call(i, xs, tables, expert_w_up, b_up4, expert_w_down, b_down4)
        x = _combine_call(i, x1.reshape(T, D), ys, pos8.T, gate8.T, p, tables, lw,
                          row(final_norm), last=(i == depth - 1)).reshape(B, S, D)
    return x
```

```python
import functools

import jax
import jax.numpy as jnp
from jax import lax
from jax.experimental import pallas as pl
from jax.experimental.pallas import tpu as pltpu

F32 = jnp.float32
BF16 = jnp.bfloat16
I32 = jnp.int32
U32 = jnp.uint32

D_MODEL = 1024
POOL_WINDOWS = (2, 4, 8, 16)
POOL_CH = 128
BRANCH_WIDTH = 512
CONV_K = 31
SGU_CHUNK = 128
SGU_HEADS = 4
N_BRANCH = 3
N_EXPERTS = 32
TOP_K = 4
D_FF = 1024
SWIGLU_LIMIT = 7.0
SWIGLU_ALPHA = 1.702
MOE_BLOCK = 256
MOE_STEP_BLOCKS = 2
PLE_DIM = 256
EPS = 1e-6

V7X_SUBLANES = 8
V7X_LANES = 128
ROW_WORDS = D_MODEL // 2
ROW_SUB = ROW_WORDS // V7X_LANES
RUN_ALIGN = V7X_SUBLANES // ROW_SUB
V7X_VMEM_BYTES = 64 * 1024 * 1024

MIX_TS = 512
HALO = 32
CONV_ROWS = 32
SIDE_CHUNK = 256
MOE_TS = 256
N_STAGE = TOP_K * MOE_TS + N_EXPERTS * (RUN_ALIGN - 1)
STAGE_CHUNK = N_STAGE // 3
assert MIX_TS % MOE_TS == 0 and STAGE_CHUNK * 3 == N_STAGE and STAGE_CHUNK % V7X_SUBLANES == 0

C_POOL = 0
C_CONV = C_POOL + BRANCH_WIDTH
C_SGU_U = C_CONV + 2 * BRANCH_WIDTH
C_SGU_V = C_SGU_U + BRANCH_WIDTH
C_GATE = C_SGU_V + BRANCH_WIDTH
IN_WIDTH = C_GATE + N_BRANCH * D_MODEL
S_POOL = 0
S_SGU_U = S_POOL + BRANCH_WIDTH
S_SGU_V = S_SGU_U + BRANCH_WIDTH
S_GATE = S_SGU_V + BRANCH_WIDTH
SIDE_WIDTH = S_GATE + N_BRANCH * D_MODEL
assert S_SGU_U % SIDE_CHUNK == 0 and SIDE_WIDTH % SIDE_CHUNK == 0


def _rms_norm(x, g):
    return x * lax.rsqrt(jnp.mean(x * x, axis=-1, keepdims=True) + EPS) * g


def _layer_norm(x, g, b):
    mu = jnp.mean(x, axis=-1, keepdims=True)
    xc = x - mu
    var = jnp.mean(xc * xc, axis=-1, keepdims=True)
    return xc * lax.rsqrt(var + EPS) * g + b


def _sigmoid(x):
    return 0.5 * jnp.tanh(0.5 * x) + 0.5


def _dot(a, b):
    return jnp.dot(a, b, preferred_element_type=F32)


def _pack_rows(v):
    bits = lax.bitcast_convert_type(v, U32)
    return (bits[:, :ROW_WORDS] >> 16) | (bits[:, ROW_WORDS:] & jnp.uint32(0xFFFF0000))


def _unpack_rows(w):
    low = lax.bitcast_convert_type(w << 16, F32)
    high = lax.bitcast_convert_type(w & jnp.uint32(0xFFFF0000), F32)
    return jnp.concatenate([low, high], axis=-1).astype(BF16)


def _store_rows(ref_2d, first_row, words):
    n = words.shape[0]
    for j in range(ROW_SUB):
        ref_2d[pl.ds(first_row * ROW_SUB + j, n, stride=ROW_SUB), :] = words[:, j * V7X_LANES:(j + 1) * V7X_LANES]


def _load_rows(ref_2d, first_row, n):
    return jnp.concatenate(
        [ref_2d[pl.ds(first_row * ROW_SUB + j, n, stride=ROW_SUB), :] for j in range(ROW_SUB)], axis=-1)


def _mixer_kernel(x_ref, mix_g_ref, w_in_ref, b_in_ref, pool_w_ref, pool_scale_ref, conv_w_ref,
                  conv_b_ref, cn_g_ref, cn_b_ref, sn_g_ref, sn_b_ref, sgu_w_ref, sgu_bt_ref,
                  branch_w_ref, branch_b_ref, w_out_ref, moe_g_ref, rw_t_ref, rb_ref,
                  x1_ref, h2b_ref, pos_ref, gate_ref, counts_ref,
                  pool_hist, conv_hist, conv_shift, side_buf, x1_prev, *, n_seq):
    i = pl.program_id(0)
    n_tiles = pl.num_programs(0) - 1
    route_refs = (moe_g_ref, rw_t_ref, rb_ref, h2b_ref, pos_ref, gate_ref, counts_ref)

    @pl.when(i == 0)
    def _():
        x1_prev[...] = jnp.zeros_like(x1_prev)

    @pl.when(i % n_seq == 0)
    def _():
        pool_hist[0:HALO, :] = jnp.zeros((HALO, BRANCH_WIDTH), F32)
        conv_hist[0:HALO, :] = jnp.zeros((HALO, BRANCH_WIDTH), F32)

    @pl.when(i < n_tiles)
    def _():
        _mix_tile(i % n_seq, x_ref, mix_g_ref, w_in_ref, b_in_ref, pool_w_ref, pool_scale_ref, conv_w_ref,
                  conv_b_ref, cn_g_ref, cn_b_ref, sn_g_ref, sn_b_ref, sgu_w_ref, sgu_bt_ref, branch_w_ref,
                  branch_b_ref, w_out_ref, x1_ref, pool_hist, conv_hist, conv_shift, side_buf, x1_prev,
                  functools.partial(_route_tile, x1_prev, *route_refs))

    @pl.when(i == n_tiles)
    def _():
        _route_tile(x1_prev, *route_refs)


def _mix_tile(s, x_ref, mix_g_ref, w_in_ref, b_in_ref, pool_w_ref, pool_scale_ref, conv_w_ref,
              conv_b_ref, cn_g_ref, cn_b_ref, sn_g_ref, sn_b_ref, sgu_w_ref, sgu_bt_ref, branch_w_ref,
              branch_b_ref, w_out_ref, x1_ref, pool_hist, conv_hist, conv_shift, side_buf, x1_prev,
              route_previous):
    ts = MIX_TS
    x = x_ref[0]
    h = _rms_norm(x, mix_g_ref[...]).astype(BF16)

    def in_proj(c0, width):
        return _dot(h, w_in_ref[:, c0:c0 + width]) + b_in_ref[:, c0:c0 + width]

    zb = in_proj(C_CONV, 2 * BRANCH_WIDTH)
    conv_hist[HALO:HALO + ts, :] = zb[:, :BRANCH_WIDTH] * _sigmoid(zb[:, BRANCH_WIDTH:])
    n_shift_rows = HALO + ts - V7X_SUBLANES
    for sft in range(1, V7X_SUBLANES):
        conv_shift[sft - 1] = conv_hist[sft:sft + n_shift_rows, :]
    yb_parts = []
    n_conv_blocks = ts // CONV_ROWS
    n_side_chunks = SIDE_WIDTH // SIDE_CHUNK
    for bi in range(n_conv_blocks):
        r0 = bi * CONV_ROWS
        for ci in range(bi * n_side_chunks // n_conv_blocks, (bi + 1) * n_side_chunks // n_conv_blocks):
            d0 = ci * SIDE_CHUNK
            c0 = d0 if d0 < S_SGU_U else d0 + (C_SGU_U - S_SGU_U)
            side_buf[:, d0:d0 + SIDE_CHUNK] = in_proj(c0, SIDE_CHUNK)
        acc = jnp.zeros((CONV_ROWS, BRANCH_WIDTH), F32) + conv_b_ref[...]
        for k in range(CONV_K):
            off = HALO - (CONV_K - 1) + k + r0
            base, sft = off - off % V7X_SUBLANES, off % V7X_SUBLANES
            if sft == 0:
                window = conv_hist[base:base + CONV_ROWS, :]
            else:
                window = conv_shift[sft - 1, base:base + CONV_ROWS, :]
            acc = acc + conv_w_ref[k:k + 1, :] * window
        yb_rows = _layer_norm(acc, cn_g_ref[...], cn_b_ref[...])
        yb_parts.append((yb_rows * _sigmoid(yb_rows)).astype(BF16))
    yb = jnp.concatenate(yb_parts, axis=0)
    conv_hist[0:HALO, :] = conv_hist[ts:ts + HALO, :]

    za = side_buf[:, S_POOL:S_POOL + BRANCH_WIDTH]
    pool_hist[HALO:HALO + ts, :] = za
    row = lax.broadcasted_iota(I32, (ts, 1), 0) + s * ts
    mixed = []
    for g, w in enumerate(POOL_WINDOWS):
        c0 = g * POOL_CH
        cur = za[:, c0:c0 + POOL_CH]
        acc = cur
        for j in range(1, w):
            acc = acc + pool_hist[HALO - j:HALO - j + ts, c0:c0 + POOL_CH]
        count = jnp.minimum(row + 1, w).astype(F32)
        pooled = (acc / count - cur).astype(BF16)
        mixed.append(_dot(pooled, pool_w_ref[g]))
    ya = (jnp.concatenate(mixed, axis=-1) * pool_scale_ref[...]).astype(BF16)
    pool_hist[0:HALO, :] = pool_hist[ts:ts + HALO, :]

    zu = side_buf[:, S_SGU_U:S_SGU_U + BRANCH_WIDTH]
    zv = side_buf[:, S_SGU_V:S_SGU_V + BRANCH_WIDTH]
    v = _layer_norm(zv, sn_g_ref[...], sn_b_ref[...]).astype(BF16)
    tri = (lax.broadcasted_iota(I32, (SGU_CHUNK, SGU_CHUNK), 0)
           >= lax.broadcasted_iota(I32, (SGU_CHUNK, SGU_CHUNK), 1))
    chunks = []
    for c in range(ts // SGU_CHUNK):
        heads = []
        for hd in range(SGU_HEADS):
            w_tri = jnp.where(tri, sgu_w_ref[hd], 0.0).astype(BF16)
            vv = v[c * SGU_CHUNK:(c + 1) * SGU_CHUNK, hd * 128:(hd + 1) * 128]
            heads.append(_dot(w_tri, vv) + sgu_bt_ref[:, hd:hd + 1])
        chunks.append(jnp.concatenate(heads, axis=-1))
    yc = (zu * jnp.concatenate(chunks, axis=0)).astype(BF16)

    route_previous()

    merged = jnp.zeros((ts, D_MODEL), F32)
    for k, yk in enumerate((ya, yb, yc)):
        proj = _dot(yk, branch_w_ref[k]) + branch_b_ref[k:k + 1, :]
        g0 = S_GATE + k * D_MODEL
        merged = merged + _sigmoid(side_buf[:, g0:g0 + D_MODEL]) * proj
    x1 = x + _dot(merged.astype(BF16), w_out_ref[...])
    x1_ref[0] = x1
    x1_prev[...] = x1


def _route_tile(x1_ref, moe_g_ref, rw_t_ref, rb_ref, h2b_ref, pos_ref, gate_ref, counts_ref):
    ts = MIX_TS
    h2 = _rms_norm(x1_ref[...], moe_g_ref[...])
    h2_hi = h2.astype(BF16)
    h2b_ref[...] = h2_hi
    h2_lo = (h2 - h2_hi.astype(F32)).astype(BF16)
    rw = rw_t_ref[...]
    rw_hi = rw.astype(BF16)
    rw_lo = (rw - rw_hi.astype(F32)).astype(BF16)
    nt = (((1,), (1,)), ((), ()))
    logits = (lax.dot_general(rw_hi, h2_hi, nt, preferred_element_type=F32)
              + lax.dot_general(rw_hi, h2_lo, nt, preferred_element_type=F32)
              + lax.dot_general(rw_lo, h2_hi, nt, preferred_element_type=F32)) + rb_ref[...]
    e_iota = lax.broadcasted_iota(I32, (N_EXPERTS, ts), 0).astype(F32)
    vals = logits
    top_v, sels = [], []
    for _k in range(TOP_K):
        m = jnp.max(vals, axis=0, keepdims=True)
        idx = jnp.min(jnp.where(vals == m, e_iota, float(N_EXPERTS)), axis=0, keepdims=True)
        sel = e_iota == idx
        vals = jnp.where(sel, -jnp.inf, vals)
        top_v.append(m)
        sels.append(sel)
    exps = [jnp.exp(tv - top_v[0]) for tv in top_v]
    denom = exps[0] + exps[1] + exps[2] + exps[3]
    chosen = jnp.zeros((N_EXPERTS, ts), F32)
    for sel in sels:
        chosen = chosen + jnp.where(sel, 1.0, 0.0)
    chosen_b = chosen.astype(BF16)
    t_row = lax.broadcasted_iota(I32, (ts, ts), 0)
    t_col = lax.broadcasted_iota(I32, (ts, ts), 1)
    before = jnp.where(t_row // MOE_TS == t_col // MOE_TS, jnp.where(t_row < t_col, 1.0, 0.0), 0.0)
    prefix = _dot(chosen_b, before.astype(BF16))
    lower = jnp.where(lax.broadcasted_iota(I32, (N_EXPERTS, N_EXPERTS), 0)
                      > lax.broadcasted_iota(I32, (N_EXPERTS, N_EXPERTS), 1), 1.0, 0.0).astype(BF16)
    base_parts = []
    for j in range(ts // MOE_TS):
        lanes = slice(j * MOE_TS, (j + 1) * MOE_TS)
        count = jnp.sum(chosen[:, lanes], axis=1, keepdims=True)
        run_rows = jnp.floor((count + (RUN_ALIGN - 1)) * (1.0 / RUN_ALIGN)) * RUN_ALIGN
        run_start = _dot(lower, jnp.broadcast_to(run_rows, (N_EXPERTS, V7X_LANES)).astype(BF16))[:, 0:1]
        base_parts.append(prefix[:, lanes] + run_start)
        counts_ref[j] = jnp.broadcast_to(count, (N_EXPERTS, V7X_LANES))
    base = jnp.concatenate(base_parts, axis=1)
    zeros4 = jnp.zeros((V7X_SUBLANES - TOP_K, ts), F32)
    pos = [jnp.sum(jnp.where(sel, base, 0.0), axis=0, keepdims=True) for sel in sels]
    pos_ref[...] = jnp.concatenate(pos + [zeros4], axis=0).astype(I32)
    gate_ref[...] = jnp.concatenate([e / denom for e in exps] + [zeros4], axis=0)


def _const_spec(shape):
    nd = len(shape)
    return pl.BlockSpec(shape, lambda *_: (0,) * nd, pipeline_mode=pl.Buffered(1))


def _mixer_call(x, lw):
    B, S, D = x.shape
    ts = MIX_TS
    n_s = S // ts
    T = B * S
    n_tiles = B * n_s
    mixed = lambda i: jnp.minimum(i, n_tiles - 1)
    routed = lambda i: jnp.maximum(i - 1, 0)
    consts = [lw["mix_norm"], lw["w_in"], lw["b_in"], lw["pool_w"], lw["pool_scale"], lw["conv_w"],
              lw["conv_b"], lw["conv_norm_g"], lw["conv_norm_b"], lw["sgu_norm_g"], lw["sgu_norm_b"],
              lw["sgu_w"], lw["sgu_bt"], lw["branch_w"], lw["branch_b"], lw["w_out"], lw["moe_norm"],
              lw["router_wt"], lw["router_b"]]
    x_spec = pl.BlockSpec((1, ts, D), lambda i: (mixed(i) // n_s, mixed(i) % n_s, 0))
    in_specs = [x_spec] + [_const_spec(c.shape) for c in consts]
    out_shape = (
        jax.ShapeDtypeStruct((B, S, D), F32),
        jax.ShapeDtypeStruct((T, D), BF16),
        jax.ShapeDtypeStruct((V7X_SUBLANES, T), I32),
        jax.ShapeDtypeStruct((V7X_SUBLANES, T), F32),
        jax.ShapeDtypeStruct((T // MOE_TS, N_EXPERTS, V7X_LANES), F32),
    )
    out_specs = (
        x_spec,
        pl.BlockSpec((ts, D), lambda i: (routed(i), 0)),
        pl.BlockSpec((V7X_SUBLANES, ts), lambda i: (0, routed(i))),
        pl.BlockSpec((V7X_SUBLANES, ts), lambda i: (0, routed(i))),
        pl.BlockSpec((ts // MOE_TS, N_EXPERTS, V7X_LANES), lambda i: (routed(i), 0, 0)),
    )
    return pl.pallas_call(
        functools.partial(_mixer_kernel, n_seq=n_s),
        grid=(n_tiles + 1,),
        in_specs=in_specs,
        out_specs=out_specs,
        out_shape=out_shape,
        scratch_shapes=[pltpu.VMEM((HALO + ts, BRANCH_WIDTH), F32),
                        pltpu.VMEM((HALO + ts, BRANCH_WIDTH), F32),
                        pltpu.VMEM((V7X_SUBLANES - 1, HALO + ts - V7X_SUBLANES, BRANCH_WIDTH), F32),
                        pltpu.VMEM((ts, SIDE_WIDTH), F32),
                        pltpu.VMEM((ts, D), F32)],
        compiler_params=pltpu.CompilerParams(
            dimension_semantics=("arbitrary",),
            vmem_limit_bytes=V7X_VMEM_BYTES - 8 * 1024 * 1024),
        name="mixer",
    )(x, *consts)


def _row_span(first_row, n_rows):
    return pl.ds(pl.multiple_of(first_row * ROW_SUB, V7X_SUBLANES),
                 pl.multiple_of(n_rows * ROW_SUB, V7X_SUBLANES))


def _dispatch_kernel(run_src, run_n, run_dst, tile_rows, pad_dst, pad_n, n_used_ref, h2b_ref, pos_ref,
                     xs_hbm, stage, zero_rows, run_sem, fill_sem, *, n_blocks):
    i = pl.program_id(0)
    n_tiles = pl.num_programs(0)
    slot = i % 2
    block_rows = MOE_BLOCK * ROW_SUB

    def run_copy(tile, e, buf, live=1):
        k = tile * N_EXPERTS + e
        n = run_n[k] * live
        return pltpu.make_async_copy(stage.at[buf, _row_span(run_src[k], n)],
                                     xs_hbm.at[_row_span(run_dst[k], n)], run_sem.at[buf])

    def wait_runs(tile, buf):
        n = tile_rows[tile]
        pltpu.make_async_copy(stage.at[buf, _row_span(0, n)], xs_hbm.at[_row_span(0, n)],
                              run_sem.at[buf]).wait()

    def pad_copy(e):
        return pltpu.make_async_copy(zero_rows.at[_row_span(0, pad_n[e])],
                                     xs_hbm.at[_row_span(pad_dst[e], pad_n[e])], fill_sem)

    def idle_block_copy(blk):
        return pltpu.make_async_copy(
            zero_rows, xs_hbm.at[pl.ds(pl.multiple_of(blk * block_rows, block_rows), block_rows)], fill_sem)

    @pl.when(i == 0)
    def _():
        zero_rows[...] = jnp.zeros_like(zero_rows)

        def start_idle(blk, c):
            idle_block_copy(blk).start()
            return c

        def wait_idle(blk, c):
            idle_block_copy(blk).wait()
            return c

        for e in range(N_EXPERTS):
            pad_copy(e).start()
        lax.fori_loop(n_used_ref[0], n_blocks, start_idle, 0)
        for e in range(N_EXPERTS):
            pad_copy(e).wait()
        lax.fori_loop(n_used_ref[0], n_blocks, wait_idle, 0)

    @pl.when(i >= 2)
    def _():
        wait_runs(i - 2, slot)

    prev_live = jnp.minimum(i, 1)
    for e in range(N_EXPERTS):
        run_copy(jnp.maximum(i - 1, 0), e, 1 - slot, prev_live).start()

    h2b = h2b_ref[...]
    for a0 in range(0, N_STAGE, STAGE_CHUNK):
        a_iota = lax.broadcasted_iota(I32, (STAGE_CHUNK, MOE_TS), 0) + a0
        onehot = jnp.zeros((STAGE_CHUNK, MOE_TS), F32)
        for k in range(TOP_K):
            onehot = jnp.where(a_iota == pos_ref[k:k + 1, :], 1.0, onehot)
        _store_rows(stage.at[slot], a0, _pack_rows(_dot(onehot.astype(BF16), h2b)))

    @pl.when(i == n_tiles - 1)
    def _():
        for e in range(N_EXPERTS):
            run_copy(i, e, slot).start()

        @pl.when(i >= 1)
        def _():
            wait_runs(i - 1, 1 - slot)
        wait_runs(i, slot)


def _dispatch_call(h2b, pos8, tables, n_blocks):
    T, D = h2b.shape
    n_tiles = T // MOE_TS
    grid_spec = pltpu.PrefetchScalarGridSpec(
        num_scalar_prefetch=7,
        grid=(n_tiles,),
        in_specs=[pl.BlockSpec((MOE_TS, D), lambda i, *_: (i, 0)),
                  pl.BlockSpec((V7X_SUBLANES, MOE_TS), lambda i, *_: (0, i))],
        out_specs=pl.BlockSpec(memory_space=pl.ANY),
        scratch_shapes=[pltpu.VMEM((2, N_STAGE * ROW_SUB, V7X_LANES), U32),
                        pltpu.VMEM((MOE_BLOCK * ROW_SUB, V7X_LANES), U32),
                        pltpu.SemaphoreType.DMA((2,)),
                        pltpu.SemaphoreType.DMA(())],
    )
    return pl.pallas_call(
        functools.partial(_dispatch_kernel, n_blocks=n_blocks),
        grid_spec=grid_spec,
        out_shape=jax.ShapeDtypeStruct((n_blocks * MOE_BLOCK * ROW_SUB, V7X_LANES), U32),
        compiler_params=pltpu.CompilerParams(
            dimension_semantics=("arbitrary",),
            vmem_limit_bytes=V7X_VMEM_BYTES - 16 * 1024 * 1024),
        name="dispatch",
    )(tables["run_src"], tables["run_n"], tables["run_dst"], tables["tile_rows"], tables["pad_dst"],
      tables["pad_n"], tables["n_used"], h2b, pos8)


def _moe_kernel(be_ref, group_end_ref, n_used_ref, xs_ref, wup_hbm, wdn_hbm, bup0_ref, bdn0_ref,
                bup1_ref, bdn1_ref, ys_ref, wup_f32, wdn_f32, wup_bf, wdn_bf, group_count, wsem, *, layer):
    step = pl.program_id(0)
    n_used = n_used_ref[0]

    def weight_copies(e, buf):
        return (pltpu.make_async_copy(wup_hbm.at[layer, e], wup_f32.at[buf], wsem.at[buf, 0]),
                pltpu.make_async_copy(wdn_hbm.at[layer, e], wdn_f32.at[buf], wsem.at[buf, 1]))

    @pl.when(step == 0)
    def _():
        group_count[0] = 0
        for c in weight_copies(be_ref[0], 0):
            c.start()

    halves = []
    for half in range(MOE_STEP_BLOCKS):
        blk = step * MOE_STEP_BLOCKS + half
        expert = be_ref[blk]
        new_expert = (blk == 0) | (expert != be_ref[jnp.maximum(blk - 1, 0)])

        @pl.when((blk < n_used) & new_expert)
        def _(expert=expert):
            par = group_count[0] % 2
            group_count[0] = group_count[0] + 1
            next_blk = group_end_ref[expert]

            @pl.when(next_blk < n_used)
            def _():
                for c in weight_copies(be_ref[next_blk], 1 - par):
                    c.start()

            for c in weight_copies(expert, par):
                c.wait()
            wup_bf[par] = wup_f32[par].astype(BF16)
            wdn_bf[par] = wdn_f32[par].astype(BF16)

        halves.append((group_count[0] - 1) % 2)

    @pl.when(step * MOE_STEP_BLOCKS < n_used)
    def _():
        for half, (par, bup_ref, bdn_ref) in enumerate(zip(halves, (bup0_ref, bup1_ref), (bdn0_ref, bdn1_ref))):
            r0 = half * MOE_BLOCK
            xs = _unpack_rows(_load_rows(xs_ref, r0, MOE_BLOCK))
            gu = _dot(xs, wup_bf[par]) + bup_ref[0, 0]
            x_glu = jnp.minimum(gu[:, :D_FF], SWIGLU_LIMIT)
            x_lin = jnp.clip(gu[:, D_FF:], -SWIGLU_LIMIT, SWIGLU_LIMIT)
            act = x_glu * _sigmoid(SWIGLU_ALPHA * x_glu) * (x_lin + 1.0)
            y = _dot(act.astype(BF16), wdn_bf[par]) + bdn_ref[0, 0]
            _store_rows(ys_ref, r0, _pack_rows(y.astype(BF16).astype(F32)))

    @pl.when(step * MOE_STEP_BLOCKS >= n_used)
    def _():
        ys_ref[...] = jnp.zeros_like(ys_ref)


def _moe_call(layer, xs, tables, w_up, b_up, w_down, b_down):
    n_blocks = tables["block_e"].shape[0]
    assert n_blocks % MOE_STEP_BLOCKS == 0
    step_rows = MOE_STEP_BLOCKS * MOE_BLOCK * ROW_SUB
    bias_spec = lambda width, half: pl.BlockSpec(
        (1, 1, 1, width), lambda s, be, *_: (layer, be[s * MOE_STEP_BLOCKS + half], 0, 0))
    grid_spec = pltpu.PrefetchScalarGridSpec(
        num_scalar_prefetch=3,
        grid=(n_blocks // MOE_STEP_BLOCKS,),
        in_specs=[
            pl.BlockSpec((step_rows, V7X_LANES), lambda s, *_: (s, 0)),
            pl.BlockSpec(memory_space=pl.ANY),
            pl.BlockSpec(memory_space=pl.ANY),
            bias_spec(2 * D_FF, 0), bias_spec(D_MODEL, 0), bias_spec(2 * D_FF, 1), bias_spec(D_MODEL, 1),
        ],
        out_specs=pl.BlockSpec((step_rows, V7X_LANES), lambda s, *_: (s, 0)),
        scratch_shapes=[pltpu.VMEM((2, D_MODEL, 2 * D_FF), F32), pltpu.VMEM((2, D_FF, D_MODEL), F32),
                        pltpu.VMEM((2, D_MODEL, 2 * D_FF), BF16), pltpu.VMEM((2, D_FF, D_MODEL), BF16),
                        pltpu.SMEM((1,), I32), pltpu.SemaphoreType.DMA((2, 2))],
    )
    return pl.pallas_call(
        functools.partial(_moe_kernel, layer=layer),
        grid_spec=grid_spec,
        out_shape=jax.ShapeDtypeStruct(xs.shape, U32),
        compiler_params=pltpu.CompilerParams(
            dimension_semantics=("arbitrary",),
            vmem_limit_bytes=V7X_VMEM_BYTES - 8 * 1024 * 1024),
        name="moe",
    )(tables["block_e"], tables["group_end"], tables["n_used"], xs, w_up, w_down, b_up, b_down, b_up, b_down)


def _combine_kernel(run_src, run_n, run_dst, tile_rows, x1_ref, pos_ref, gate_ref, p_ref, ple_g_ref,
                    gate_w_ref, proj_w_ref, fin_g_ref, ys_hbm, out_ref, stage, run_sem, *, last):
    i = pl.program_id(0)
    n_tiles = pl.num_programs(0)
    slot = i % 2

    def run_copy(tile, e, buf, live=1):
        k = tile * N_EXPERTS + e
        n = run_n[k] * live
        return pltpu.make_async_copy(ys_hbm.at[_row_span(run_dst[k], n)],
                                     stage.at[buf, _row_span(run_src[k], n)], run_sem.at[buf])

    @pl.when(i == 0)
    def _():
        stage[...] = jnp.zeros_like(stage)
        for e in range(N_EXPERTS):
            run_copy(0, e, 0).start()

    n_rows = tile_rows[i]
    pltpu.make_async_copy(ys_hbm.at[_row_span(0, n_rows)], stage.at[slot, _row_span(0, n_rows)],
                          run_sem.at[slot]).wait()

    next_live = jnp.where(i + 1 < n_tiles, 1, 0)
    for e in range(N_EXPERTS):
        run_copy(jnp.minimum(i + 1, n_tiles - 1), e, 1 - slot, next_live).start()

    x2 = x1_ref[...]
    for a0 in range(0, N_STAGE, STAGE_CHUNK):
        a_iota = lax.broadcasted_iota(I32, (MOE_TS, STAGE_CHUNK), 1) + a0
        weights = jnp.zeros((MOE_TS, STAGE_CHUNK), F32)
        for k in range(TOP_K):
            weights = jnp.where(a_iota == pos_ref[:, k:k + 1], gate_ref[:, k:k + 1], weights)
        y_sorted = _unpack_rows(_load_rows(stage.at[slot], a0, STAGE_CHUNK))
        x2 = x2 + _dot(weights.astype(BF16), y_sorted)
    h3 = _rms_norm(x2, ple_g_ref[...]).astype(BF16)
    g = _sigmoid(_dot(h3, gate_w_ref[...]))
    pp = _dot(p_ref[0, 0].astype(BF16), proj_w_ref[...])
    x3 = x2 + g * pp
    if last:
        x3 = _rms_norm(x3, fin_g_ref[...])
    out_ref[...] = x3


def _combine_call(layer, x1, ys, pos_tm, gate_tm, p, tables, lw, fin_g, last):
    T, D = x1.shape
    n_tiles = T // MOE_TS
    n_seq = p.shape[2] // MOE_TS
    consts = [lw["ple_norm"], lw["ple_gate_w"], lw["ple_proj_w"], fin_g]
    grid_spec = pltpu.PrefetchScalarGridSpec(
        num_scalar_prefetch=4,
        grid=(n_tiles,),
        in_specs=[pl.BlockSpec((MOE_TS, D), lambda i, *_: (i, 0)),
                  pl.BlockSpec((MOE_TS, V7X_SUBLANES), lambda i, *_: (i, 0)),
                  pl.BlockSpec((MOE_TS, V7X_SUBLANES), lambda i, *_: (i, 0)),
                  pl.BlockSpec((1, 1, MOE_TS, PLE_DIM), lambda i, *_: (layer, i // n_seq, i % n_seq, 0))]
        + [_const_spec(c.shape) for c in consts] + [pl.BlockSpec(memory_space=pl.ANY)],
        out_specs=pl.BlockSpec((MOE_TS, D), lambda i, *_: (i, 0)),
        scratch_shapes=[pltpu.VMEM((2, N_STAGE * ROW_SUB, V7X_LANES), U32),
                        pltpu.SemaphoreType.DMA((2,))],
    )
    return pl.pallas_call(
        functools.partial(_combine_kernel, last=last),
        grid_spec=grid_spec,
        out_shape=jax.ShapeDtypeStruct((T, D), F32),
        compiler_params=pltpu.CompilerParams(
            dimension_semantics=("arbitrary",),
            vmem_limit_bytes=V7X_VMEM_BYTES - 16 * 1024 * 1024),
        name="combine",
    )(tables["run_src"], tables["run_n"], tables["run_dst"], tables["tile_rows"], x1, pos_tm, gate_tm, p,
      *consts, ys)


def _routing_tables(tile_counts, n_blocks):
    c = tile_counts[:, :, 0].astype(I32)
    c = (c + RUN_ALIGN - 1) // RUN_ALIGN * RUN_ALIGN
    counts = jnp.sum(c, axis=0)
    padded = (counts + MOE_BLOCK - 1) // MOE_BLOCK * MOE_BLOCK
    pad_end = jnp.cumsum(padded)
    pad_start = pad_end - padded
    run_dst = pad_start[None, :] + jnp.cumsum(c, axis=0) - c
    run_src = jnp.cumsum(c, axis=1) - c
    n_used = pad_end[-1] // MOE_BLOCK
    blk0 = jnp.arange(n_blocks, dtype=I32) * MOE_BLOCK
    be = jnp.minimum(jnp.sum(blk0[:, None] >= pad_end[None, :], axis=1), N_EXPERTS - 1)
    be = be[jnp.minimum(jnp.arange(n_blocks), n_used - 1)]
    return {
        "group_end": (pad_end // MOE_BLOCK).astype(I32),
        "run_src": run_src.reshape(-1).astype(I32), "run_n": c.reshape(-1),
        "run_dst": run_dst.reshape(-1).astype(I32), "tile_rows": jnp.sum(c, axis=1).astype(I32),
        "pad_dst": (pad_start + counts).astype(I32), "pad_n": (padded - counts).astype(I32),
        "n_used": n_used.reshape(1).astype(I32), "block_e": be.astype(I32),
    }


def kernel(x, p, mix_norm, w_in, b_in, pool_w, pool_scale, conv_w, conv_b, conv_norm_g, conv_norm_b,
           sgu_norm_g, sgu_norm_b, sgu_w, sgu_b, branch_w, branch_b, w_out, moe_norm, router_w,
           router_b, expert_w_up, expert_b_up, expert_w_down, expert_b_down, ple_norm, ple_gate_w,
           ple_proj_w, final_norm):
    B, S, D = x.shape
    T = B * S
    depth = w_in.shape[0]
    assert D == D_MODEL and S % MIX_TS == 0 and w_in.shape[2] == IN_WIDTH
    max_rows = T * TOP_K + (T // MOE_TS) * N_EXPERTS * (RUN_ALIGN - 1)
    n_blocks = -(-max_rows // MOE_BLOCK) + N_EXPERTS
    n_blocks += -n_blocks % MOE_STEP_BLOCKS
    row = lambda a: a.reshape(1, -1)
    b_up4 = expert_b_up[:, :, None, :]
    b_down4 = expert_b_down[:, :, None, :]
    for i in range(depth):
        lw = {
            "mix_norm": row(mix_norm[i]), "w_in": w_in[i].astype(BF16), "b_in": row(b_in[i]),
            "pool_w": pool_w[i].astype(BF16), "pool_scale": row(pool_scale[i]),
            "conv_w": conv_w[i], "conv_b": row(conv_b[i]),
            "conv_norm_g": row(conv_norm_g[i]), "conv_norm_b": row(conv_norm_b[i]),
            "sgu_norm_g": row(sgu_norm_g[i]), "sgu_norm_b": row(sgu_norm_b[i]),
            "sgu_w": sgu_w[i], "sgu_bt": sgu_b[i].T,
            "branch_w": branch_w[i].astype(BF16), "branch_b": branch_b[i],
            "w_out": w_out[i].astype(BF16), "moe_norm": row(moe_norm[i]),
            "router_wt": router_w[i].T, "router_b": router_b[i].reshape(-1, 1),
            "ple_norm": row(ple_norm[i]), "ple_gate_w": ple_gate_w[i].astype(BF16),
            "ple_proj_w": ple_proj_w[i].astype(BF16),
        }
        x1, h2b, pos8, gate8, tile_counts = _mixer_call(x, lw)
        tables = _routing_tables(tile_counts, n_blocks)
        xs = _dispatch_call(h2b, pos8, tables, n_blocks)
        ys = _moe_call(i, xs, tables, expert_w_up, b_up4, expert_w_down, b_down4)
        x = _combine_call(i, x1.reshape(T, D), ys, pos8.T, gate8.T, p, tables, lw,
                          row(final_norm), last=(i == depth - 1)).reshape(B, S, D)
    return x
```

```python
import functools

import jax
import jax.numpy as jnp
from jax import lax
from jax.experimental import pallas as pl
from jax.experimental.pallas import tpu as pltpu

F32 = jnp.float32
BF16 = jnp.bfloat16
I32 = jnp.int32
U32 = jnp.uint32

D_MODEL = 1024
POOL_WINDOWS = (2, 4, 8, 16)
POOL_CH = 128
BRANCH_WIDTH = 512
CONV_K = 31
SGU_CHUNK = 128
SGU_HEADS = 4
N_BRANCH = 3
N_EXPERTS = 32
TOP_K = 4
D_FF = 1024
SWIGLU_LIMIT = 7.0
SWIGLU_ALPHA = 1.702
MOE_BLOCK = 256
MOE_STEP_BLOCKS = 2
assert MOE_STEP_BLOCKS <= 2
PLE_DIM = 256
EPS = 1e-6

V7X_SUBLANES = 8
V7X_LANES = 128
ROW_WORDS = D_MODEL // 2
ROW_SUB = ROW_WORDS // V7X_LANES
RUN_ALIGN = V7X_SUBLANES // ROW_SUB
V7X_VMEM_BYTES = 64 * 1024 * 1024

MIX_TS = 512
HALO = 32
CONV_ROWS = 32
SIDE_CHUNK = 256
MOE_TS = 256
N_STAGE = TOP_K * MOE_TS + N_EXPERTS * (RUN_ALIGN - 1)
STAGE_CHUNK = N_STAGE // 3
assert MIX_TS % MOE_TS == 0 and STAGE_CHUNK * 3 == N_STAGE and STAGE_CHUNK % V7X_SUBLANES == 0

C_POOL = 0
C_CONV = C_POOL + BRANCH_WIDTH
C_SGU_U = C_CONV + 2 * BRANCH_WIDTH
C_SGU_V = C_SGU_U + BRANCH_WIDTH
C_GATE = C_SGU_V + BRANCH_WIDTH
IN_WIDTH = C_GATE + N_BRANCH * D_MODEL
S_POOL = 0
S_SGU_U = S_POOL + BRANCH_WIDTH
S_SGU_V = S_SGU_U + BRANCH_WIDTH
S_GATE = S_SGU_V + BRANCH_WIDTH
SIDE_WIDTH = S_GATE + N_BRANCH * D_MODEL
assert S_SGU_U % SIDE_CHUNK == 0 and SIDE_WIDTH % SIDE_CHUNK == 0


def _rms_norm(x, g):
    return x * lax.rsqrt(jnp.mean(x * x, axis=-1, keepdims=True) + EPS) * g


def _layer_norm(x, g, b):
    mu = jnp.mean(x, axis=-1, keepdims=True)
    xc = x - mu
    var = jnp.mean(xc * xc, axis=-1, keepdims=True)
    return xc * lax.rsqrt(var + EPS) * g + b


def _sigmoid(x):
    return 0.5 * jnp.tanh(0.5 * x) + 0.5


def _dot(a, b):
    return jnp.dot(a, b, preferred_element_type=F32)


def _pack_rows(v):
    bits = lax.bitcast_convert_type(v, U32)
    return (bits[:, :ROW_WORDS] >> 16) | (bits[:, ROW_WORDS:] & jnp.uint32(0xFFFF0000))


def _unpack_rows(w):
    low = lax.bitcast_convert_type(w << 16, F32)
    high = lax.bitcast_convert_type(w & jnp.uint32(0xFFFF0000), F32)
    return jnp.concatenate([low, high], axis=-1).astype(BF16)


def _store_rows(ref_2d, first_row, words):
    n = words.shape[0]
    for j in range(ROW_SUB):
        ref_2d[pl.ds(first_row * ROW_SUB + j, n, stride=ROW_SUB), :] = words[:, j * V7X_LANES:(j + 1) * V7X_LANES]


def _load_rows(ref_2d, first_row, n):
    return jnp.concatenate(
        [ref_2d[pl.ds(first_row * ROW_SUB + j, n, stride=ROW_SUB), :] for j in range(ROW_SUB)], axis=-1)


def _mixer_kernel(x_ref, x_next_ref, mix_g_ref, w_in_ref, b_in_ref, pool_w_ref, pool_scale_ref, conv_w_ref,
                  conv_b_ref, cn_g_ref, cn_b_ref, sn_g_ref, sn_b_ref, sgu_w_ref, sgu_bt_ref,
                  branch_w_ref, branch_b_ref, w_out_ref, moe_g_ref, rw_t_ref, rb_ref,
                  x1_ref, h2b_ref, pos_ref, gate_ref, counts_ref,
                  pool_hist, conv_hist, conv_shift, side_buf, x1_prev, before_buf, h_buf, *, n_seq):
    i = pl.program_id(0)
    n_tiles = pl.num_programs(0) - 1
    route_refs = (before_buf, moe_g_ref, rw_t_ref, rb_ref, h2b_ref, pos_ref, gate_ref, counts_ref)

    @pl.when(i == 0)
    def _():
        x1_prev[...] = jnp.zeros_like(x1_prev)
        h_buf[...] = _rms_norm(x_ref[0], mix_g_ref[...]).astype(BF16)
        t_row = lax.broadcasted_iota(I32, (MIX_TS, MIX_TS), 0)
        t_col = lax.broadcasted_iota(I32, (MIX_TS, MIX_TS), 1)
        same_tile = t_row // MOE_TS == t_col // MOE_TS
        before_buf[...] = jnp.where(same_tile, jnp.where(t_row < t_col, 1.0, 0.0), 0.0).astype(BF16)

    @pl.when(i % n_seq == 0)
    def _():
        pool_hist[0:HALO, :] = jnp.zeros((HALO, BRANCH_WIDTH), F32)
        conv_hist[0:HALO, :] = jnp.zeros((HALO, BRANCH_WIDTH), F32)

    @pl.when(i < n_tiles)
    def _():
        _mix_tile(i % n_seq, x_ref, x_next_ref, mix_g_ref, w_in_ref, b_in_ref, pool_w_ref, pool_scale_ref,
                  conv_w_ref, conv_b_ref, cn_g_ref, cn_b_ref, sn_g_ref, sn_b_ref, sgu_w_ref, sgu_bt_ref,
                  branch_w_ref, branch_b_ref, w_out_ref, x1_ref, pool_hist, conv_hist, conv_shift, side_buf,
                  x1_prev, h_buf, functools.partial(_route_tile, x1_prev, *route_refs))

    @pl.when(i == n_tiles)
    def _():
        _route_tile(x1_prev, *route_refs)


def _mix_tile(s, x_ref, x_next_ref, mix_g_ref, w_in_ref, b_in_ref, pool_w_ref, pool_scale_ref, conv_w_ref,
              conv_b_ref, cn_g_ref, cn_b_ref, sn_g_ref, sn_b_ref, sgu_w_ref, sgu_bt_ref, branch_w_ref,
              branch_b_ref, w_out_ref, x1_ref, pool_hist, conv_hist, conv_shift, side_buf, x1_prev, h_buf,
              route_previous):
    ts = MIX_TS
    x = x_ref[0]
    h = h_buf[...]

    def in_proj(c0, width):
        return _dot(h, w_in_ref[:, c0:c0 + width]) + b_in_ref[:, c0:c0 + width]

    zb = in_proj(C_CONV, 2 * BRANCH_WIDTH)
    conv_hist[HALO:HALO + ts, :] = zb[:, :BRANCH_WIDTH] * _sigmoid(zb[:, BRANCH_WIDTH:])
    n_shift_rows = HALO + ts - V7X_SUBLANES
    for sft in range(1, V7X_SUBLANES):
        conv_shift[sft - 1] = conv_hist[sft:sft + n_shift_rows, :]
    yb_parts = []
    n_conv_blocks = ts // CONV_ROWS
    n_side_chunks = SIDE_WIDTH // SIDE_CHUNK
    for bi in range(n_conv_blocks):
        r0 = bi * CONV_ROWS
        for ci in range(bi * n_side_chunks // n_conv_blocks, (bi + 1) * n_side_chunks // n_conv_blocks):
            d0 = ci * SIDE_CHUNK
            c0 = d0 if d0 < S_SGU_U else d0 + (C_SGU_U - S_SGU_U)
            side_buf[:, d0:d0 + SIDE_CHUNK] = in_proj(c0, SIDE_CHUNK)
        acc = jnp.zeros((CONV_ROWS, BRANCH_WIDTH), F32) + conv_b_ref[...]
        for k in range(CONV_K):
            off = HALO - (CONV_K - 1) + k + r0
            base, sft = off - off % V7X_SUBLANES, off % V7X_SUBLANES
            if sft == 0:
                window = conv_hist[base:base + CONV_ROWS, :]
            else:
                window = conv_shift[sft - 1, base:base + CONV_ROWS, :]
            acc = acc + conv_w_ref[k:k + 1, :] * window
        yb_rows = _layer_norm(acc, cn_g_ref[...], cn_b_ref[...])
        yb_parts.append((yb_rows * _sigmoid(yb_rows)).astype(BF16))
    yb = jnp.concatenate(yb_parts, axis=0)
    conv_hist[0:HALO, :] = conv_hist[ts:ts + HALO, :]

    za = side_buf[:, S_POOL:S_POOL + BRANCH_WIDTH]
    pool_hist[HALO:HALO + ts, :] = za
    row = lax.broadcasted_iota(I32, (ts, 1), 0) + s * ts
    mixed = []
    for g, w in enumerate(POOL_WINDOWS):
        c0 = g * POOL_CH
        cur = za[:, c0:c0 + POOL_CH]
        acc = cur
        for j in range(1, w):
            acc = acc + pool_hist[HALO - j:HALO - j + ts, c0:c0 + POOL_CH]
        count = jnp.minimum(row + 1, w).astype(F32)
        pooled = (acc / count - cur).astype(BF16)
        mixed.append(_dot(pooled, pool_w_ref[g]))
    ya = (jnp.concatenate(mixed, axis=-1) * pool_scale_ref[...]).astype(BF16)
    pool_hist[0:HALO, :] = pool_hist[ts:ts + HALO, :]

    zu = side_buf[:, S_SGU_U:S_SGU_U + BRANCH_WIDTH]
    zv = side_buf[:, S_SGU_V:S_SGU_V + BRANCH_WIDTH]
    v = _layer_norm(zv, sn_g_ref[...], sn_b_ref[...]).astype(BF16)
    tri = (lax.broadcasted_iota(I32, (SGU_CHUNK, SGU_CHUNK), 0)
           >= lax.broadcasted_iota(I32, (SGU_CHUNK, SGU_CHUNK), 1))
    w_tri = [jnp.where(tri, sgu_w_ref[hd], 0.0).astype(BF16) for hd in range(SGU_HEADS)]
    chunks = []
    for c in range(ts // SGU_CHUNK):
        heads = []
        for hd in range(SGU_HEADS):
            vv = v[c * SGU_CHUNK:(c + 1) * SGU_CHUNK, hd * 128:(hd + 1) * 128]
            heads.append(_dot(w_tri[hd], vv) + sgu_bt_ref[:, hd:hd + 1])
        chunks.append(jnp.concatenate(heads, axis=-1))
    yc = (zu * jnp.concatenate(chunks, axis=0)).astype(BF16)

    route_previous()
    h_buf[...] = _rms_norm(x_next_ref[0], mix_g_ref[...]).astype(BF16)

    merged = jnp.zeros((ts, D_MODEL), F32)
    for k, yk in enumerate((ya, yb, yc)):
        proj = _dot(yk, branch_w_ref[k]) + branch_b_ref[k:k + 1, :]
        g0 = S_GATE + k * D_MODEL
        merged = merged + _sigmoid(side_buf[:, g0:g0 + D_MODEL]) * proj
    x1 = x + _dot(merged.astype(BF16), w_out_ref[...])
    x1_ref[0] = x1
    x1_prev[...] = x1


def _route_tile(x1_ref, before_ref, moe_g_ref, rw_t_ref, rb_ref, h2b_ref, pos_ref, gate_ref, counts_ref):
    ts = MIX_TS
    h2 = _rms_norm(x1_ref[...], moe_g_ref[...])
    h2_hi = h2.astype(BF16)
    h2b_ref[...] = h2_hi
    h2_lo = (h2 - h2_hi.astype(F32)).astype(BF16)
    rw = rw_t_ref[...]
    rw_hi = rw.astype(BF16)
    rw_lo = (rw - rw_hi.astype(F32)).astype(BF16)
    nt = (((1,), (1,)), ((), ()))
    logits = (lax.dot_general(rw_hi, h2_hi, nt, preferred_element_type=F32)
              + lax.dot_general(rw_hi, h2_lo, nt, preferred_element_type=F32)
              + lax.dot_general(rw_lo, h2_hi, nt, preferred_element_type=F32)) + rb_ref[...]
    e_iota = lax.broadcasted_iota(I32, (N_EXPERTS, ts), 0).astype(F32)
    vals = logits
    top_v, sels = [], []
    for _k in range(TOP_K):
        m = jnp.max(vals, axis=0, keepdims=True)
        idx = jnp.min(jnp.where(vals == m, e_iota, float(N_EXPERTS)), axis=0, keepdims=True)
        sel = e_iota == idx
        vals = jnp.where(sel, -jnp.inf, vals)
        top_v.append(m)
        sels.append(sel)
    exps = [jnp.exp(tv - top_v[0]) for tv in top_v]
    denom = exps[0] + exps[1] + exps[2] + exps[3]
    chosen = jnp.zeros((N_EXPERTS, ts), F32)
    for sel in sels:
        chosen = chosen + jnp.where(sel, 1.0, 0.0)
    chosen_b = chosen.astype(BF16)
    prefix = _dot(chosen_b, before_ref[...])
    lower = jnp.where(lax.broadcasted_iota(I32, (N_EXPERTS, N_EXPERTS), 0)
                      > lax.broadcasted_iota(I32, (N_EXPERTS, N_EXPERTS), 1), 1.0, 0.0).astype(BF16)
    base_parts = []
    for j in range(ts // MOE_TS):
        lanes = slice(j * MOE_TS, (j + 1) * MOE_TS)
        count = jnp.sum(chosen[:, lanes], axis=1, keepdims=True)
        run_rows = jnp.floor((count + (RUN_ALIGN - 1)) * (1.0 / RUN_ALIGN)) * RUN_ALIGN
        run_start = _dot(lower, jnp.broadcast_to(run_rows, (N_EXPERTS, V7X_LANES)).astype(BF16))[:, 0:1]
        base_parts.append(prefix[:, lanes] + run_start)
        counts_ref[j] = jnp.broadcast_to(count, (N_EXPERTS, V7X_LANES))
    base = jnp.concatenate(base_parts, axis=1)
    zeros4 = jnp.zeros((V7X_SUBLANES - TOP_K, ts), F32)
    pos = [jnp.sum(jnp.where(sel, base, 0.0), axis=0, keepdims=True) for sel in sels]
    pos_ref[...] = jnp.concatenate(pos + [zeros4], axis=0).astype(I32)
    gate_ref[...] = jnp.concatenate([e / denom for e in exps] + [zeros4], axis=0)


def _const_spec(shape):
    nd = len(shape)
    return pl.BlockSpec(shape, lambda *_: (0,) * nd, pipeline_mode=pl.Buffered(1))


def _mixer_call(x, lw):
    B, S, D = x.shape
    ts = MIX_TS
    n_s = S // ts
    T = B * S
    n_tiles = B * n_s
    mixed = lambda i: jnp.minimum(i, n_tiles - 1)
    routed = lambda i: jnp.maximum(i - 1, 0)
    consts = [lw["mix_norm"], lw["w_in"], lw["b_in"], lw["pool_w"], lw["pool_scale"], lw["conv_w"],
              lw["conv_b"], lw["conv_norm_g"], lw["conv_norm_b"], lw["sgu_norm_g"], lw["sgu_norm_b"],
              lw["sgu_w"], lw["sgu_bt"], lw["branch_w"], lw["branch_b"], lw["w_out"], lw["moe_norm"],
              lw["router_wt"], lw["router_b"]]
    x_spec = pl.BlockSpec((1, ts, D), lambda i: (mixed(i) // n_s, mixed(i) % n_s, 0))
    x_next_spec = pl.BlockSpec((1, ts, D), lambda i: (mixed(i + 1) // n_s, mixed(i + 1) % n_s, 0))
    in_specs = [x_spec, x_next_spec] + [_const_spec(c.shape) for c in consts]
    out_shape = (
        jax.ShapeDtypeStruct((B, S, D), F32),
        jax.ShapeDtypeStruct((T, D), BF16),
        jax.ShapeDtypeStruct((V7X_SUBLANES, T), I32),
        jax.ShapeDtypeStruct((V7X_SUBLANES, T), F32),
        jax.ShapeDtypeStruct((T // MOE_TS, N_EXPERTS, V7X_LANES), F32),
    )
    out_specs = (
        x_spec,
        pl.BlockSpec((ts, D), lambda i: (routed(i), 0)),
        pl.BlockSpec((V7X_SUBLANES, ts), lambda i: (0, routed(i))),
        pl.BlockSpec((V7X_SUBLANES, ts), lambda i: (0, routed(i))),
        pl.BlockSpec((ts // MOE_TS, N_EXPERTS, V7X_LANES), lambda i: (routed(i), 0, 0)),
    )
    return pl.pallas_call(
        functools.partial(_mixer_kernel, n_seq=n_s),
        grid=(n_tiles + 1,),
        in_specs=in_specs,
        out_specs=out_specs,
        out_shape=out_shape,
        scratch_shapes=[pltpu.VMEM((HALO + ts, BRANCH_WIDTH), F32),
                        pltpu.VMEM((HALO + ts, BRANCH_WIDTH), F32),
                        pltpu.VMEM((V7X_SUBLANES - 1, HALO + ts - V7X_SUBLANES, BRANCH_WIDTH), F32),
                        pltpu.VMEM((ts, SIDE_WIDTH), F32),
                        pltpu.VMEM((ts, D), F32),
                        pltpu.VMEM((ts, ts), BF16),
                        pltpu.VMEM((ts, D), BF16)],
        compiler_params=pltpu.CompilerParams(
            dimension_semantics=("arbitrary",),
            vmem_limit_bytes=V7X_VMEM_BYTES - 8 * 1024 * 1024),
        name="mixer",
    )(x, x, *consts)


def _row_span(first_row, n_rows):
    return pl.ds(pl.multiple_of(first_row * ROW_SUB, V7X_SUBLANES),
                 pl.multiple_of(n_rows * ROW_SUB, V7X_SUBLANES))


def _dispatch_kernel(run_src, run_n, run_dst, tile_rows, pad_dst, pad_n, n_used_ref, h2b_ref, pos_ref,
                     xs_hbm, stage, zero_rows, run_sem, fill_sem, *, n_blocks):
    i = pl.program_id(0)
    n_tiles = pl.num_programs(0)
    slot = i % 2
    block_rows = MOE_BLOCK * ROW_SUB

    def run_copy(tile, e, buf, live=1):
        k = tile * N_EXPERTS + e
        n = run_n[k] * live
        return pltpu.make_async_copy(stage.at[buf, _row_span(run_src[k], n)],
                                     xs_hbm.at[_row_span(run_dst[k], n)], run_sem.at[buf])

    def wait_runs(tile, buf):
        n = tile_rows[tile]
        pltpu.make_async_copy(stage.at[buf, _row_span(0, n)], xs_hbm.at[_row_span(0, n)],
                              run_sem.at[buf]).wait()

    def pad_copy(e):
        return pltpu.make_async_copy(zero_rows.at[_row_span(0, pad_n[e])],
                                     xs_hbm.at[_row_span(pad_dst[e], pad_n[e])], fill_sem)

    def idle_block_copy(blk):
        return pltpu.make_async_copy(
            zero_rows, xs_hbm.at[pl.ds(pl.multiple_of(blk * block_rows, block_rows), block_rows)], fill_sem)

    @pl.when(i == 0)
    def _():
        zero_rows[...] = jnp.zeros_like(zero_rows)

        def start_idle(blk, c):
            idle_block_copy(blk).start()
            return c

        def wait_idle(blk, c):
            idle_block_copy(blk).wait()
            return c

        for e in range(N_EXPERTS):
            pad_copy(e).start()
        lax.fori_loop(n_used_ref[0], n_blocks, start_idle, 0)
        for e in range(N_EXPERTS):
            pad_copy(e).wait()
        lax.fori_loop(n_used_ref[0], n_blocks, wait_idle, 0)

    @pl.when(i >= 2)
    def _():
        wait_runs(i - 2, slot)

    prev_live = jnp.minimum(i, 1)
    for e in range(N_EXPERTS):
        run_copy(jnp.maximum(i - 1, 0), e, 1 - slot, prev_live).start()

    h2b = h2b_ref[...]
    for a0 in range(0, N_STAGE, STAGE_CHUNK):
        a_iota = lax.broadcasted_iota(I32, (STAGE_CHUNK, MOE_TS), 0) + a0
        onehot = jnp.zeros((STAGE_CHUNK, MOE_TS), F32)
        for k in range(TOP_K):
            onehot = jnp.where(a_iota == pos_ref[k:k + 1, :], 1.0, onehot)
        _store_rows(stage.at[slot], a0, _pack_rows(_dot(onehot.astype(BF16), h2b)))

    @pl.when(i == n_tiles - 1)
    def _():
        for e in range(N_EXPERTS):
            run_copy(i, e, slot).start()

        @pl.when(i >= 1)
        def _():
            wait_runs(i - 1, 1 - slot)
        wait_runs(i, slot)


def _dispatch_call(h2b, pos8, tables, n_blocks):
    T, D = h2b.shape
    n_tiles = T // MOE_TS
    grid_spec = pltpu.PrefetchScalarGridSpec(
        num_scalar_prefetch=7,
        grid=(n_tiles,),
        in_specs=[pl.BlockSpec((MOE_TS, D), lambda i, *_: (i, 0)),
                  pl.BlockSpec((V7X_SUBLANES, MOE_TS), lambda i, *_: (0, i))],
        out_specs=pl.BlockSpec(memory_space=pl.ANY),
        scratch_shapes=[pltpu.VMEM((2, N_STAGE * ROW_SUB, V7X_LANES), U32),
                        pltpu.VMEM((MOE_BLOCK * ROW_SUB, V7X_LANES), U32),
                        pltpu.SemaphoreType.DMA((2,)),
                        pltpu.SemaphoreType.DMA(())],
    )
    return pl.pallas_call(
        functools.partial(_dispatch_kernel, n_blocks=n_blocks),
        grid_spec=grid_spec,
        out_shape=jax.ShapeDtypeStruct((n_blocks * MOE_BLOCK * ROW_SUB, V7X_LANES), U32),
        compiler_params=pltpu.CompilerParams(
            dimension_semantics=("arbitrary",),
            vmem_limit_bytes=V7X_VMEM_BYTES - 16 * 1024 * 1024),
        name="dispatch",
    )(tables["run_src"], tables["run_n"], tables["run_dst"], tables["tile_rows"], tables["pad_dst"],
      tables["pad_n"], tables["n_used"], h2b, pos8)


def _moe_kernel(be_ref, group_end_ref, n_used_ref, xs_ref, wup_hbm, wdn_hbm, *rest, layer):
    bias_refs = rest[:2 * MOE_STEP_BLOCKS]
    ys_ref, wup_f32, wdn_f32, wup_bf, wdn_bf, group_count, wsem = rest[2 * MOE_STEP_BLOCKS:]
    step = pl.program_id(0)
    n_used = n_used_ref[0]

    def weight_copies(e, buf):
        return (pltpu.make_async_copy(wup_hbm.at[layer, e], wup_f32.at[buf], wsem.at[buf, 0]),
                pltpu.make_async_copy(wdn_hbm.at[layer, e], wdn_f32.at[buf], wsem.at[buf, 1]))

    @pl.when(step == 0)
    def _():
        group_count[0] = 0
        for c in weight_copies(be_ref[0], 0):
            c.start()

    halves = []
    for half in range(MOE_STEP_BLOCKS):
        blk = step * MOE_STEP_BLOCKS + half
        expert = be_ref[blk]
        new_expert = (blk == 0) | (expert != be_ref[jnp.maximum(blk - 1, 0)])

        @pl.when((blk < n_used) & new_expert)
        def _(expert=expert):
            par = group_count[0] % 2
            group_count[0] = group_count[0] + 1
            next_blk = group_end_ref[expert]

            @pl.when(next_blk < n_used)
            def _():
                for c in weight_copies(be_ref[next_blk], 1 - par):
                    c.start()

            for c in weight_copies(expert, par):
                c.wait()
            wup_bf[par] = wup_f32[par].astype(BF16)
            wdn_bf[par] = wdn_f32[par].astype(BF16)

        halves.append((group_count[0] - 1) % 2)

    @pl.when(step * MOE_STEP_BLOCKS < n_used)
    def _():
        for half, par in enumerate(halves):
            bup_ref, bdn_ref = bias_refs[2 * half], bias_refs[2 * half + 1]
            r0 = half * MOE_BLOCK
            xs = _unpack_rows(_load_rows(xs_ref, r0, MOE_BLOCK))
            gu = _dot(xs, wup_bf[par]) + bup_ref[0, 0]
            x_glu = jnp.minimum(gu[:, :D_FF], SWIGLU_LIMIT)
            x_lin = jnp.clip(gu[:, D_FF:], -SWIGLU_LIMIT, SWIGLU_LIMIT)
            act = x_glu * _sigmoid(SWIGLU_ALPHA * x_glu) * (x_lin + 1.0)
            y = _dot(act.astype(BF16), wdn_bf[par]) + bdn_ref[0, 0]
            _store_rows(ys_ref, r0, _pack_rows(y.astype(BF16).astype(F32)))

    @pl.when(step * MOE_STEP_BLOCKS >= n_used)
    def _():
        ys_ref[...] = jnp.zeros_like(ys_ref)


def _moe_call(layer, xs, tables, w_up, b_up, w_down, b_down):
    n_blocks = tables["block_e"].shape[0]
    assert n_blocks % MOE_STEP_BLOCKS == 0
    step_rows = MOE_STEP_BLOCKS * MOE_BLOCK * ROW_SUB
    bias_spec = lambda width, half: pl.BlockSpec(
        (1, 1, 1, width), lambda s, be, *_: (layer, be[s * MOE_STEP_BLOCKS + half], 0, 0))
    grid_spec = pltpu.PrefetchScalarGridSpec(
        num_scalar_prefetch=3,
        grid=(n_blocks // MOE_STEP_BLOCKS,),
        in_specs=[
            pl.BlockSpec((step_rows, V7X_LANES), lambda s, *_: (s, 0)),
            pl.BlockSpec(memory_space=pl.ANY),
            pl.BlockSpec(memory_space=pl.ANY),
        ] + [bias_spec(width, half) for half in range(MOE_STEP_BLOCKS) for width in (2 * D_FF, D_MODEL)],
        out_specs=pl.BlockSpec((step_rows, V7X_LANES), lambda s, *_: (s, 0)),
        scratch_shapes=[pltpu.VMEM((2, D_MODEL, 2 * D_FF), F32), pltpu.VMEM((2, D_FF, D_MODEL), F32),
                        pltpu.VMEM((2, D_MODEL, 2 * D_FF), BF16), pltpu.VMEM((2, D_FF, D_MODEL), BF16),
                        pltpu.SMEM((1,), I32), pltpu.SemaphoreType.DMA((2, 2))],
    )
    return pl.pallas_call(
        functools.partial(_moe_kernel, layer=layer),
        grid_spec=grid_spec,
        out_shape=jax.ShapeDtypeStruct(xs.shape, U32),
        compiler_params=pltpu.CompilerParams(
            dimension_semantics=("arbitrary",),
            vmem_limit_bytes=V7X_VMEM_BYTES - 8 * 1024 * 1024),
        name="moe",
    )(tables["block_e"], tables["group_end"], tables["n_used"], xs, w_up, w_down,
      *([b_up, b_down] * MOE_STEP_BLOCKS))


def _combine_kernel(run_src, run_n, run_dst, tile_rows, x1_ref, pos_ref, gate_ref, p_ref, ple_g_ref,
                    gate_w_ref, proj_w_ref, fin_g_ref, ys_hbm, out_ref, stage, run_sem, *, last):
    i = pl.program_id(0)
    n_tiles = pl.num_programs(0)
    slot = i % 2

    def run_copy(tile, e, buf, live=1):
        k = tile * N_EXPERTS + e
        n = run_n[k] * live
        return pltpu.make_async_copy(ys_hbm.at[_row_span(run_dst[k], n)],
                                     stage.at[buf, _row_span(run_src[k], n)], run_sem.at[buf])

    @pl.when(i == 0)
    def _():
        stage[...] = jnp.zeros_like(stage)
        for e in range(N_EXPERTS):
            run_copy(0, e, 0).start()

    n_rows = tile_rows[i]
    pltpu.make_async_copy(ys_hbm.at[_row_span(0, n_rows)], stage.at[slot, _row_span(0, n_rows)],
                          run_sem.at[slot]).wait()

    next_live = jnp.where(i + 1 < n_tiles, 1, 0)
    for e in range(N_EXPERTS):
        run_copy(jnp.minimum(i + 1, n_tiles - 1), e, 1 - slot, next_live).start()

    x2 = x1_ref[...]
    for a0 in range(0, N_STAGE, STAGE_CHUNK):
        a_iota = lax.broadcasted_iota(I32, (MOE_TS, STAGE_CHUNK), 1) + a0
        weights = jnp.zeros((MOE_TS, STAGE_CHUNK), F32)
        for k in range(TOP_K):
            weights = jnp.where(a_iota == pos_ref[:, k:k + 1], gate_ref[:, k:k + 1], weights)
        y_sorted = _unpack_rows(_load_rows(stage.at[slot], a0, STAGE_CHUNK))
        x2 = x2 + _dot(weights.astype(BF16), y_sorted)
    h3 = _rms_norm(x2, ple_g_ref[...]).astype(BF16)
    g = _sigmoid(_dot(h3, gate_w_ref[...]))
    pp = _dot(p_ref[0, 0].astype(BF16), proj_w_ref[...])
    x3 = x2 + g * pp
    if last:
        x3 = _rms_norm(x3, fin_g_ref[...])
    out_ref[...] = x3


def _combine_call(layer, x1, ys, pos_tm, gate_tm, p, tables, lw, fin_g, last):
    T, D = x1.shape
    n_tiles = T // MOE_TS
    n_seq = p.shape[2] // MOE_TS
    consts = [lw["ple_norm"], lw["ple_gate_w"], lw["ple_proj_w"], fin_g]
    grid_spec = pltpu.PrefetchScalarGridSpec(
        num_scalar_prefetch=4,
        grid=(n_tiles,),
        in_specs=[pl.BlockSpec((MOE_TS, D), lambda i, *_: (i, 0)),
                  pl.BlockSpec((MOE_TS, V7X_SUBLANES), lambda i, *_: (i, 0)),
                  pl.BlockSpec((MOE_TS, V7X_SUBLANES), lambda i, *_: (i, 0)),
                  pl.BlockSpec((1, 1, MOE_TS, PLE_DIM), lambda i, *_: (layer, i // n_seq, i % n_seq, 0))]
        + [_const_spec(c.shape) for c in consts] + [pl.BlockSpec(memory_space=pl.ANY)],
        out_specs=pl.BlockSpec((MOE_TS, D), lambda i, *_: (i, 0)),
        scratch_shapes=[pltpu.VMEM((2, N_STAGE * ROW_SUB, V7X_LANES), U32),
                        pltpu.SemaphoreType.DMA((2,))],
    )
    return pl.pallas_call(
        functools.partial(_combine_kernel, last=last),
        grid_spec=grid_spec,
        out_shape=jax.ShapeDtypeStruct((T, D), F32),
        compiler_params=pltpu.CompilerParams(
            dimension_semantics=("arbitrary",),
            vmem_limit_bytes=V7X_VMEM_BYTES - 16 * 1024 * 1024),
        name="combine",
    )(tables["run_src"], tables["run_n"], tables["run_dst"], tables["tile_rows"], x1, pos_tm, gate_tm, p,
      *consts, ys)


def _routing_tables(tile_counts, n_blocks):
    c = tile_counts[:, :, 0].astype(I32)
    c = (c + RUN_ALIGN - 1) // RUN_ALIGN * RUN_ALIGN
    counts = jnp.sum(c, axis=0)
    padded = (counts + MOE_BLOCK - 1) // MOE_BLOCK * MOE_BLOCK
    pad_end = jnp.cumsum(padded)
    pad_start = pad_end - padded
    run_dst = pad_start[None, :] + jnp.cumsum(c, axis=0) - c
    run_src = jnp.cumsum(c, axis=1) - c
    n_used = pad_end[-1] // MOE_BLOCK
    blk0 = jnp.arange(n_blocks, dtype=I32) * MOE_BLOCK
    be = jnp.minimum(jnp.sum(blk0[:, None] >= pad_end[None, :], axis=1), N_EXPERTS - 1)
    be = be[jnp.minimum(jnp.arange(n_blocks), n_used - 1)]
    return {
        "group_end": (pad_end // MOE_BLOCK).astype(I32),
        "run_src": run_src.reshape(-1).astype(I32), "run_n": c.reshape(-1),
        "run_dst": run_dst.reshape(-1).astype(I32), "tile_rows": jnp.sum(c, axis=1).astype(I32),
        "pad_dst": (pad_start + counts).astype(I32), "pad_n": (padded - counts).astype(I32),
        "n_used": n_used.reshape(1).astype(I32), "block_e": be.astype(I32),
    }


def kernel(x, p, mix_norm, w_in, b_in, pool_w, pool_scale, conv_w, conv_b, conv_norm_g, conv_norm_b,
           sgu_norm_g, sgu_norm_b, sgu_w, sgu_b, branch_w, branch_b, w_out, moe_norm, router_w,
           router_b, expert_w_up, expert_b_up, expert_w_down, expert_b_down, ple_norm, ple_gate_w,
           ple_proj_w, final_norm):
    B, S, D = x.shape
    T = B * S
    depth = w_in.shape[0]
    assert D == D_MODEL and S % MIX_TS == 0 and w_in.shape[2] == IN_WIDTH
    max_rows = T * TOP_K + (T // MOE_TS) * N_EXPERTS * (RUN_ALIGN - 1)
    n_blocks = -(-max_rows // MOE_BLOCK) + N_EXPERTS
    n_blocks += -n_blocks % MOE_STEP_BLOCKS
    row = lambda a: a.reshape(1, -1)
    b_up4 = expert_b_up[:, :, None, :]
    b_down4 = expert_b_down[:, :, None, :]
    for i in range(depth):
        lw = {
            "mix_norm": row(mix_norm[i]), "w_in": w_in[i].astype(BF16), "b_in": row(b_in[i]),
            "pool_w": pool_w[i].astype(BF16), "pool_scale": row(pool_scale[i]),
            "conv_w": conv_w[i], "conv_b": row(conv_b[i]),
            "conv_norm_g": row(conv_norm_g[i]), "conv_norm_b": row(conv_norm_b[i]),
            "sgu_norm_g": row(sgu_norm_g[i]), "sgu_norm_b": row(sgu_norm_b[i]),
            "sgu_w": sgu_w[i], "sgu_bt": sgu_b[i].T,
            "branch_w": branch_w[i].astype(BF16), "branch_b": branch_b[i],
            "w_out": w_out[i].astype(BF16), "moe_norm": row(moe_norm[i]),
            "router_wt": router_w[i].T, "router_b": router_b[i].reshape(-1, 1),
            "ple_norm": row(ple_norm[i]), "ple_gate_w": ple_gate_w[i].astype(BF16),
            "ple_proj_w": ple_proj_w[i].astype(BF16),
        }
        x1, h2b, pos8, gate8, tile_counts = _mixer_call(x, lw)
        tables = _routing_tables(tile_counts, n_blocks)
        xs = _dispatch_call(h2b, pos8, tables, n_blocks)
        ys = _moe_call(i, xs, tables, expert_w_up, b_up4, expert_w_down, b_down4)
        x = _combine_call(i, x1.reshape(T, D), ys, pos8.T, gate8.T, p, tables, lw,
                          row(final_norm), last=(i == depth - 1)).reshape(B, S, D)
    return x
```

```python
import functools

import jax
import jax.numpy as jnp
from jax import lax
from jax.experimental import pallas as pl
from jax.experimental.pallas import tpu as pltpu

F32 = jnp.float32
BF16 = jnp.bfloat16
I32 = jnp.int32
U32 = jnp.uint32

D_MODEL = 1024
POOL_WINDOWS = (2, 4, 8, 16)
POOL_CH = 128
BRANCH_WIDTH = 512
CONV_K = 31
SGU_CHUNK = 128
SGU_HEADS = 4
N_BRANCH = 3
N_EXPERTS = 32
TOP_K = 4
D_FF = 1024
SWIGLU_LIMIT = 7.0
SWIGLU_ALPHA = 1.702
MOE_BLOCK = 256
MOE_STEP_BLOCKS = 2
assert MOE_STEP_BLOCKS <= 2
PLE_DIM = 256
EPS = 1e-6

V7X_SUBLANES = 8
V7X_LANES = 128
ROW_WORDS = D_MODEL // 2
ROW_SUB = ROW_WORDS // V7X_LANES
RUN_ALIGN = V7X_SUBLANES // ROW_SUB
V7X_VMEM_BYTES = 64 * 1024 * 1024

MIX_TS = 512
HALO = 32
CONV_ROWS = 32
SIDE_CHUNK = 256
MOE_TS = 256
N_STAGE = TOP_K * MOE_TS + N_EXPERTS * (RUN_ALIGN - 1)
STAGE_CHUNK = N_STAGE // 3
assert MIX_TS % MOE_TS == 0 and STAGE_CHUNK * 3 == N_STAGE and STAGE_CHUNK % V7X_SUBLANES == 0

C_POOL = 0
C_CONV = C_POOL + BRANCH_WIDTH
C_SGU_U = C_CONV + 2 * BRANCH_WIDTH
C_SGU_V = C_SGU_U + BRANCH_WIDTH
C_GATE = C_SGU_V + BRANCH_WIDTH
IN_WIDTH = C_GATE + N_BRANCH * D_MODEL
S_POOL = 0
S_SGU_U = S_POOL + BRANCH_WIDTH
S_SGU_V = S_SGU_U + BRANCH_WIDTH
S_GATE = S_SGU_V + BRANCH_WIDTH
SIDE_WIDTH = S_GATE + N_BRANCH * D_MODEL
assert S_SGU_U % SIDE_CHUNK == 0 and SIDE_WIDTH % SIDE_CHUNK == 0


def _rms_norm(x, g):
    return x * lax.rsqrt(jnp.mean(x * x, axis=-1, keepdims=True) + EPS) * g


def _layer_norm(x, g, b):
    mu = jnp.mean(x, axis=-1, keepdims=True)
    xc = x - mu
    var = jnp.mean(xc * xc, axis=-1, keepdims=True)
    return xc * lax.rsqrt(var + EPS) * g + b


def _sigmoid(x):
    return 0.5 * jnp.tanh(0.5 * x) + 0.5


def _dot(a, b):
    return jnp.dot(a, b, preferred_element_type=F32)


def _pack_rows(v):
    bits = lax.bitcast_convert_type(v, U32)
    return (bits[:, :ROW_WORDS] >> 16) | (bits[:, ROW_WORDS:] & jnp.uint32(0xFFFF0000))


def _unpack_rows(w):
    low = lax.bitcast_convert_type(w << 16, F32)
    high = lax.bitcast_convert_type(w & jnp.uint32(0xFFFF0000), F32)
    return jnp.concatenate([low, high], axis=-1).astype(BF16)


def _store_rows(ref_2d, first_row, words):
    n = words.shape[0]
    for j in range(ROW_SUB):
        ref_2d[pl.ds(first_row * ROW_SUB + j, n, stride=ROW_SUB), :] = words[:, j * V7X_LANES:(j + 1) * V7X_LANES]


def _load_rows(ref_2d, first_row, n):
    return jnp.concatenate(
        [ref_2d[pl.ds(first_row * ROW_SUB + j, n, stride=ROW_SUB), :] for j in range(ROW_SUB)], axis=-1)


def _mixer_kernel(x_ref, x_next_ref, mix_g_ref, w_in_ref, b_in_ref, pool_w_ref, pool_scale_ref, conv_w_ref,
                  conv_b_ref, cn_g_ref, cn_b_ref, sn_g_ref, sn_b_ref, sgu_w_ref, sgu_bt_ref,
                  branch_w_ref, branch_b_ref, w_out_ref, moe_g_ref, rw_t_ref, rb_ref,
                  x1_ref, h2b_ref, pos_ref, gate_ref, counts_ref,
                  pool_hist, conv_hist, conv_shift, side_buf, x1_prev, before_buf, h_buf, *, n_seq):
    i = pl.program_id(0)
    n_tiles = pl.num_programs(0) - 1
    route_refs = (before_buf, moe_g_ref, rw_t_ref, rb_ref, h2b_ref, pos_ref, gate_ref, counts_ref)

    @pl.when(i == 0)
    def _():
        x1_prev[...] = jnp.zeros_like(x1_prev)
        h_buf[...] = _rms_norm(x_ref[0], mix_g_ref[...]).astype(BF16)
        t_row = lax.broadcasted_iota(I32, (MIX_TS, MIX_TS), 0)
        t_col = lax.broadcasted_iota(I32, (MIX_TS, MIX_TS), 1)
        same_tile = t_row // MOE_TS == t_col // MOE_TS
        before_buf[...] = jnp.where(same_tile, jnp.where(t_row < t_col, 1.0, 0.0), 0.0).astype(BF16)

    @pl.when(i % n_seq == 0)
    def _():
        pool_hist[0:HALO, :] = jnp.zeros((HALO, BRANCH_WIDTH), F32)
        conv_hist[0:HALO, :] = jnp.zeros((HALO, BRANCH_WIDTH), F32)

    @pl.when(i < n_tiles)
    def _():
        _mix_tile(i % n_seq, x_ref, x_next_ref, mix_g_ref, w_in_ref, b_in_ref, pool_w_ref, pool_scale_ref,
                  conv_w_ref, conv_b_ref, cn_g_ref, cn_b_ref, sn_g_ref, sn_b_ref, sgu_w_ref, sgu_bt_ref,
                  branch_w_ref, branch_b_ref, w_out_ref, x1_ref, pool_hist, conv_hist, conv_shift, side_buf,
                  x1_prev, h_buf, functools.partial(_route_tile, x1_prev, *route_refs))

    @pl.when(i == n_tiles)
    def _():
        _route_tile(x1_prev, *route_refs)


def _mix_tile(s, x_ref, x_next_ref, mix_g_ref, w_in_ref, b_in_ref, pool_w_ref, pool_scale_ref, conv_w_ref,
              conv_b_ref, cn_g_ref, cn_b_ref, sn_g_ref, sn_b_ref, sgu_w_ref, sgu_bt_ref, branch_w_ref,
              branch_b_ref, w_out_ref, x1_ref, pool_hist, conv_hist, conv_shift, side_buf, x1_prev, h_buf,
              route_previous):
    ts = MIX_TS
    x = x_ref[0]
    h = h_buf[...]

    def in_proj(c0, width):
        return _dot(h, w_in_ref[:, c0:c0 + width]) + b_in_ref[:, c0:c0 + width]

    zb = in_proj(C_CONV, 2 * BRANCH_WIDTH)
    conv_hist[HALO:HALO + ts, :] = zb[:, :BRANCH_WIDTH] * _sigmoid(zb[:, BRANCH_WIDTH:])
    n_shift_rows = HALO + ts - V7X_SUBLANES
    for sft in range(1, V7X_SUBLANES):
        conv_shift[sft - 1] = conv_hist[sft:sft + n_shift_rows, :]
    yb_parts = []
    n_conv_blocks = ts // CONV_ROWS
    n_side_chunks = SIDE_WIDTH // SIDE_CHUNK
    for bi in range(n_conv_blocks):
        r0 = bi * CONV_ROWS
        for ci in range(bi * n_side_chunks // n_conv_blocks, (bi + 1) * n_side_chunks // n_conv_blocks):
            d0 = ci * SIDE_CHUNK
            c0 = d0 if d0 < S_SGU_U else d0 + (C_SGU_U - S_SGU_U)
            side_buf[:, d0:d0 + SIDE_CHUNK] = in_proj(c0, SIDE_CHUNK)
        acc = jnp.zeros((CONV_ROWS, BRANCH_WIDTH), F32) + conv_b_ref[...]
        for k in range(CONV_K):
            off = HALO - (CONV_K - 1) + k + r0
            base, sft = off - off % V7X_SUBLANES, off % V7X_SUBLANES
            if sft == 0:
                window = conv_hist[base:base + CONV_ROWS, :]
            else:
                window = conv_shift[sft - 1, base:base + CONV_ROWS, :]
            acc = acc + conv_w_ref[k:k + 1, :] * window
        yb_rows = _layer_norm(acc, cn_g_ref[...], cn_b_ref[...])
        yb_parts.append((yb_rows * _sigmoid(yb_rows)).astype(BF16))
    yb = jnp.concatenate(yb_parts, axis=0)
    conv_hist[0:HALO, :] = conv_hist[ts:ts + HALO, :]

    za = side_buf[:, S_POOL:S_POOL + BRANCH_WIDTH]
    pool_hist[HALO:HALO + ts, :] = za
    row = lax.broadcasted_iota(I32, (ts, 1), 0) + s * ts
    mixed = []
    for g, w in enumerate(POOL_WINDOWS):
        c0 = g * POOL_CH
        cur = za[:, c0:c0 + POOL_CH]
        acc = cur
        for j in range(1, w):
            acc = acc + pool_hist[HALO - j:HALO - j + ts, c0:c0 + POOL_CH]
        count = jnp.minimum(row + 1, w).astype(F32)
        pooled = (acc / count - cur).astype(BF16)
        mixed.append(_dot(pooled, pool_w_ref[g]))
    ya = (jnp.concatenate(mixed, axis=-1) * pool_scale_ref[...]).astype(BF16)
    pool_hist[0:HALO, :] = pool_hist[ts:ts + HALO, :]

    zu = side_buf[:, S_SGU_U:S_SGU_U + BRANCH_WIDTH]
    zv = side_buf[:, S_SGU_V:S_SGU_V + BRANCH_WIDTH]
    v = _layer_norm(zv, sn_g_ref[...], sn_b_ref[...]).astype(BF16)
    tri = (lax.broadcasted_iota(I32, (SGU_CHUNK, SGU_CHUNK), 0)
           >= lax.broadcasted_iota(I32, (SGU_CHUNK, SGU_CHUNK), 1))
    w_tri = [jnp.where(tri, sgu_w_ref[hd], 0.0).astype(BF16) for hd in range(SGU_HEADS)]
    chunks = []
    for c in range(ts // SGU_CHUNK):
        heads = []
        for hd in range(SGU_HEADS):
            vv = v[c * SGU_CHUNK:(c + 1) * SGU_CHUNK, hd * 128:(hd + 1) * 128]
            heads.append(_dot(w_tri[hd], vv) + sgu_bt_ref[:, hd:hd + 1])
        chunks.append(jnp.concatenate(heads, axis=-1))
    yc = (zu * jnp.concatenate(chunks, axis=0)).astype(BF16)

    route_previous()
    h_buf[...] = _rms_norm(x_next_ref[0], mix_g_ref[...]).astype(BF16)

    merged = jnp.zeros((ts, D_MODEL), F32)
    for k, yk in enumerate((ya, yb, yc)):
        proj = _dot(yk, branch_w_ref[k]) + branch_b_ref[k:k + 1, :]
        g0 = S_GATE + k * D_MODEL
        merged = merged + _sigmoid(side_buf[:, g0:g0 + D_MODEL]) * proj
    x1 = x + _dot(merged.astype(BF16), w_out_ref[...])
    x1_ref[0] = x1
    x1_prev[...] = x1


def _route_tile(x1_ref, before_ref, moe_g_ref, rw_t_ref, rb_ref, h2b_ref, pos_ref, gate_ref, counts_ref):
    ts = MIX_TS
    h2 = _rms_norm(x1_ref[...], moe_g_ref[...])
    h2_hi = h2.astype(BF16)
    h2b_ref[...] = h2_hi
    h2_lo = (h2 - h2_hi.astype(F32)).astype(BF16)
    rw = rw_t_ref[...]
    rw_hi = rw.astype(BF16)
    rw_lo = (rw - rw_hi.astype(F32)).astype(BF16)
    nt = (((1,), (1,)), ((), ()))
    logits = (lax.dot_general(rw_hi, h2_hi, nt, preferred_element_type=F32)
              + lax.dot_general(rw_hi, h2_lo, nt, preferred_element_type=F32)
              + lax.dot_general(rw_lo, h2_hi, nt, preferred_element_type=F32)) + rb_ref[...]
    e_iota = lax.broadcasted_iota(I32, (N_EXPERTS, ts), 0).astype(F32)
    vals = logits
    top_v, sels = [], []
    for _k in range(TOP_K):
        m = jnp.max(vals, axis=0, keepdims=True)
        idx = jnp.min(jnp.where(vals == m, e_iota, float(N_EXPERTS)), axis=0, keepdims=True)
        sel = e_iota == idx
        vals = jnp.where(sel, -jnp.inf, vals)
        top_v.append(m)
        sels.append(sel)
    exps = [jnp.exp(tv - top_v[0]) for tv in top_v]
    denom = exps[0] + exps[1] + exps[2] + exps[3]
    chosen = jnp.zeros((N_EXPERTS, ts), F32)
    for sel in sels:
        chosen = chosen + jnp.where(sel, 1.0, 0.0)
    chosen_b = chosen.astype(BF16)
    prefix = _dot(chosen_b, before_ref[...])
    lower = jnp.where(lax.broadcasted_iota(I32, (N_EXPERTS, N_EXPERTS), 0)
                      > lax.broadcasted_iota(I32, (N_EXPERTS, N_EXPERTS), 1), 1.0, 0.0).astype(BF16)
    base_parts = []
    for j in range(ts // MOE_TS):
        lanes = slice(j * MOE_TS, (j + 1) * MOE_TS)
        count = jnp.sum(chosen[:, lanes], axis=1, keepdims=True)
        run_rows = jnp.floor((count + (RUN_ALIGN - 1)) * (1.0 / RUN_ALIGN)) * RUN_ALIGN
        run_start = _dot(lower, jnp.broadcast_to(run_rows, (N_EXPERTS, V7X_LANES)).astype(BF16))[:, 0:1]
        base_parts.append(prefix[:, lanes] + run_start)
        counts_ref[j] = jnp.broadcast_to(count, (N_EXPERTS, V7X_LANES))
    base = jnp.concatenate(base_parts, axis=1)
    zeros4 = jnp.zeros((V7X_SUBLANES - TOP_K, ts), F32)
    pos = [jnp.sum(jnp.where(sel, base, 0.0), axis=0, keepdims=True) for sel in sels]
    pos_ref[...] = jnp.concatenate(pos + [zeros4], axis=0).astype(I32)
    gate_ref[...] = jnp.concatenate([e / denom for e in exps] + [zeros4], axis=0)


def _const_spec(shape, layer):
    nd = len(shape)
    return pl.BlockSpec((None,) + tuple(shape[1:]), lambda *_: (layer,) + (0,) * (nd - 1),
                        pipeline_mode=pl.Buffered(1))


def _mixer_call(layer, x, lw):
    B, S, D = x.shape
    ts = MIX_TS
    n_s = S // ts
    T = B * S
    n_tiles = B * n_s
    mixed = lambda i: jnp.minimum(i, n_tiles - 1)
    routed = lambda i: jnp.maximum(i - 1, 0)
    consts = [lw["mix_norm"], lw["w_in"], lw["b_in"], lw["pool_w"], lw["pool_scale"], lw["conv_w"],
              lw["conv_b"], lw["conv_norm_g"], lw["conv_norm_b"], lw["sgu_norm_g"], lw["sgu_norm_b"],
              lw["sgu_w"], lw["sgu_bt"], lw["branch_w"], lw["branch_b"], lw["w_out"], lw["moe_norm"],
              lw["router_wt"], lw["router_b"]]
    x_spec = pl.BlockSpec((1, ts, D), lambda i: (mixed(i) // n_s, mixed(i) % n_s, 0))
    x_next_spec = pl.BlockSpec((1, ts, D), lambda i: (mixed(i + 1) // n_s, mixed(i + 1) % n_s, 0))
    in_specs = [x_spec, x_next_spec] + [_const_spec(c.shape, layer) for c in consts]
    out_shape = (
        jax.ShapeDtypeStruct((B, S, D), F32),
        jax.ShapeDtypeStruct((T, D), BF16),
        jax.ShapeDtypeStruct((V7X_SUBLANES, T), I32),
        jax.ShapeDtypeStruct((V7X_SUBLANES, T), F32),
        jax.ShapeDtypeStruct((T // MOE_TS, N_EXPERTS, V7X_LANES), F32),
    )
    out_specs = (
        x_spec,
        pl.BlockSpec((ts, D), lambda i: (routed(i), 0)),
        pl.BlockSpec((V7X_SUBLANES, ts), lambda i: (0, routed(i))),
        pl.BlockSpec((V7X_SUBLANES, ts), lambda i: (0, routed(i))),
        pl.BlockSpec((ts // MOE_TS, N_EXPERTS, V7X_LANES), lambda i: (routed(i), 0, 0)),
    )
    return pl.pallas_call(
        functools.partial(_mixer_kernel, n_seq=n_s),
        grid=(n_tiles + 1,),
        in_specs=in_specs,
        out_specs=out_specs,
        out_shape=out_shape,
        scratch_shapes=[pltpu.VMEM((HALO + ts, BRANCH_WIDTH), F32),
                        pltpu.VMEM((HALO + ts, BRANCH_WIDTH), F32),
                        pltpu.VMEM((V7X_SUBLANES - 1, HALO + ts - V7X_SUBLANES, BRANCH_WIDTH), F32),
                        pltpu.VMEM((ts, SIDE_WIDTH), F32),
                        pltpu.VMEM((ts, D), F32),
                        pltpu.VMEM((ts, ts), BF16),
                        pltpu.VMEM((ts, D), BF16)],
        compiler_params=pltpu.CompilerParams(
            dimension_semantics=("arbitrary",),
            vmem_limit_bytes=V7X_VMEM_BYTES - 8 * 1024 * 1024),
        name="mixer",
    )(x, x, *consts)


def _row_span(first_row, n_rows):
    return pl.ds(pl.multiple_of(first_row * ROW_SUB, V7X_SUBLANES),
                 pl.multiple_of(n_rows * ROW_SUB, V7X_SUBLANES))


def _dispatch_kernel(run_src, run_n, run_dst, tile_rows, pad_dst, pad_n, n_used_ref, h2b_ref, pos_ref,
                     xs_hbm, stage, zero_rows, run_sem, fill_sem, *, n_blocks):
    i = pl.program_id(0)
    n_tiles = pl.num_programs(0)
    slot = i % 2
    block_rows = MOE_BLOCK * ROW_SUB

    def run_copy(tile, e, buf, live=1):
        k = tile * N_EXPERTS + e
        n = run_n[k] * live
        return pltpu.make_async_copy(stage.at[buf, _row_span(run_src[k], n)],
                                     xs_hbm.at[_row_span(run_dst[k], n)], run_sem.at[buf])

    def wait_runs(tile, buf):
        n = tile_rows[tile]
        pltpu.make_async_copy(stage.at[buf, _row_span(0, n)], xs_hbm.at[_row_span(0, n)],
                              run_sem.at[buf]).wait()

    def pad_copy(e):
        return pltpu.make_async_copy(zero_rows.at[_row_span(0, pad_n[e])],
                                     xs_hbm.at[_row_span(pad_dst[e], pad_n[e])], fill_sem)

    def idle_block_copy(blk):
        return pltpu.make_async_copy(
            zero_rows, xs_hbm.at[pl.ds(pl.multiple_of(blk * block_rows, block_rows), block_rows)], fill_sem)

    @pl.when(i == 0)
    def _():
        zero_rows[...] = jnp.zeros_like(zero_rows)

        def start_idle(blk, c):
            idle_block_copy(blk).start()
            return c

        def wait_idle(blk, c):
            idle_block_copy(blk).wait()
            return c

        for e in range(N_EXPERTS):
            pad_copy(e).start()
        lax.fori_loop(n_used_ref[0], n_blocks, start_idle, 0)
        for e in range(N_EXPERTS):
            pad_copy(e).wait()
        lax.fori_loop(n_used_ref[0], n_blocks, wait_idle, 0)

    @pl.when(i >= 2)
    def _():
        wait_runs(i - 2, slot)

    prev_live = jnp.minimum(i, 1)
    for e in range(N_EXPERTS):
        run_copy(jnp.maximum(i - 1, 0), e, 1 - slot, prev_live).start()

    h2b = h2b_ref[...]
    for a0 in range(0, N_STAGE, STAGE_CHUNK):
        a_iota = lax.broadcasted_iota(I32, (STAGE_CHUNK, MOE_TS), 0) + a0
        onehot = jnp.zeros((STAGE_CHUNK, MOE_TS), F32)
        for k in range(TOP_K):
            onehot = jnp.where(a_iota == pos_ref[k:k + 1, :], 1.0, onehot)
        _store_rows(stage.at[slot], a0, _pack_rows(_dot(onehot.astype(BF16), h2b)))

    @pl.when(i == n_tiles - 1)
    def _():
        for e in range(N_EXPERTS):
            run_copy(i, e, slot).start()

        @pl.when(i >= 1)
        def _():
            wait_runs(i - 1, 1 - slot)
        wait_runs(i, slot)


def _dispatch_call(h2b, pos8, tables, n_blocks):
    T, D = h2b.shape
    n_tiles = T // MOE_TS
    grid_spec = pltpu.PrefetchScalarGridSpec(
        num_scalar_prefetch=7,
        grid=(n_tiles,),
        in_specs=[pl.BlockSpec((MOE_TS, D), lambda i, *_: (i, 0)),
                  pl.BlockSpec((V7X_SUBLANES, MOE_TS), lambda i, *_: (0, i))],
        out_specs=pl.BlockSpec(memory_space=pl.ANY),
        scratch_shapes=[pltpu.VMEM((2, N_STAGE * ROW_SUB, V7X_LANES), U32),
                        pltpu.VMEM((MOE_BLOCK * ROW_SUB, V7X_LANES), U32),
                        pltpu.SemaphoreType.DMA((2,)),
                        pltpu.SemaphoreType.DMA(())],
    )
    return pl.pallas_call(
        functools.partial(_dispatch_kernel, n_blocks=n_blocks),
        grid_spec=grid_spec,
        out_shape=jax.ShapeDtypeStruct((n_blocks * MOE_BLOCK * ROW_SUB, V7X_LANES), U32),
        compiler_params=pltpu.CompilerParams(
            dimension_semantics=("arbitrary",),
            vmem_limit_bytes=V7X_VMEM_BYTES - 16 * 1024 * 1024),
        name="dispatch",
    )(tables["run_src"], tables["run_n"], tables["run_dst"], tables["tile_rows"], tables["pad_dst"],
      tables["pad_n"], tables["n_used"], h2b, pos8)


def _moe_kernel(be_ref, group_end_ref, n_used_ref, xs_ref, wup_hbm, wdn_hbm, *rest, layer):
    bias_refs = rest[:2 * MOE_STEP_BLOCKS]
    ys_ref, wup_f32, wdn_f32, wup_bf, wdn_bf, group_count, wsem = rest[2 * MOE_STEP_BLOCKS:]
    step = pl.program_id(0)
    n_used = n_used_ref[0]

    def weight_copies(e, buf):
        return (pltpu.make_async_copy(wup_hbm.at[layer, e], wup_f32.at[buf], wsem.at[buf, 0]),
                pltpu.make_async_copy(wdn_hbm.at[layer, e], wdn_f32.at[buf], wsem.at[buf, 1]))

    @pl.when(step == 0)
    def _():
        group_count[0] = 0
        for c in weight_copies(be_ref[0], 0):
            c.start()

    halves = []
    for half in range(MOE_STEP_BLOCKS):
        blk = step * MOE_STEP_BLOCKS + half
        expert = be_ref[blk]
        new_expert = (blk == 0) | (expert != be_ref[jnp.maximum(blk - 1, 0)])

        @pl.when((blk < n_used) & new_expert)
        def _(expert=expert):
            par = group_count[0] % 2
            group_count[0] = group_count[0] + 1
            next_blk = group_end_ref[expert]

            @pl.when(next_blk < n_used)
            def _():
                for c in weight_copies(be_ref[next_blk], 1 - par):
                    c.start()

            for c in weight_copies(expert, par):
                c.wait()
            wup_bf[par] = wup_f32[par].astype(BF16)
            wdn_bf[par] = wdn_f32[par].astype(BF16)

        halves.append((group_count[0] - 1) % 2)

    @pl.when(step * MOE_STEP_BLOCKS < n_used)
    def _():
        for half, par in enumerate(halves):
            bup_ref, bdn_ref = bias_refs[2 * half], bias_refs[2 * half + 1]
            r0 = half * MOE_BLOCK
            xs = _unpack_rows(_load_rows(xs_ref, r0, MOE_BLOCK))
            gu = _dot(xs, wup_bf[par]) + bup_ref[0, 0]
            x_glu = jnp.minimum(gu[:, :D_FF], SWIGLU_LIMIT)
            x_lin = jnp.clip(gu[:, D_FF:], -SWIGLU_LIMIT, SWIGLU_LIMIT)
            act = x_glu * _sigmoid(SWIGLU_ALPHA * x_glu) * (x_lin + 1.0)
            y = _dot(act.astype(BF16), wdn_bf[par]) + bdn_ref[0, 0]
            _store_rows(ys_ref, r0, _pack_rows(y.astype(BF16).astype(F32)))

    @pl.when(step * MOE_STEP_BLOCKS >= n_used)
    def _():
        ys_ref[...] = jnp.zeros_like(ys_ref)


def _moe_call(layer, xs, tables, w_up, b_up, w_down, b_down):
    n_blocks = tables["block_e"].shape[0]
    assert n_blocks % MOE_STEP_BLOCKS == 0
    step_rows = MOE_STEP_BLOCKS * MOE_BLOCK * ROW_SUB
    bias_spec = lambda width, half: pl.BlockSpec(
        (1, 1, 1, width), lambda s, be, *_: (layer, be[s * MOE_STEP_BLOCKS + half], 0, 0))
    grid_spec = pltpu.PrefetchScalarGridSpec(
        num_scalar_prefetch=3,
        grid=(n_blocks // MOE_STEP_BLOCKS,),
        in_specs=[
            pl.BlockSpec((step_rows, V7X_LANES), lambda s, *_: (s, 0)),
            pl.BlockSpec(memory_space=pl.ANY),
            pl.BlockSpec(memory_space=pl.ANY),
        ] + [bias_spec(width, half) for half in range(MOE_STEP_BLOCKS) for width in (2 * D_FF, D_MODEL)],
        out_specs=pl.BlockSpec((step_rows, V7X_LANES), lambda s, *_: (s, 0)),
        scratch_shapes=[pltpu.VMEM((2, D_MODEL, 2 * D_FF), F32), pltpu.VMEM((2, D_FF, D_MODEL), F32),
                        pltpu.VMEM((2, D_MODEL, 2 * D_FF), BF16), pltpu.VMEM((2, D_FF, D_MODEL), BF16),
                        pltpu.SMEM((1,), I32), pltpu.SemaphoreType.DMA((2, 2))],
    )
    return pl.pallas_call(
        functools.partial(_moe_kernel, layer=layer),
        grid_spec=grid_spec,
        out_shape=jax.ShapeDtypeStruct(xs.shape, U32),
        compiler_params=pltpu.CompilerParams(
            dimension_semantics=("arbitrary",),
            vmem_limit_bytes=V7X_VMEM_BYTES - 8 * 1024 * 1024),
        name="moe",
    )(tables["block_e"], tables["group_end"], tables["n_used"], xs, w_up, w_down,
      *([b_up, b_down] * MOE_STEP_BLOCKS))


def _combine_kernel(run_src, run_n, run_dst, tile_rows, x1_ref, pos_ref, gate_ref, p_ref, ple_g_ref,
                    gate_w_ref, proj_w_ref, fin_g_ref, ys_hbm, out_ref, stage, run_sem, *, last):
    i = pl.program_id(0)
    n_tiles = pl.num_programs(0)
    slot = i % 2

    def run_copy(tile, e, buf, live=1):
        k = tile * N_EXPERTS + e
        n = run_n[k] * live
        return pltpu.make_async_copy(ys_hbm.at[_row_span(run_dst[k], n)],
                                     stage.at[buf, _row_span(run_src[k], n)], run_sem.at[buf])

    @pl.when(i == 0)
    def _():
        stage[...] = jnp.zeros_like(stage)
        for e in range(N_EXPERTS):
            run_copy(0, e, 0).start()

    n_rows = tile_rows[i]
    pltpu.make_async_copy(ys_hbm.at[_row_span(0, n_rows)], stage.at[slot, _row_span(0, n_rows)],
                          run_sem.at[slot]).wait()

    next_live = jnp.where(i + 1 < n_tiles, 1, 0)
    for e in range(N_EXPERTS):
        run_copy(jnp.minimum(i + 1, n_tiles - 1), e, 1 - slot, next_live).start()

    x2 = x1_ref[...]
    for a0 in range(0, N_STAGE, STAGE_CHUNK):
        a_iota = lax.broadcasted_iota(I32, (MOE_TS, STAGE_CHUNK), 1) + a0
        weights = jnp.zeros((MOE_TS, STAGE_CHUNK), F32)
        for k in range(TOP_K):
            weights = jnp.where(a_iota == pos_ref[:, k:k + 1], gate_ref[:, k:k + 1], weights)
        y_sorted = _unpack_rows(_load_rows(stage.at[slot], a0, STAGE_CHUNK))
        x2 = x2 + _dot(weights.astype(BF16), y_sorted)
    h3 = _rms_norm(x2, ple_g_ref[...]).astype(BF16)
    g = _sigmoid(_dot(h3, gate_w_ref[...]))
    pp = _dot(p_ref[0, 0].astype(BF16), proj_w_ref[...])
    x3 = x2 + g * pp
    if last:
        x3 = _rms_norm(x3, fin_g_ref[...])
    out_ref[...] = x3


def _combine_call(layer, x1, ys, pos_tm, gate_tm, p, tables, lw, fin_g, last):
    T, D = x1.shape
    n_tiles = T // MOE_TS
    n_seq = p.shape[2] // MOE_TS
    consts = [lw["ple_norm"], lw["ple_gate_w"], lw["ple_proj_w"]]
    grid_spec = pltpu.PrefetchScalarGridSpec(
        num_scalar_prefetch=4,
        grid=(n_tiles,),
        in_specs=[pl.BlockSpec((MOE_TS, D), lambda i, *_: (i, 0)),
                  pl.BlockSpec((MOE_TS, V7X_SUBLANES), lambda i, *_: (i, 0)),
                  pl.BlockSpec((MOE_TS, V7X_SUBLANES), lambda i, *_: (i, 0)),
                  pl.BlockSpec((1, 1, MOE_TS, PLE_DIM), lambda i, *_: (layer, i // n_seq, i % n_seq, 0))]
        + [_const_spec(c.shape, layer) for c in consts] + [_const_spec(fin_g.shape, 0)]
        + [pl.BlockSpec(memory_space=pl.ANY)],
        out_specs=pl.BlockSpec((MOE_TS, D), lambda i, *_: (i, 0)),
        scratch_shapes=[pltpu.VMEM((2, N_STAGE * ROW_SUB, V7X_LANES), U32),
                        pltpu.SemaphoreType.DMA((2,))],
    )
    return pl.pallas_call(
        functools.partial(_combine_kernel, last=last),
        grid_spec=grid_spec,
        out_shape=jax.ShapeDtypeStruct((T, D), F32),
        compiler_params=pltpu.CompilerParams(
            dimension_semantics=("arbitrary",),
            vmem_limit_bytes=V7X_VMEM_BYTES - 16 * 1024 * 1024),
        name="combine",
    )(tables["run_src"], tables["run_n"], tables["run_dst"], tables["tile_rows"], x1, pos_tm, gate_tm, p,
      *consts, fin_g, ys)


def _routing_tables(tile_counts, n_blocks):
    c = tile_counts[:, :, 0].astype(I32)
    c = (c + RUN_ALIGN - 1) // RUN_ALIGN * RUN_ALIGN
    counts = jnp.sum(c, axis=0)
    padded = (counts + MOE_BLOCK - 1) // MOE_BLOCK * MOE_BLOCK
    pad_end = jnp.cumsum(padded)
    pad_start = pad_end - padded
    run_dst = pad_start[None, :] + jnp.cumsum(c, axis=0) - c
    run_src = jnp.cumsum(c, axis=1) - c
    n_used = pad_end[-1] // MOE_BLOCK
    blk0 = jnp.arange(n_blocks, dtype=I32) * MOE_BLOCK
    be = jnp.minimum(jnp.sum(blk0[:, None] >= pad_end[None, :], axis=1), N_EXPERTS - 1)
    be = be[jnp.minimum(jnp.arange(n_blocks), n_used - 1)]
    return {
        "group_end": (pad_end // MOE_BLOCK).astype(I32),
        "run_src": run_src.reshape(-1).astype(I32), "run_n": c.reshape(-1),
        "run_dst": run_dst.reshape(-1).astype(I32), "tile_rows": jnp.sum(c, axis=1).astype(I32),
        "pad_dst": (pad_start + counts).astype(I32), "pad_n": (padded - counts).astype(I32),
        "n_used": n_used.reshape(1).astype(I32), "block_e": be.astype(I32),
    }


def kernel(x, p, mix_norm, w_in, b_in, pool_w, pool_scale, conv_w, conv_b, conv_norm_g, conv_norm_b,
           sgu_norm_g, sgu_norm_b, sgu_w, sgu_b, branch_w, branch_b, w_out, moe_norm, router_w,
           router_b, expert_w_up, expert_b_up, expert_w_down, expert_b_down, ple_norm, ple_gate_w,
           ple_proj_w, final_norm):
    B, S, D = x.shape
    T = B * S
    depth = w_in.shape[0]
    assert D == D_MODEL and S % MIX_TS == 0 and w_in.shape[2] == IN_WIDTH
    max_rows = T * TOP_K + (T // MOE_TS) * N_EXPERTS * (RUN_ALIGN - 1)
    n_blocks = -(-max_rows // MOE_BLOCK) + N_EXPERTS
    n_blocks += -n_blocks % MOE_STEP_BLOCKS
    rows = lambda a: a[:, None, :]
    b_up4 = expert_b_up[:, :, None, :]
    b_down4 = expert_b_down[:, :, None, :]
    lw = {
        "mix_norm": rows(mix_norm), "w_in": w_in.astype(BF16), "b_in": rows(b_in),
        "pool_w": pool_w.astype(BF16), "pool_scale": rows(pool_scale),
        "conv_w": conv_w, "conv_b": rows(conv_b),
        "conv_norm_g": rows(conv_norm_g), "conv_norm_b": rows(conv_norm_b),
        "sgu_norm_g": rows(sgu_norm_g), "sgu_norm_b": rows(sgu_norm_b),
        "sgu_w": sgu_w, "sgu_bt": jnp.swapaxes(sgu_b, 1, 2),
        "branch_w": branch_w.astype(BF16), "branch_b": branch_b,
        "w_out": w_out.astype(BF16), "moe_norm": rows(moe_norm),
        "router_wt": jnp.swapaxes(router_w, 1, 2), "router_b": router_b[:, :, None],
        "ple_norm": rows(ple_norm), "ple_gate_w": ple_gate_w.astype(BF16),
        "ple_proj_w": ple_proj_w.astype(BF16),
    }
    fin_g = final_norm.reshape(1, 1, D)
    for i in range(depth):
        x1, h2b, pos8, gate8, tile_counts = _mixer_call(i, x, lw)
        tables = _routing_tables(tile_counts, n_blocks)
        xs = _dispatch_call(h2b, pos8, tables, n_blocks)
        ys = _moe_call(i, xs, tables, expert_w_up, b_up4, expert_w_down, b_down4)
        x = _combine_call(i, x1.reshape(T, D), ys, pos8.T, gate8.T, p, tables, lw, fin_g,
                          last=(i == depth - 1)).reshape(B, S, D)
    return x
```

```python
import functools

import jax
import jax.numpy as jnp
from jax import lax
from jax.experimental import pallas as pl
from jax.experimental.pallas import tpu as pltpu

F32 = jnp.float32
BF16 = jnp.bfloat16
I32 = jnp.int32
U32 = jnp.uint32

D_MODEL = 1024
POOL_WINDOWS = (2, 4, 8, 16)
POOL_CH = 128
BRANCH_WIDTH = 512
CONV_K = 31
SGU_CHUNK = 128
SGU_HEADS = 4
N_BRANCH = 3
N_EXPERTS = 32
TOP_K = 4
D_FF = 1024
SWIGLU_LIMIT = 7.0
SWIGLU_ALPHA = 1.702
MOE_BLOCK = 256
MOE_STEP_BLOCKS = 2
assert MOE_STEP_BLOCKS <= 2
PLE_DIM = 256
EPS = 1e-6

V7X_SUBLANES = 8
V7X_LANES = 128
ROW_WORDS = D_MODEL // 2
ROW_SUB = ROW_WORDS // V7X_LANES
RUN_ALIGN = V7X_SUBLANES // ROW_SUB
V7X_VMEM_BYTES = 64 * 1024 * 1024

MIX_TS = 512
HALO = 32
CONV_ROWS = 32
SIDE_CHUNK = 256
MOE_TS = 256
N_STAGE = TOP_K * MOE_TS + N_EXPERTS * (RUN_ALIGN - 1)
STAGE_CHUNK = N_STAGE // 3
assert MIX_TS % MOE_TS == 0 and STAGE_CHUNK * 3 == N_STAGE and STAGE_CHUNK % V7X_SUBLANES == 0

C_POOL = 0
C_CONV = C_POOL + BRANCH_WIDTH
C_SGU_U = C_CONV + 2 * BRANCH_WIDTH
C_SGU_V = C_SGU_U + BRANCH_WIDTH
C_GATE = C_SGU_V + BRANCH_WIDTH
IN_WIDTH = C_GATE + N_BRANCH * D_MODEL
S_POOL = 0
S_SGU_U = S_POOL + BRANCH_WIDTH
S_SGU_V = S_SGU_U + BRANCH_WIDTH
S_GATE = S_SGU_V + BRANCH_WIDTH
SIDE_WIDTH = S_GATE + N_BRANCH * D_MODEL
assert S_SGU_U % SIDE_CHUNK == 0 and SIDE_WIDTH % SIDE_CHUNK == 0


def _rms_norm(x, g):
    return x * lax.rsqrt(jnp.mean(x * x, axis=-1, keepdims=True) + EPS) * g


def _layer_norm(x, g, b):
    mu = jnp.mean(x, axis=-1, keepdims=True)
    xc = x - mu
    var = jnp.mean(xc * xc, axis=-1, keepdims=True)
    return xc * lax.rsqrt(var + EPS) * g + b


def _sigmoid(x):
    return 0.5 * jnp.tanh(0.5 * x) + 0.5


def _dot(a, b):
    return jnp.dot(a, b, preferred_element_type=F32)


def _pack_rows(v):
    bits = lax.bitcast_convert_type(v, U32)
    return (bits[:, :ROW_WORDS] >> 16) | (bits[:, ROW_WORDS:] & jnp.uint32(0xFFFF0000))


def _unpack_rows(w):
    low = lax.bitcast_convert_type(w << 16, F32)
    high = lax.bitcast_convert_type(w & jnp.uint32(0xFFFF0000), F32)
    return jnp.concatenate([low, high], axis=-1).astype(BF16)


def _store_rows(ref_2d, first_row, words):
    n = words.shape[0]
    for j in range(ROW_SUB):
        ref_2d[pl.ds(first_row * ROW_SUB + j, n, stride=ROW_SUB), :] = words[:, j * V7X_LANES:(j + 1) * V7X_LANES]


def _load_rows(ref_2d, first_row, n):
    return jnp.concatenate(
        [ref_2d[pl.ds(first_row * ROW_SUB + j, n, stride=ROW_SUB), :] for j in range(ROW_SUB)], axis=-1)


def _mixer_kernel(x_ref, x_next_ref, mix_g_ref, w_in_ref, b_in_ref, pool_w_ref, pool_scale_ref, conv_w_ref,
                  conv_b_ref, cn_g_ref, cn_b_ref, sn_g_ref, sn_b_ref, sgu_w_ref, sgu_bt_ref,
                  branch_w_ref, branch_b_ref, w_out_ref, moe_g_ref, rw_t_ref, rb_ref,
                  x1_ref, h2b_ref, pos_ref, gate_ref, counts_ref,
                  pool_hist, conv_hist, conv_shift, side_buf, x1_prev, before_buf, h_buf, *, n_seq):
    i = pl.program_id(0)
    n_tiles = pl.num_programs(0) - 1
    route_refs = (before_buf, moe_g_ref, rw_t_ref, rb_ref, h2b_ref, pos_ref, gate_ref, counts_ref)

    @pl.when(i == 0)
    def _():
        x1_prev[...] = jnp.zeros_like(x1_prev)
        h_buf[...] = _rms_norm(x_ref[0], mix_g_ref[...]).astype(BF16)
        t_row = lax.broadcasted_iota(I32, (MIX_TS, MIX_TS), 0)
        t_col = lax.broadcasted_iota(I32, (MIX_TS, MIX_TS), 1)
        same_tile = t_row // MOE_TS == t_col // MOE_TS
        before_buf[...] = jnp.where(same_tile, jnp.where(t_row < t_col, 1.0, 0.0), 0.0).astype(BF16)

    @pl.when(i % n_seq == 0)
    def _():
        pool_hist[0:HALO, :] = jnp.zeros((HALO, BRANCH_WIDTH), F32)
        conv_hist[0:HALO, :] = jnp.zeros((HALO, BRANCH_WIDTH), F32)

    @pl.when(i < n_tiles)
    def _():
        _mix_tile(i % n_seq, x_ref, x_next_ref, mix_g_ref, w_in_ref, b_in_ref, pool_w_ref, pool_scale_ref,
                  conv_w_ref, conv_b_ref, cn_g_ref, cn_b_ref, sn_g_ref, sn_b_ref, sgu_w_ref, sgu_bt_ref,
                  branch_w_ref, branch_b_ref, w_out_ref, x1_ref, pool_hist, conv_hist, conv_shift, side_buf,
                  x1_prev, h_buf, functools.partial(_route_tile, x1_prev, *route_refs))

    @pl.when(i == n_tiles)
    def _():
        _route_tile(x1_prev, *route_refs)


def _mix_tile(s, x_ref, x_next_ref, mix_g_ref, w_in_ref, b_in_ref, pool_w_ref, pool_scale_ref, conv_w_ref,
              conv_b_ref, cn_g_ref, cn_b_ref, sn_g_ref, sn_b_ref, sgu_w_ref, sgu_bt_ref, branch_w_ref,
              branch_b_ref, w_out_ref, x1_ref, pool_hist, conv_hist, conv_shift, side_buf, x1_prev, h_buf,
              route_previous):
    ts = MIX_TS
    x = x_ref[0]
    h = h_buf[...]

    def in_proj(c0, width):
        return _dot(h, w_in_ref[:, c0:c0 + width]) + b_in_ref[:, c0:c0 + width]

    zb = in_proj(C_CONV, 2 * BRANCH_WIDTH)
    conv_hist[HALO:HALO + ts, :] = zb[:, :BRANCH_WIDTH] * _sigmoid(zb[:, BRANCH_WIDTH:])
    n_shift_rows = HALO + ts - V7X_SUBLANES
    for sft in range(1, V7X_SUBLANES):
        conv_shift[sft - 1] = conv_hist[sft:sft + n_shift_rows, :]
    yb_parts = []
    n_conv_blocks = ts // CONV_ROWS
    n_side_chunks = SIDE_WIDTH // SIDE_CHUNK
    for bi in range(n_conv_blocks):
        r0 = bi * CONV_ROWS
        for ci in range(bi * n_side_chunks // n_conv_blocks, (bi + 1) * n_side_chunks // n_conv_blocks):
            d0 = ci * SIDE_CHUNK
            c0 = d0 if d0 < S_SGU_U else d0 + (C_SGU_U - S_SGU_U)
            side_buf[:, d0:d0 + SIDE_CHUNK] = in_proj(c0, SIDE_CHUNK)
        acc = jnp.zeros((CONV_ROWS, BRANCH_WIDTH), F32) + conv_b_ref[...]
        for k in range(CONV_K):
            off = HALO - (CONV_K - 1) + k + r0
            base, sft = off - off % V7X_SUBLANES, off % V7X_SUBLANES
            if sft == 0:
                window = conv_hist[base:base + CONV_ROWS, :]
            else:
                window = conv_shift[sft - 1, base:base + CONV_ROWS, :]
            acc = acc + conv_w_ref[k:k + 1, :] * window
        yb_rows = _layer_norm(acc, cn_g_ref[...], cn_b_ref[...])
        yb_parts.append((yb_rows * _sigmoid(yb_rows)).astype(BF16))
    yb = jnp.concatenate(yb_parts, axis=0)
    conv_hist[0:HALO, :] = conv_hist[ts:ts + HALO, :]

    za = side_buf[:, S_POOL:S_POOL + BRANCH_WIDTH]
    pool_hist[HALO:HALO + ts, :] = za
    row = lax.broadcasted_iota(I32, (ts, 1), 0) + s * ts
    mixed = []
    for g, w in enumerate(POOL_WINDOWS):
        c0 = g * POOL_CH
        cur = za[:, c0:c0 + POOL_CH]
        acc = cur
        for j in range(1, w):
            acc = acc + pool_hist[HALO - j:HALO - j + ts, c0:c0 + POOL_CH]
        count = jnp.minimum(row + 1, w).astype(F32)
        pooled = (acc / count - cur).astype(BF16)
        mixed.append(_dot(pooled, pool_w_ref[g]))
    ya = (jnp.concatenate(mixed, axis=-1) * pool_scale_ref[...]).astype(BF16)
    pool_hist[0:HALO, :] = pool_hist[ts:ts + HALO, :]

    zu = side_buf[:, S_SGU_U:S_SGU_U + BRANCH_WIDTH]
    zv = side_buf[:, S_SGU_V:S_SGU_V + BRANCH_WIDTH]
    v = _layer_norm(zv, sn_g_ref[...], sn_b_ref[...]).astype(BF16)
    tri = (lax.broadcasted_iota(I32, (SGU_CHUNK, SGU_CHUNK), 0)
           >= lax.broadcasted_iota(I32, (SGU_CHUNK, SGU_CHUNK), 1))
    w_tri = [jnp.where(tri, sgu_w_ref[hd], 0.0).astype(BF16) for hd in range(SGU_HEADS)]
    chunks = []
    for c in range(ts // SGU_CHUNK):
        heads = []
        for hd in range(SGU_HEADS):
            vv = v[c * SGU_CHUNK:(c + 1) * SGU_CHUNK, hd * 128:(hd + 1) * 128]
            heads.append(_dot(w_tri[hd], vv) + sgu_bt_ref[:, hd:hd + 1])
        chunks.append(jnp.concatenate(heads, axis=-1))
    yc = (zu * jnp.concatenate(chunks, axis=0)).astype(BF16)

    route_previous()
    h_buf[...] = _rms_norm(x_next_ref[0], mix_g_ref[...]).astype(BF16)

    merged = jnp.zeros((ts, D_MODEL), F32)
    for k, yk in enumerate((ya, yb, yc)):
        proj = _dot(yk, branch_w_ref[k]) + branch_b_ref[k:k + 1, :]
        g0 = S_GATE + k * D_MODEL
        merged = merged + _sigmoid(side_buf[:, g0:g0 + D_MODEL]) * proj
    x1 = x + _dot(merged.astype(BF16), w_out_ref[...])
    x1_ref[0] = x1
    x1_prev[...] = x1


def _route_tile(x1_ref, before_ref, moe_g_ref, rw_t_ref, rb_ref, h2b_ref, pos_ref, gate_ref, counts_ref):
    ts = MIX_TS
    h2 = _rms_norm(x1_ref[...], moe_g_ref[...])
    h2_hi = h2.astype(BF16)
    h2b_ref[...] = h2_hi
    h2_lo = (h2 - h2_hi.astype(F32)).astype(BF16)
    rw = rw_t_ref[...]
    rw_hi = rw.astype(BF16)
    rw_lo = (rw - rw_hi.astype(F32)).astype(BF16)
    nt = (((1,), (1,)), ((), ()))
    logits = (lax.dot_general(rw_hi, h2_hi, nt, preferred_element_type=F32)
              + lax.dot_general(rw_hi, h2_lo, nt, preferred_element_type=F32)
              + lax.dot_general(rw_lo, h2_hi, nt, preferred_element_type=F32)) + rb_ref[...]
    e_iota = lax.broadcasted_iota(I32, (N_EXPERTS, ts), 0).astype(F32)
    vals = logits
    top_v, sels = [], []
    for _k in range(TOP_K):
        m = jnp.max(vals, axis=0, keepdims=True)
        idx = jnp.min(jnp.where(vals == m, e_iota, float(N_EXPERTS)), axis=0, keepdims=True)
        sel = e_iota == idx
        vals = jnp.where(sel, -jnp.inf, vals)
        top_v.append(m)
        sels.append(sel)
    exps = [jnp.exp(tv - top_v[0]) for tv in top_v]
    denom = exps[0] + exps[1] + exps[2] + exps[3]
    chosen = jnp.zeros((N_EXPERTS, ts), F32)
    for sel in sels:
        chosen = chosen + jnp.where(sel, 1.0, 0.0)
    chosen_b = chosen.astype(BF16)
    prefix = _dot(chosen_b, before_ref[...])
    lower = jnp.where(lax.broadcasted_iota(I32, (N_EXPERTS, N_EXPERTS), 0)
                      > lax.broadcasted_iota(I32, (N_EXPERTS, N_EXPERTS), 1), 1.0, 0.0).astype(BF16)
    base_parts = []
    for j in range(ts // MOE_TS):
        lanes = slice(j * MOE_TS, (j + 1) * MOE_TS)
        count = jnp.sum(chosen[:, lanes], axis=1, keepdims=True)
        run_rows = jnp.floor((count + (RUN_ALIGN - 1)) * (1.0 / RUN_ALIGN)) * RUN_ALIGN
        run_start = _dot(lower, jnp.broadcast_to(run_rows, (N_EXPERTS, V7X_LANES)).astype(BF16))[:, 0:1]
        base_parts.append(prefix[:, lanes] + run_start)
        counts_ref[j] = jnp.broadcast_to(count, (N_EXPERTS, V7X_LANES))
    base = jnp.concatenate(base_parts, axis=1)
    zeros4 = jnp.zeros((V7X_SUBLANES - TOP_K, ts), F32)
    pos = [jnp.sum(jnp.where(sel, base, 0.0), axis=0, keepdims=True) for sel in sels]
    pos_ref[...] = jnp.concatenate(pos + [zeros4], axis=0).astype(I32)
    gate_ref[...] = jnp.concatenate([e / denom for e in exps] + [zeros4], axis=0)


def _const_spec(shape, layer):
    nd = len(shape)
    return pl.BlockSpec((None,) + tuple(shape[1:]), lambda *_: (layer,) + (0,) * (nd - 1),
                        pipeline_mode=pl.Buffered(1))


def _mixer_call(layer, x, lw):
    B, S, D = x.shape
    ts = MIX_TS
    n_s = S // ts
    T = B * S
    n_tiles = B * n_s
    mixed = lambda i: jnp.minimum(i, n_tiles - 1)
    routed = lambda i: jnp.maximum(i - 1, 0)
    consts = [lw["mix_norm"], lw["w_in"], lw["b_in"], lw["pool_w"], lw["pool_scale"], lw["conv_w"],
              lw["conv_b"], lw["conv_norm_g"], lw["conv_norm_b"], lw["sgu_norm_g"], lw["sgu_norm_b"],
              lw["sgu_w"], lw["sgu_bt"], lw["branch_w"], lw["branch_b"], lw["w_out"], lw["moe_norm"],
              lw["router_wt"], lw["router_b"]]
    x_spec = pl.BlockSpec((1, ts, D), lambda i: (mixed(i) // n_s, mixed(i) % n_s, 0))
    x_next_spec = pl.BlockSpec((1, ts, D), lambda i: (mixed(i + 1) // n_s, mixed(i + 1) % n_s, 0))
    in_specs = [x_spec, x_next_spec] + [_const_spec(c.shape, layer) for c in consts]
    out_shape = (
        jax.ShapeDtypeStruct((B, S, D), F32),
        jax.ShapeDtypeStruct((T, D), BF16),
        jax.ShapeDtypeStruct((V7X_SUBLANES, T), I32),
        jax.ShapeDtypeStruct((V7X_SUBLANES, T), F32),
        jax.ShapeDtypeStruct((T // MOE_TS, N_EXPERTS, V7X_LANES), F32),
    )
    out_specs = (
        x_spec,
        pl.BlockSpec((ts, D), lambda i: (routed(i), 0)),
        pl.BlockSpec((V7X_SUBLANES, ts), lambda i: (0, routed(i))),
        pl.BlockSpec((V7X_SUBLANES, ts), lambda i: (0, routed(i))),
        pl.BlockSpec((ts // MOE_TS, N_EXPERTS, V7X_LANES), lambda i: (routed(i), 0, 0)),
    )
    return pl.pallas_call(
        functools.partial(_mixer_kernel, n_seq=n_s),
        grid=(n_tiles + 1,),
        in_specs=in_specs,
        out_specs=out_specs,
        out_shape=out_shape,
        scratch_shapes=[pltpu.VMEM((HALO + ts, BRANCH_WIDTH), F32),
                        pltpu.VMEM((HALO + ts, BRANCH_WIDTH), F32),
                        pltpu.VMEM((V7X_SUBLANES - 1, HALO + ts - V7X_SUBLANES, BRANCH_WIDTH), F32),
                        pltpu.VMEM((ts, SIDE_WIDTH), F32),
                        pltpu.VMEM((ts, D), F32),
                        pltpu.VMEM((ts, ts), BF16),
                        pltpu.VMEM((ts, D), BF16)],
        compiler_params=pltpu.CompilerParams(
            dimension_semantics=("arbitrary",),
            vmem_limit_bytes=V7X_VMEM_BYTES - 8 * 1024 * 1024),
        name="mixer",
    )(x, x, *consts)


def _row_span(first_row, n_rows):
    return pl.ds(pl.multiple_of(first_row * ROW_SUB, V7X_SUBLANES),
                 pl.multiple_of(n_rows * ROW_SUB, V7X_SUBLANES))


def _dispatch_kernel(run_src, run_n, run_dst, tile_rows, pad_dst, pad_n, n_used_ref, h2b_ref, pos_ref,
                     xs_hbm, stage, zero_rows, run_sem, fill_sem, *, n_blocks):
    i = pl.program_id(0)
    n_tiles = pl.num_programs(0)
    slot = i % 2
    block_rows = MOE_BLOCK * ROW_SUB

    def run_copy(tile, e, buf, live=1):
        k = tile * N_EXPERTS + e
        n = run_n[k] * live
        return pltpu.make_async_copy(stage.at[buf, _row_span(run_src[k], n)],
                                     xs_hbm.at[_row_span(run_dst[k], n)], run_sem.at[buf])

    def wait_runs(tile, buf):
        n = tile_rows[tile]
        pltpu.make_async_copy(stage.at[buf, _row_span(0, n)], xs_hbm.at[_row_span(0, n)],
                              run_sem.at[buf]).wait()

    def pad_copy(e):
        return pltpu.make_async_copy(zero_rows.at[_row_span(0, pad_n[e])],
                                     xs_hbm.at[_row_span(pad_dst[e], pad_n[e])], fill_sem)

    def idle_block_copy(blk):
        return pltpu.make_async_copy(
            zero_rows, xs_hbm.at[pl.ds(pl.multiple_of(blk * block_rows, block_rows), block_rows)], fill_sem)

    @pl.when(i == 0)
    def _():
        zero_rows[...] = jnp.zeros_like(zero_rows)

        def start_idle(blk, c):
            idle_block_copy(blk).start()
            return c

        def wait_idle(blk, c):
            idle_block_copy(blk).wait()
            return c

        for e in range(N_EXPERTS):
            pad_copy(e).start()
        lax.fori_loop(n_used_ref[0], n_blocks, start_idle, 0)
        for e in range(N_EXPERTS):
            pad_copy(e).wait()
        lax.fori_loop(n_used_ref[0], n_blocks, wait_idle, 0)

    @pl.when(i >= 2)
    def _():
        wait_runs(i - 2, slot)

    prev_live = jnp.minimum(i, 1)
    for e in range(N_EXPERTS):
        run_copy(jnp.maximum(i - 1, 0), e, 1 - slot, prev_live).start()

    h2b = h2b_ref[...]
    for a0 in range(0, N_STAGE, STAGE_CHUNK):
        a_iota = lax.broadcasted_iota(I32, (STAGE_CHUNK, MOE_TS), 0) + a0
        onehot = jnp.zeros((STAGE_CHUNK, MOE_TS), F32)
        for k in range(TOP_K):
            onehot = jnp.where(a_iota == pos_ref[k:k + 1, :], 1.0, onehot)
        _store_rows(stage.at[slot], a0, _pack_rows(_dot(onehot.astype(BF16), h2b)))

    @pl.when(i == n_tiles - 1)
    def _():
        for e in range(N_EXPERTS):
            run_copy(i, e, slot).start()

        @pl.when(i >= 1)
        def _():
            wait_runs(i - 1, 1 - slot)
        wait_runs(i, slot)


def _dispatch_call(h2b, pos8, tables, n_blocks):
    T, D = h2b.shape
    n_tiles = T // MOE_TS
    grid_spec = pltpu.PrefetchScalarGridSpec(
        num_scalar_prefetch=7,
        grid=(n_tiles,),
        in_specs=[pl.BlockSpec((MOE_TS, D), lambda i, *_: (i, 0)),
                  pl.BlockSpec((V7X_SUBLANES, MOE_TS), lambda i, *_: (0, i))],
        out_specs=pl.BlockSpec(memory_space=pl.ANY),
        scratch_shapes=[pltpu.VMEM((2, N_STAGE * ROW_SUB, V7X_LANES), U32),
                        pltpu.VMEM((MOE_BLOCK * ROW_SUB, V7X_LANES), U32),
                        pltpu.SemaphoreType.DMA((2,)),
                        pltpu.SemaphoreType.DMA(())],
    )
    return pl.pallas_call(
        functools.partial(_dispatch_kernel, n_blocks=n_blocks),
        grid_spec=grid_spec,
        out_shape=jax.ShapeDtypeStruct((n_blocks * MOE_BLOCK * ROW_SUB, V7X_LANES), U32),
        compiler_params=pltpu.CompilerParams(
            dimension_semantics=("arbitrary",),
            vmem_limit_bytes=V7X_VMEM_BYTES - 16 * 1024 * 1024),
        name="dispatch",
    )(tables["run_src"], tables["run_n"], tables["run_dst"], tables["tile_rows"], tables["pad_dst"],
      tables["pad_n"], tables["n_used"], h2b, pos8)


def _moe_kernel(be_ref, group_end_ref, n_used_ref, xs_ref, wup_hbm, wdn_hbm, *rest, layer):
    bias_refs = rest[:2 * MOE_STEP_BLOCKS]
    ys_ref, wup_f32, wdn_f32, wup_bf, wdn_bf, group_count, wsem = rest[2 * MOE_STEP_BLOCKS:]
    step = pl.program_id(0)
    n_used = n_used_ref[0]

    def weight_copies(e, buf):
        return (pltpu.make_async_copy(wup_hbm.at[layer, e], wup_f32.at[buf], wsem.at[buf, 0]),
                pltpu.make_async_copy(wdn_hbm.at[layer, e], wdn_f32.at[buf], wsem.at[buf, 1]))

    @pl.when(step == 0)
    def _():
        group_count[0] = 0
        for c in weight_copies(be_ref[0], 0):
            c.start()

    halves = []
    for half in range(MOE_STEP_BLOCKS):
        blk = step * MOE_STEP_BLOCKS + half
        expert = be_ref[blk]
        new_expert = (blk == 0) | (expert != be_ref[jnp.maximum(blk - 1, 0)])

        @pl.when((blk < n_used) & new_expert)
        def _(expert=expert):
            par = group_count[0] % 2
            group_count[0] = group_count[0] + 1
            next_blk = group_end_ref[expert]

            @pl.when(next_blk < n_used)
            def _():
                for c in weight_copies(be_ref[next_blk], 1 - par):
                    c.start()

            for c in weight_copies(expert, par):
                c.wait()
            wup_bf[par] = wup_f32[par].astype(BF16)
            wdn_bf[par] = wdn_f32[par].astype(BF16)

        halves.append((group_count[0] - 1) % 2)

    @pl.when(step * MOE_STEP_BLOCKS < n_used)
    def _():
        for half, par in enumerate(halves):
            bup_ref, bdn_ref = bias_refs[2 * half], bias_refs[2 * half + 1]
            r0 = half * MOE_BLOCK
            xs = _unpack_rows(_load_rows(xs_ref, r0, MOE_BLOCK))
            gu = _dot(xs, wup_bf[par]) + bup_ref[0, 0]
            x_glu = jnp.minimum(gu[:, :D_FF], SWIGLU_LIMIT)
            x_lin = jnp.clip(gu[:, D_FF:], -SWIGLU_LIMIT, SWIGLU_LIMIT)
            act = x_glu * _sigmoid(SWIGLU_ALPHA * x_glu) * (x_lin + 1.0)
            y = _dot(act.astype(BF16), wdn_bf[par]) + bdn_ref[0, 0]
            _store_rows(ys_ref, r0, _pack_rows(y.astype(BF16).astype(F32)))

    @pl.when(step * MOE_STEP_BLOCKS >= n_used)
    def _():
        ys_ref[...] = jnp.zeros_like(ys_ref)


def _moe_call(layer, xs, tables, w_up, b_up, w_down, b_down):
    n_blocks = tables["block_e"].shape[0]
    assert n_blocks % MOE_STEP_BLOCKS == 0
    step_rows = MOE_STEP_BLOCKS * MOE_BLOCK * ROW_SUB
    bias_spec = lambda width, half: pl.BlockSpec(
        (1, 1, 1, width), lambda s, be, *_: (layer, be[s * MOE_STEP_BLOCKS + half], 0, 0))
    grid_spec = pltpu.PrefetchScalarGridSpec(
        num_scalar_prefetch=3,
        grid=(n_blocks // MOE_STEP_BLOCKS,),
        in_specs=[
            pl.BlockSpec((step_rows, V7X_LANES), lambda s, *_: (s, 0)),
            pl.BlockSpec(memory_space=pl.ANY),
            pl.BlockSpec(memory_space=pl.ANY),
        ] + [bias_spec(width, half) for half in range(MOE_STEP_BLOCKS) for width in (2 * D_FF, D_MODEL)],
        out_specs=pl.BlockSpec((step_rows, V7X_LANES), lambda s, *_: (s, 0)),
        scratch_shapes=[pltpu.VMEM((2, D_MODEL, 2 * D_FF), F32), pltpu.VMEM((2, D_FF, D_MODEL), F32),
                        pltpu.VMEM((2, D_MODEL, 2 * D_FF), BF16), pltpu.VMEM((2, D_FF, D_MODEL), BF16),
                        pltpu.SMEM((1,), I32), pltpu.SemaphoreType.DMA((2, 2))],
    )
    return pl.pallas_call(
        functools.partial(_moe_kernel, layer=layer),
        grid_spec=grid_spec,
        out_shape=jax.ShapeDtypeStruct(xs.shape, U32),
        compiler_params=pltpu.CompilerParams(
            dimension_semantics=("arbitrary",),
            vmem_limit_bytes=V7X_VMEM_BYTES - 8 * 1024 * 1024),
        name="moe",
    )(tables["block_e"], tables["group_end"], tables["n_used"], xs, w_up, w_down,
      *([b_up, b_down] * MOE_STEP_BLOCKS))


def _combine_kernel(run_src, run_n, run_dst, tile_rows, x1_ref, pos_ref, gate_ref, p_ref, ple_g_ref,
                    gate_w_ref, proj_w_ref, fin_g_ref, ys_hbm, out_ref, stage, x2_buf, run_sem, *, last):
    i = pl.program_id(0)
    n_tiles = pl.num_programs(0) - 1
    slot = i % 2
    tile = jnp.minimum(i, n_tiles - 1)
    live = jnp.where(i < n_tiles, 1, 0)

    def run_copy(tile, e, buf, live=1):
        k = tile * N_EXPERTS + e
        n = run_n[k] * live
        return pltpu.make_async_copy(ys_hbm.at[_row_span(run_dst[k], n)],
                                     stage.at[buf, _row_span(run_src[k], n)], run_sem.at[buf])

    @pl.when(i == 0)
    def _():
        stage[...] = jnp.zeros_like(stage)
        x2_buf[...] = jnp.zeros_like(x2_buf)
        for e in range(N_EXPERTS):
            run_copy(0, e, 0).start()

    n_rows = tile_rows[tile] * live
    pltpu.make_async_copy(ys_hbm.at[_row_span(0, n_rows)], stage.at[slot, _row_span(0, n_rows)],
                          run_sem.at[slot]).wait()

    next_live = jnp.where(i + 1 < n_tiles, 1, 0)
    for e in range(N_EXPERTS):
        run_copy(jnp.minimum(i + 1, n_tiles - 1), e, 1 - slot, next_live).start()

    x2p = x2_buf[...]
    h3 = _rms_norm(x2p, ple_g_ref[...]).astype(BF16)
    g = _sigmoid(_dot(h3, gate_w_ref[...]))
    pp = _dot(p_ref[0, 0].astype(BF16), proj_w_ref[...])
    x3 = x2p + g * pp
    if last:
        x3 = _rms_norm(x3, fin_g_ref[...])
    out_ref[...] = x3

    x2 = x1_ref[...]
    for a0 in range(0, N_STAGE, STAGE_CHUNK):
        a_iota = lax.broadcasted_iota(I32, (MOE_TS, STAGE_CHUNK), 1) + a0
        weights = jnp.zeros((MOE_TS, STAGE_CHUNK), F32)
        for k in range(TOP_K):
            weights = jnp.where(a_iota == pos_ref[:, k:k + 1], gate_ref[:, k:k + 1], weights)
        y_sorted = _unpack_rows(_load_rows(stage.at[slot], a0, STAGE_CHUNK))
        x2 = x2 + _dot(weights.astype(BF16), y_sorted)
    x2_buf[...] = x2


def _combine_call(layer, x1, ys, pos_tm, gate_tm, p, tables, lw, fin_g, last):
    T, D = x1.shape
    n_tiles = T // MOE_TS
    n_seq = p.shape[2] // MOE_TS
    consts = [lw["ple_norm"], lw["ple_gate_w"], lw["ple_proj_w"]]
    gathered = lambda i: jnp.minimum(i, n_tiles - 1)
    finished = lambda i: jnp.maximum(i - 1, 0)
    grid_spec = pltpu.PrefetchScalarGridSpec(
        num_scalar_prefetch=4,
        grid=(n_tiles + 1,),
        in_specs=[pl.BlockSpec((MOE_TS, D), lambda i, *_: (gathered(i), 0)),
                  pl.BlockSpec((MOE_TS, V7X_SUBLANES), lambda i, *_: (gathered(i), 0)),
                  pl.BlockSpec((MOE_TS, V7X_SUBLANES), lambda i, *_: (gathered(i), 0)),
                  pl.BlockSpec((1, 1, MOE_TS, PLE_DIM),
                               lambda i, *_: (layer, finished(i) // n_seq, finished(i) % n_seq, 0))]
        + [_const_spec(c.shape, layer) for c in consts] + [_const_spec(fin_g.shape, 0)]
        + [pl.BlockSpec(memory_space=pl.ANY)],
        out_specs=pl.BlockSpec((MOE_TS, D), lambda i, *_: (finished(i), 0)),
        scratch_shapes=[pltpu.VMEM((2, N_STAGE * ROW_SUB, V7X_LANES), U32),
                        pltpu.VMEM((MOE_TS, D), F32),
                        pltpu.SemaphoreType.DMA((2,))],
    )
    return pl.pallas_call(
        functools.partial(_combine_kernel, last=last),
        grid_spec=grid_spec,
        out_shape=jax.ShapeDtypeStruct((T, D), F32),
        compiler_params=pltpu.CompilerParams(
            dimension_semantics=("arbitrary",),
            vmem_limit_bytes=V7X_VMEM_BYTES - 16 * 1024 * 1024),
        name="combine",
    )(tables["run_src"], tables["run_n"], tables["run_dst"], tables["tile_rows"], x1, pos_tm, gate_tm, p,
      *consts, fin_g, ys)


def _routing_tables(tile_counts, n_blocks):
    c = tile_counts[:, :, 0].astype(I32)
    c = (c + RUN_ALIGN - 1) // RUN_ALIGN * RUN_ALIGN
    counts = jnp.sum(c, axis=0)
    padded = (counts + MOE_BLOCK - 1) // MOE_BLOCK * MOE_BLOCK
    pad_end = jnp.cumsum(padded)
    pad_start = pad_end - padded
    run_dst = pad_start[None, :] + jnp.cumsum(c, axis=0) - c
    run_src = jnp.cumsum(c, axis=1) - c
    n_used = pad_end[-1] // MOE_BLOCK
    blk0 = jnp.arange(n_blocks, dtype=I32) * MOE_BLOCK
    be = jnp.minimum(jnp.sum(blk0[:, None] >= pad_end[None, :], axis=1), N_EXPERTS - 1)
    be = be[jnp.minimum(jnp.arange(n_blocks), n_used - 1)]
    return {
        "group_end": (pad_end // MOE_BLOCK).astype(I32),
        "run_src": run_src.reshape(-1).astype(I32), "run_n": c.reshape(-1),
        "run_dst": run_dst.reshape(-1).astype(I32), "tile_rows": jnp.sum(c, axis=1).astype(I32),
        "pad_dst": (pad_start + counts).astype(I32), "pad_n": (padded - counts).astype(I32),
        "n_used": n_used.reshape(1).astype(I32), "block_e": be.astype(I32),
    }


def kernel(x, p, mix_norm, w_in, b_in, pool_w, pool_scale, conv_w, conv_b, conv_norm_g, conv_norm_b,
           sgu_norm_g, sgu_norm_b, sgu_w, sgu_b, branch_w, branch_b, w_out, moe_norm, router_w,
           router_b, expert_w_up, expert_b_up, expert_w_down, expert_b_down, ple_norm, ple_gate_w,
           ple_proj_w, final_norm):
    B, S, D = x.shape
    T = B * S
    depth = w_in.shape[0]
    assert D == D_MODEL and S % MIX_TS == 0 and w_in.shape[2] == IN_WIDTH
    max_rows = T * TOP_K + (T // MOE_TS) * N_EXPERTS * (RUN_ALIGN - 1)
    n_blocks = -(-max_rows // MOE_BLOCK) + N_EXPERTS
    n_blocks += -n_blocks % MOE_STEP_BLOCKS
    rows = lambda a: a[:, None, :]
    b_up4 = expert_b_up[:, :, None, :]
    b_down4 = expert_b_down[:, :, None, :]
    lw = {
        "mix_norm": rows(mix_norm), "w_in": w_in.astype(BF16), "b_in": rows(b_in),
        "pool_w": pool_w.astype(BF16), "pool_scale": rows(pool_scale),
        "conv_w": conv_w, "conv_b": rows(conv_b),
        "conv_norm_g": rows(conv_norm_g), "conv_norm_b": rows(conv_norm_b),
        "sgu_norm_g": rows(sgu_norm_g), "sgu_norm_b": rows(sgu_norm_b),
        "sgu_w": sgu_w, "sgu_bt": jnp.swapaxes(sgu_b, 1, 2),
        "branch_w": branch_w.astype(BF16), "branch_b": branch_b,
        "w_out": w_out.astype(BF16), "moe_norm": rows(moe_norm),
        "router_wt": jnp.swapaxes(router_w, 1, 2), "router_b": router_b[:, :, None],
        "ple_norm": rows(ple_norm), "ple_gate_w": ple_gate_w.astype(BF16),
        "ple_proj_w": ple_proj_w.astype(BF16),
    }
    fin_g = final_norm.reshape(1, 1, D)
    for i in range(depth):
        x1, h2b, pos8, gate8, tile_counts = _mixer_call(i, x, lw)
        tables = _routing_tables(tile_counts, n_blocks)
        xs = _dispatch_call(h2b, pos8, tables, n_blocks)
        ys = _moe_call(i, xs, tables, expert_w_up, b_up4, expert_w_down, b_down4)
        x = _combine_call(i, x1.reshape(T, D), ys, pos8.T, gate8.T, p, tables, lw, fin_g,
                          last=(i == depth - 1)).reshape(B, S, D)
    return x
```

```python
import functools

import jax
import jax.numpy as jnp
from jax import lax
from jax.experimental import pallas as pl
from jax.experimental.pallas import tpu as pltpu

F32 = jnp.float32
BF16 = jnp.bfloat16
I32 = jnp.int32
U32 = jnp.uint32

D_MODEL = 1024
POOL_WINDOWS = (2, 4, 8, 16)
POOL_CH = 128
BRANCH_WIDTH = 512
CONV_K = 31
SGU_CHUNK = 128
SGU_HEADS = 4
N_BRANCH = 3
N_EXPERTS = 32
TOP_K = 4
D_FF = 1024
SWIGLU_LIMIT = 7.0
SWIGLU_ALPHA = 1.702
MOE_BLOCK = 256
MOE_STEP_BLOCKS = 2
assert MOE_STEP_BLOCKS <= 2
PLE_DIM = 256
EPS = 1e-6

V7X_SUBLANES = 8
V7X_LANES = 128
ROW_WORDS = D_MODEL // 2
ROW_SUB = ROW_WORDS // V7X_LANES
RUN_ALIGN = V7X_SUBLANES // ROW_SUB
V7X_VMEM_BYTES = 64 * 1024 * 1024

MIX_TS = 512
HALO = 32
CONV_ROWS = 32
SIDE_CHUNK = 256
MOE_TS = 256
N_STAGE = TOP_K * MOE_TS + N_EXPERTS * (RUN_ALIGN - 1)
STAGE_CHUNK = N_STAGE // 3
assert MIX_TS % MOE_TS == 0 and STAGE_CHUNK * 3 == N_STAGE and STAGE_CHUNK % V7X_SUBLANES == 0

C_POOL = 0
C_CONV = C_POOL + BRANCH_WIDTH
C_SGU_U = C_CONV + 2 * BRANCH_WIDTH
C_SGU_V = C_SGU_U + BRANCH_WIDTH
C_GATE = C_SGU_V + BRANCH_WIDTH
IN_WIDTH = C_GATE + N_BRANCH * D_MODEL
S_POOL = 0
S_SGU_U = S_POOL + BRANCH_WIDTH
S_SGU_V = S_SGU_U + BRANCH_WIDTH
S_GATE = S_SGU_V + BRANCH_WIDTH
SIDE_WIDTH = S_GATE + N_BRANCH * D_MODEL
assert S_SGU_U % SIDE_CHUNK == 0 and SIDE_WIDTH % SIDE_CHUNK == 0


def _rms_norm(x, g):
    return x * lax.rsqrt(jnp.mean(x * x, axis=-1, keepdims=True) + EPS) * g


def _layer_norm(x, g, b):
    mu = jnp.mean(x, axis=-1, keepdims=True)
    xc = x - mu
    var = jnp.mean(xc * xc, axis=-1, keepdims=True)
    return xc * lax.rsqrt(var + EPS) * g + b


def _sigmoid(x):
    return 0.5 * jnp.tanh(0.5 * x) + 0.5


def _dot(a, b):
    return jnp.dot(a, b, preferred_element_type=F32)


def _pack_rows(v):
    bits = lax.bitcast_convert_type(v, U32)
    return (bits[:, :ROW_WORDS] >> 16) | (bits[:, ROW_WORDS:] & jnp.uint32(0xFFFF0000))


def _unpack_rows(w):
    low = lax.bitcast_convert_type(w << 16, F32)
    high = lax.bitcast_convert_type(w & jnp.uint32(0xFFFF0000), F32)
    return jnp.concatenate([low, high], axis=-1).astype(BF16)


def _store_rows(ref_2d, first_row, words):
    n = words.shape[0]
    for j in range(ROW_SUB):
        ref_2d[pl.ds(first_row * ROW_SUB + j, n, stride=ROW_SUB), :] = words[:, j * V7X_LANES:(j + 1) * V7X_LANES]


def _load_rows(ref_2d, first_row, n):
    return jnp.concatenate(
        [ref_2d[pl.ds(first_row * ROW_SUB + j, n, stride=ROW_SUB), :] for j in range(ROW_SUB)], axis=-1)


def _mixer_kernel(x_ref, x_next_ref, mix_g_ref, w_in_ref, b_in_ref, pool_w_ref, pool_scale_ref, conv_w_ref,
                  conv_b_ref, cn_g_ref, cn_b_ref, sn_g_ref, sn_b_ref, sgu_w_ref, sgu_bt_ref,
                  branch_w_ref, branch_b_ref, w_out_ref, moe_g_ref, rw_t_ref, rb_ref,
                  x1_ref, h2b_ref, pos_ref, gate_ref, counts_ref,
                  pool_hist, conv_hist, conv_shift, side_buf, x1_prev, before_buf, h_buf, *, n_seq):
    i = pl.program_id(0)
    n_tiles = pl.num_programs(0) - 1
    route_refs = (before_buf, moe_g_ref, rw_t_ref, rb_ref, h2b_ref, pos_ref, gate_ref, counts_ref)

    @pl.when(i == 0)
    def _():
        x1_prev[...] = jnp.zeros_like(x1_prev)
        h_buf[...] = _rms_norm(x_ref[0], mix_g_ref[...]).astype(BF16)
        t_row = lax.broadcasted_iota(I32, (MIX_TS, MIX_TS), 0)
        t_col = lax.broadcasted_iota(I32, (MIX_TS, MIX_TS), 1)
        same_tile = t_row // MOE_TS == t_col // MOE_TS
        before_buf[...] = jnp.where(same_tile, jnp.where(t_row < t_col, 1.0, 0.0), 0.0).astype(BF16)

    @pl.when(i % n_seq == 0)
    def _():
        pool_hist[0:HALO, :] = jnp.zeros((HALO, BRANCH_WIDTH), F32)
        conv_hist[0:HALO, :] = jnp.zeros((HALO, BRANCH_WIDTH), F32)

    @pl.when(i < n_tiles)
    def _():
        _mix_tile(i % n_seq, x_ref, x_next_ref, mix_g_ref, w_in_ref, b_in_ref, pool_w_ref, pool_scale_ref,
                  conv_w_ref, conv_b_ref, cn_g_ref, cn_b_ref, sn_g_ref, sn_b_ref, sgu_w_ref, sgu_bt_ref,
                  branch_w_ref, branch_b_ref, w_out_ref, x1_ref, pool_hist, conv_hist, conv_shift, side_buf,
                  x1_prev, h_buf, functools.partial(_route_tile, x1_prev, *route_refs))

    @pl.when(i == n_tiles)
    def _():
        _route_tile(x1_prev, *route_refs)


def _mix_tile(s, x_ref, x_next_ref, mix_g_ref, w_in_ref, b_in_ref, pool_w_ref, pool_scale_ref, conv_w_ref,
              conv_b_ref, cn_g_ref, cn_b_ref, sn_g_ref, sn_b_ref, sgu_w_ref, sgu_bt_ref, branch_w_ref,
              branch_b_ref, w_out_ref, x1_ref, pool_hist, conv_hist, conv_shift, side_buf, x1_prev, h_buf,
              route_previous):
    ts = MIX_TS
    x = x_ref[0]
    h = h_buf[...]

    def in_proj(c0, width):
        return _dot(h, w_in_ref[:, c0:c0 + width]) + b_in_ref[:, c0:c0 + width]

    zb = in_proj(C_CONV, 2 * BRANCH_WIDTH)
    conv_hist[HALO:HALO + ts, :] = zb[:, :BRANCH_WIDTH] * _sigmoid(zb[:, BRANCH_WIDTH:])
    n_shift_rows = HALO + ts - V7X_SUBLANES
    for sft in range(1, V7X_SUBLANES):
        conv_shift[sft - 1] = conv_hist[sft:sft + n_shift_rows, :]
    yb_parts = []
    n_conv_blocks = ts // CONV_ROWS
    n_side_chunks = SIDE_WIDTH // SIDE_CHUNK
    for bi in range(n_conv_blocks):
        r0 = bi * CONV_ROWS
        for ci in range(bi * n_side_chunks // n_conv_blocks, (bi + 1) * n_side_chunks // n_conv_blocks):
            d0 = ci * SIDE_CHUNK
            c0 = d0 if d0 < S_SGU_U else d0 + (C_SGU_U - S_SGU_U)
            side_buf[:, d0:d0 + SIDE_CHUNK] = in_proj(c0, SIDE_CHUNK)
        acc = jnp.zeros((CONV_ROWS, BRANCH_WIDTH), F32) + conv_b_ref[...]
        for k in range(CONV_K):
            off = HALO - (CONV_K - 1) + k + r0
            base, sft = off - off % V7X_SUBLANES, off % V7X_SUBLANES
            if sft == 0:
                window = conv_hist[base:base + CONV_ROWS, :]
            else:
                window = conv_shift[sft - 1, base:base + CONV_ROWS, :]
            acc = acc + conv_w_ref[k:k + 1, :] * window
        yb_rows = _layer_norm(acc, cn_g_ref[...], cn_b_ref[...])
        yb_parts.append((yb_rows * _sigmoid(yb_rows)).astype(BF16))
    yb = jnp.concatenate(yb_parts, axis=0)
    conv_hist[0:HALO, :] = conv_hist[ts:ts + HALO, :]

    za = side_buf[:, S_POOL:S_POOL + BRANCH_WIDTH]
    pool_hist[HALO:HALO + ts, :] = za
    row = lax.broadcasted_iota(I32, (ts, 1), 0) + s * ts
    mixed = []
    for g, w in enumerate(POOL_WINDOWS):
        c0 = g * POOL_CH
        cur = za[:, c0:c0 + POOL_CH]
        acc = cur
        for j in range(1, w):
            acc = acc + pool_hist[HALO - j:HALO - j + ts, c0:c0 + POOL_CH]
        count = jnp.minimum(row + 1, w).astype(F32)
        pooled = (acc / count - cur).astype(BF16)
        mixed.append(_dot(pooled, pool_w_ref[g]))
    ya = (jnp.concatenate(mixed, axis=-1) * pool_scale_ref[...]).astype(BF16)
    pool_hist[0:HALO, :] = pool_hist[ts:ts + HALO, :]

    zu = side_buf[:, S_SGU_U:S_SGU_U + BRANCH_WIDTH]
    zv = side_buf[:, S_SGU_V:S_SGU_V + BRANCH_WIDTH]
    v = _layer_norm(zv, sn_g_ref[...], sn_b_ref[...]).astype(BF16)
    tri = (lax.broadcasted_iota(I32, (SGU_CHUNK, SGU_CHUNK), 0)
           >= lax.broadcasted_iota(I32, (SGU_CHUNK, SGU_CHUNK), 1))
    w_tri = [jnp.where(tri, sgu_w_ref[hd], 0.0).astype(BF16) for hd in range(SGU_HEADS)]
    chunks = []
    for c in range(ts // SGU_CHUNK):
        heads = []
        for hd in range(SGU_HEADS):
            vv = v[c * SGU_CHUNK:(c + 1) * SGU_CHUNK, hd * 128:(hd + 1) * 128]
            heads.append(_dot(w_tri[hd], vv) + sgu_bt_ref[:, hd:hd + 1])
        chunks.append(jnp.concatenate(heads, axis=-1))
    yc = (zu * jnp.concatenate(chunks, axis=0)).astype(BF16)

    route_previous()
    h_buf[...] = _rms_norm(x_next_ref[0], mix_g_ref[...]).astype(BF16)

    merged = jnp.zeros((ts, D_MODEL), F32)
    for k, yk in enumerate((ya, yb, yc)):
        proj = _dot(yk, branch_w_ref[k]) + branch_b_ref[k:k + 1, :]
        g0 = S_GATE + k * D_MODEL
        merged = merged + _sigmoid(side_buf[:, g0:g0 + D_MODEL]) * proj
    x1 = x + _dot(merged.astype(BF16), w_out_ref[...])
    x1_ref[0] = x1
    x1_prev[...] = x1


def _route_tile(x1_ref, before_ref, moe_g_ref, rw_t_ref, rb_ref, h2b_ref, pos_ref, gate_ref, counts_ref):
    ts = MIX_TS
    h2 = _rms_norm(x1_ref[...], moe_g_ref[...])
    h2_hi = h2.astype(BF16)
    h2b_ref[...] = h2_hi
    h2_lo = (h2 - h2_hi.astype(F32)).astype(BF16)
    rw = rw_t_ref[...]
    rw_hi = rw.astype(BF16)
    rw_lo = (rw - rw_hi.astype(F32)).astype(BF16)
    nt = (((1,), (1,)), ((), ()))
    logits = (lax.dot_general(rw_hi, h2_hi, nt, preferred_element_type=F32)
              + lax.dot_general(rw_hi, h2_lo, nt, preferred_element_type=F32)
              + lax.dot_general(rw_lo, h2_hi, nt, preferred_element_type=F32)) + rb_ref[...]
    e_iota = lax.broadcasted_iota(I32, (N_EXPERTS, ts), 0).astype(F32)
    vals = logits
    top_v, sels = [], []
    for _k in range(TOP_K):
        m = jnp.max(vals, axis=0, keepdims=True)
        idx = jnp.min(jnp.where(vals == m, e_iota, float(N_EXPERTS)), axis=0, keepdims=True)
        sel = e_iota == idx
        vals = jnp.where(sel, -jnp.inf, vals)
        top_v.append(m)
        sels.append(sel)
    exps = [jnp.exp(tv - top_v[0]) for tv in top_v]
    denom = exps[0] + exps[1] + exps[2] + exps[3]
    chosen = jnp.zeros((N_EXPERTS, ts), F32)
    for sel in sels:
        chosen = chosen + jnp.where(sel, 1.0, 0.0)
    chosen_b = chosen.astype(BF16)
    prefix = _dot(chosen_b, before_ref[...])
    lower = jnp.where(lax.broadcasted_iota(I32, (N_EXPERTS, N_EXPERTS), 0)
                      > lax.broadcasted_iota(I32, (N_EXPERTS, N_EXPERTS), 1), 1.0, 0.0).astype(BF16)
    base_parts = []
    for j in range(ts // MOE_TS):
        lanes = slice(j * MOE_TS, (j + 1) * MOE_TS)
        count = jnp.sum(chosen[:, lanes], axis=1, keepdims=True)
        run_rows = jnp.floor((count + (RUN_ALIGN - 1)) * (1.0 / RUN_ALIGN)) * RUN_ALIGN
        run_start = _dot(lower, jnp.broadcast_to(run_rows, (N_EXPERTS, V7X_LANES)).astype(BF16))[:, 0:1]
        base_parts.append(prefix[:, lanes] + run_start)
        counts_ref[j] = jnp.broadcast_to(count, (N_EXPERTS, V7X_LANES))
    base = jnp.concatenate(base_parts, axis=1)
    zeros4 = jnp.zeros((V7X_SUBLANES - TOP_K, ts), F32)
    pos = [jnp.sum(jnp.where(sel, base, 0.0), axis=0, keepdims=True) for sel in sels]
    pos_ref[...] = jnp.concatenate(pos + [zeros4], axis=0).astype(I32)
    gate_ref[...] = jnp.concatenate([e / denom for e in exps] + [zeros4], axis=0)


def _const_spec(shape, layer):
    nd = len(shape)
    return pl.BlockSpec((None,) + tuple(shape[1:]), lambda *_: (layer,) + (0,) * (nd - 1),
                        pipeline_mode=pl.Buffered(1))


def _mixer_call(layer, x, lw):
    B, S, D = x.shape
    ts = MIX_TS
    n_s = S // ts
    T = B * S
    n_tiles = B * n_s
    mixed = lambda i: jnp.minimum(i, n_tiles - 1)
    routed = lambda i: jnp.maximum(i - 1, 0)
    consts = [lw["mix_norm"], lw["w_in"], lw["b_in"], lw["pool_w"], lw["pool_scale"], lw["conv_w"],
              lw["conv_b"], lw["conv_norm_g"], lw["conv_norm_b"], lw["sgu_norm_g"], lw["sgu_norm_b"],
              lw["sgu_w"], lw["sgu_bt"], lw["branch_w"], lw["branch_b"], lw["w_out"], lw["moe_norm"],
              lw["router_wt"], lw["router_b"]]
    x_spec = pl.BlockSpec((1, ts, D), lambda i: (mixed(i) // n_s, mixed(i) % n_s, 0))
    x_next_spec = pl.BlockSpec((1, ts, D), lambda i: (mixed(i + 1) // n_s, mixed(i + 1) % n_s, 0))
    in_specs = [x_spec, x_next_spec] + [_const_spec(c.shape, layer) for c in consts]
    out_shape = (
        jax.ShapeDtypeStruct((B, S, D), F32),
        jax.ShapeDtypeStruct((T, D), BF16),
        jax.ShapeDtypeStruct((V7X_SUBLANES, T), I32),
        jax.ShapeDtypeStruct((V7X_SUBLANES, T), F32),
        jax.ShapeDtypeStruct((T // MOE_TS, N_EXPERTS, V7X_LANES), F32),
    )
    out_specs = (
        x_spec,
        pl.BlockSpec((ts, D), lambda i: (routed(i), 0)),
        pl.BlockSpec((V7X_SUBLANES, ts), lambda i: (0, routed(i))),
        pl.BlockSpec((V7X_SUBLANES, ts), lambda i: (0, routed(i))),
        pl.BlockSpec((ts // MOE_TS, N_EXPERTS, V7X_LANES), lambda i: (routed(i), 0, 0)),
    )
    return pl.pallas_call(
        functools.partial(_mixer_kernel, n_seq=n_s),
        grid=(n_tiles + 1,),
        in_specs=in_specs,
        out_specs=out_specs,
        out_shape=out_shape,
        scratch_shapes=[pltpu.VMEM((HALO + ts, BRANCH_WIDTH), F32),
                        pltpu.VMEM((HALO + ts, BRANCH_WIDTH), F32),
                        pltpu.VMEM((V7X_SUBLANES - 1, HALO + ts - V7X_SUBLANES, BRANCH_WIDTH), F32),
                        pltpu.VMEM((ts, SIDE_WIDTH), F32),
                        pltpu.VMEM((ts, D), F32),
                        pltpu.VMEM((ts, ts), BF16),
                        pltpu.VMEM((ts, D), BF16)],
        compiler_params=pltpu.CompilerParams(
            dimension_semantics=("arbitrary",),
            vmem_limit_bytes=V7X_VMEM_BYTES - 8 * 1024 * 1024),
        name="mixer",
    )(x, x, *consts)


def _row_span(first_row, n_rows):
    return pl.ds(pl.multiple_of(first_row * ROW_SUB, V7X_SUBLANES),
                 pl.multiple_of(n_rows * ROW_SUB, V7X_SUBLANES))


def _dispatch_kernel(run_src, run_n, run_dst, tile_rows, pad_dst, pad_n, n_used_ref, h2b_ref, pos_ref,
                     xs_hbm, stage, zero_rows, run_sem, fill_sem, *, n_blocks):
    i = pl.program_id(0)
    n_tiles = pl.num_programs(0)
    slot = i % 2
    block_rows = MOE_BLOCK * ROW_SUB

    def run_copy(tile, e, buf, live=1):
        k = tile * N_EXPERTS + e
        n = run_n[k] * live
        return pltpu.make_async_copy(stage.at[buf, _row_span(run_src[k], n)],
                                     xs_hbm.at[_row_span(run_dst[k], n)], run_sem.at[buf])

    def wait_runs(tile, buf):
        n = tile_rows[tile]
        pltpu.make_async_copy(stage.at[buf, _row_span(0, n)], xs_hbm.at[_row_span(0, n)],
                              run_sem.at[buf]).wait()

    def pad_copy(e):
        return pltpu.make_async_copy(zero_rows.at[_row_span(0, pad_n[e])],
                                     xs_hbm.at[_row_span(pad_dst[e], pad_n[e])], fill_sem)

    def idle_block_copy(blk):
        return pltpu.make_async_copy(
            zero_rows, xs_hbm.at[pl.ds(pl.multiple_of(blk * block_rows, block_rows), block_rows)], fill_sem)

    @pl.when(i == 0)
    def _():
        zero_rows[...] = jnp.zeros_like(zero_rows)

        def start_idle(blk, c):
            idle_block_copy(blk).start()
            return c

        def wait_idle(blk, c):
            idle_block_copy(blk).wait()
            return c

        for e in range(N_EXPERTS):
            pad_copy(e).start()
        lax.fori_loop(n_used_ref[0], n_blocks, start_idle, 0)
        for e in range(N_EXPERTS):
            pad_copy(e).wait()
        lax.fori_loop(n_used_ref[0], n_blocks, wait_idle, 0)

    @pl.when(i >= 2)
    def _():
        wait_runs(i - 2, slot)

    prev_live = jnp.minimum(i, 1)
    for e in range(N_EXPERTS):
        run_copy(jnp.maximum(i - 1, 0), e, 1 - slot, prev_live).start()

    h2b = h2b_ref[...]
    for a0 in range(0, N_STAGE, STAGE_CHUNK):
        a_iota = lax.broadcasted_iota(I32, (STAGE_CHUNK, MOE_TS), 0) + a0
        onehot = jnp.zeros((STAGE_CHUNK, MOE_TS), F32)
        for k in range(TOP_K):
            onehot = jnp.where(a_iota == pos_ref[k:k + 1, :], 1.0, onehot)
        _store_rows(stage.at[slot], a0, _pack_rows(_dot(onehot.astype(BF16), h2b)))

    @pl.when(i == n_tiles - 1)
    def _():
        for e in range(N_EXPERTS):
            run_copy(i, e, slot).start()

        @pl.when(i >= 1)
        def _():
            wait_runs(i - 1, 1 - slot)
        wait_runs(i, slot)


def _dispatch_call(h2b, pos8, tables, n_blocks):
    T, D = h2b.shape
    n_tiles = T // MOE_TS
    grid_spec = pltpu.PrefetchScalarGridSpec(
        num_scalar_prefetch=7,
        grid=(n_tiles,),
        in_specs=[pl.BlockSpec((MOE_TS, D), lambda i, *_: (i, 0)),
                  pl.BlockSpec((V7X_SUBLANES, MOE_TS), lambda i, *_: (0, i))],
        out_specs=pl.BlockSpec(memory_space=pl.ANY),
        scratch_shapes=[pltpu.VMEM((2, N_STAGE * ROW_SUB, V7X_LANES), U32),
                        pltpu.VMEM((MOE_BLOCK * ROW_SUB, V7X_LANES), U32),
                        pltpu.SemaphoreType.DMA((2,)),
                        pltpu.SemaphoreType.DMA(())],
    )
    return pl.pallas_call(
        functools.partial(_dispatch_kernel, n_blocks=n_blocks),
        grid_spec=grid_spec,
        out_shape=jax.ShapeDtypeStruct((n_blocks * MOE_BLOCK * ROW_SUB, V7X_LANES), U32),
        compiler_params=pltpu.CompilerParams(
            dimension_semantics=("arbitrary",),
            vmem_limit_bytes=V7X_VMEM_BYTES - 16 * 1024 * 1024),
        name="dispatch",
    )(tables["run_src"], tables["run_n"], tables["run_dst"], tables["tile_rows"], tables["pad_dst"],
      tables["pad_n"], tables["n_used"], h2b, pos8)


def _moe_kernel(be_ref, group_end_ref, n_used_ref, xs_ref, wup_hbm, wdn_hbm, *rest, layer):
    bias_refs = rest[:2 * MOE_STEP_BLOCKS]
    ys_ref, wup_f32, wdn_f32, wup_bf, wdn_bf, group_count, wsem = rest[2 * MOE_STEP_BLOCKS:]
    step = pl.program_id(0)
    n_used = n_used_ref[0]

    def weight_copies(e, buf):
        return (pltpu.make_async_copy(wup_hbm.at[layer, e], wup_f32.at[buf], wsem.at[buf, 0]),
                pltpu.make_async_copy(wdn_hbm.at[layer, e], wdn_f32.at[buf], wsem.at[buf, 1]))

    @pl.when(step == 0)
    def _():
        group_count[0] = 0
        for c in weight_copies(be_ref[0], 0):
            c.start()

    halves = []
    for half in range(MOE_STEP_BLOCKS):
        blk = step * MOE_STEP_BLOCKS + half
        expert = be_ref[blk]
        new_expert = (blk == 0) | (expert != be_ref[jnp.maximum(blk - 1, 0)])

        @pl.when((blk < n_used) & new_expert)
        def _(expert=expert):
            par = group_count[0] % 2
            group_count[0] = group_count[0] + 1
            next_blk = group_end_ref[expert]

            @pl.when(next_blk < n_used)
            def _():
                for c in weight_copies(be_ref[next_blk], 1 - par):
                    c.start()

            for c in weight_copies(expert, par):
                c.wait()
            wup_bf[par] = wup_f32[par].astype(BF16)
            wdn_bf[par] = wdn_f32[par].astype(BF16)

        halves.append((group_count[0] - 1) % 2)

    @pl.when(step * MOE_STEP_BLOCKS < n_used)
    def _():
        for half, par in enumerate(halves):
            bup_ref, bdn_ref = bias_refs[2 * half], bias_refs[2 * half + 1]
            r0 = half * MOE_BLOCK
            xs = _unpack_rows(_load_rows(xs_ref, r0, MOE_BLOCK))
            gu = _dot(xs, wup_bf[par]) + bup_ref[0, 0]
            x_glu = jnp.minimum(gu[:, :D_FF], SWIGLU_LIMIT)
            x_lin = jnp.clip(gu[:, D_FF:], -SWIGLU_LIMIT, SWIGLU_LIMIT)
            act = x_glu * _sigmoid(SWIGLU_ALPHA * x_glu) * (x_lin + 1.0)
            y = _dot(act.astype(BF16), wdn_bf[par]) + bdn_ref[0, 0]
            _store_rows(ys_ref, r0, _pack_rows(y.astype(BF16).astype(F32)))

    @pl.when(step * MOE_STEP_BLOCKS >= n_used)
    def _():
        ys_ref[...] = jnp.zeros_like(ys_ref)


def _moe_call(layer, xs, tables, w_up, b_up, w_down, b_down):
    n_blocks = tables["block_e"].shape[0]
    assert n_blocks % MOE_STEP_BLOCKS == 0
    step_rows = MOE_STEP_BLOCKS * MOE_BLOCK * ROW_SUB
    bias_spec = lambda width, half: pl.BlockSpec(
        (1, 1, 1, width), lambda s, be, *_: (layer, be[s * MOE_STEP_BLOCKS + half], 0, 0))
    grid_spec = pltpu.PrefetchScalarGridSpec(
        num_scalar_prefetch=3,
        grid=(n_blocks // MOE_STEP_BLOCKS,),
        in_specs=[
            pl.BlockSpec((step_rows, V7X_LANES), lambda s, *_: (s, 0)),
            pl.BlockSpec(memory_space=pl.ANY),
            pl.BlockSpec(memory_space=pl.ANY),
        ] + [bias_spec(width, half) for half in range(MOE_STEP_BLOCKS) for width in (2 * D_FF, D_MODEL)],
        out_specs=pl.BlockSpec((step_rows, V7X_LANES), lambda s, *_: (s, 0)),
        scratch_shapes=[pltpu.VMEM((2, D_MODEL, 2 * D_FF), F32), pltpu.VMEM((2, D_FF, D_MODEL), F32),
                        pltpu.VMEM((2, D_MODEL, 2 * D_FF), BF16), pltpu.VMEM((2, D_FF, D_MODEL), BF16),
                        pltpu.SMEM((1,), I32), pltpu.SemaphoreType.DMA((2, 2))],
    )
    return pl.pallas_call(
        functools.partial(_moe_kernel, layer=layer),
        grid_spec=grid_spec,
        out_shape=jax.ShapeDtypeStruct(xs.shape, U32),
        compiler_params=pltpu.CompilerParams(
            dimension_semantics=("arbitrary",),
            vmem_limit_bytes=V7X_VMEM_BYTES - 8 * 1024 * 1024),
        name="moe",
    )(tables["block_e"], tables["group_end"], tables["n_used"], xs, w_up, w_down,
      *([b_up, b_down] * MOE_STEP_BLOCKS))


def _combine_kernel(run_src, run_n, run_dst, tile_rows, x1_ref, pos_ref, gate_ref, p_ref, ple_g_ref,
                    gate_w_ref, proj_w_ref, fin_g_ref, ys_hbm, out_ref, stage, x2_buf, run_sem, *, last):
    i = pl.program_id(0)
    n_tiles = pl.num_programs(0) - 1
    slot = i % 2
    tile = jnp.minimum(i, n_tiles - 1)
    live = jnp.where(i < n_tiles, 1, 0)

    def run_copy(tile, e, buf, live=1):
        k = tile * N_EXPERTS + e
        n = run_n[k] * live
        return pltpu.make_async_copy(ys_hbm.at[_row_span(run_dst[k], n)],
                                     stage.at[buf, _row_span(run_src[k], n)], run_sem.at[buf])

    @pl.when(i == 0)
    def _():
        stage[...] = jnp.zeros_like(stage)
        x2_buf[...] = jnp.zeros_like(x2_buf)
        for e in range(N_EXPERTS):
            run_copy(0, e, 0).start()

    n_rows = tile_rows[tile] * live
    pltpu.make_async_copy(ys_hbm.at[_row_span(0, n_rows)], stage.at[slot, _row_span(0, n_rows)],
                          run_sem.at[slot]).wait()

    next_live = jnp.where(i + 1 < n_tiles, 1, 0)
    for e in range(N_EXPERTS):
        run_copy(jnp.minimum(i + 1, n_tiles - 1), e, 1 - slot, next_live).start()

    x2p = x2_buf[...]
    h3 = _rms_norm(x2p, ple_g_ref[...]).astype(BF16)
    g = _sigmoid(_dot(h3, gate_w_ref[...]))
    pp = _dot(p_ref[0, 0].astype(BF16), proj_w_ref[...])
    x3 = x2p + g * pp
    if last:
        x3 = _rms_norm(x3, fin_g_ref[...])
    out_ref[...] = x3

    x2 = x1_ref[...]
    for a0 in range(0, N_STAGE, STAGE_CHUNK):
        a_iota = lax.broadcasted_iota(I32, (STAGE_CHUNK, MOE_TS), 0) + a0
        weights_t = jnp.zeros((STAGE_CHUNK, MOE_TS), F32)
        for k in range(TOP_K):
            weights_t = jnp.where(a_iota == pos_ref[k:k + 1, :], gate_ref[k:k + 1, :], weights_t)
        y_sorted = _unpack_rows(_load_rows(stage.at[slot], a0, STAGE_CHUNK))
        x2 = x2 + lax.dot_general(weights_t.astype(BF16), y_sorted, (((0,), (0,)), ((), ())),
                                  preferred_element_type=F32)
    x2_buf[...] = x2


def _combine_call(layer, x1, ys, pos8, gate8, p, tables, lw, fin_g, last):
    T, D = x1.shape
    n_tiles = T // MOE_TS
    n_seq = p.shape[2] // MOE_TS
    consts = [lw["ple_norm"], lw["ple_gate_w"], lw["ple_proj_w"]]
    gathered = lambda i: jnp.minimum(i, n_tiles - 1)
    finished = lambda i: jnp.maximum(i - 1, 0)
    grid_spec = pltpu.PrefetchScalarGridSpec(
        num_scalar_prefetch=4,
        grid=(n_tiles + 1,),
        in_specs=[pl.BlockSpec((MOE_TS, D), lambda i, *_: (gathered(i), 0)),
                  pl.BlockSpec((V7X_SUBLANES, MOE_TS), lambda i, *_: (0, gathered(i))),
                  pl.BlockSpec((V7X_SUBLANES, MOE_TS), lambda i, *_: (0, gathered(i))),
                  pl.BlockSpec((1, 1, MOE_TS, PLE_DIM),
                               lambda i, *_: (layer, finished(i) // n_seq, finished(i) % n_seq, 0))]
        + [_const_spec(c.shape, layer) for c in consts] + [_const_spec(fin_g.shape, 0)]
        + [pl.BlockSpec(memory_space=pl.ANY)],
        out_specs=pl.BlockSpec((MOE_TS, D), lambda i, *_: (finished(i), 0)),
        scratch_shapes=[pltpu.VMEM((2, N_STAGE * ROW_SUB, V7X_LANES), U32),
                        pltpu.VMEM((MOE_TS, D), F32),
                        pltpu.SemaphoreType.DMA((2,))],
    )
    return pl.pallas_call(
        functools.partial(_combine_kernel, last=last),
        grid_spec=grid_spec,
        out_shape=jax.ShapeDtypeStruct((T, D), F32),
        compiler_params=pltpu.CompilerParams(
            dimension_semantics=("arbitrary",),
            vmem_limit_bytes=V7X_VMEM_BYTES - 16 * 1024 * 1024),
        name="combine",
    )(tables["run_src"], tables["run_n"], tables["run_dst"], tables["tile_rows"], x1, pos8, gate8, p,
      *consts, fin_g, ys)


def _routing_tables(tile_counts, n_blocks):
    c = tile_counts[:, :, 0].astype(I32)
    c = (c + RUN_ALIGN - 1) // RUN_ALIGN * RUN_ALIGN
    counts = jnp.sum(c, axis=0)
    padded = (counts + MOE_BLOCK - 1) // MOE_BLOCK * MOE_BLOCK
    pad_end = jnp.cumsum(padded)
    pad_start = pad_end - padded
    run_dst = pad_start[None, :] + jnp.cumsum(c, axis=0) - c
    run_src = jnp.cumsum(c, axis=1) - c
    n_used = pad_end[-1] // MOE_BLOCK
    blk0 = jnp.arange(n_blocks, dtype=I32) * MOE_BLOCK
    be = jnp.minimum(jnp.sum(blk0[:, None] >= pad_end[None, :], axis=1), N_EXPERTS - 1)
    be = be[jnp.minimum(jnp.arange(n_blocks), n_used - 1)]
    return {
        "group_end": (pad_end // MOE_BLOCK).astype(I32),
        "run_src": run_src.reshape(-1).astype(I32), "run_n": c.reshape(-1),
        "run_dst": run_dst.reshape(-1).astype(I32), "tile_rows": jnp.sum(c, axis=1).astype(I32),
        "pad_dst": (pad_start + counts).astype(I32), "pad_n": (padded - counts).astype(I32),
        "n_used": n_used.reshape(1).astype(I32), "block_e": be.astype(I32),
    }


def kernel(x, p, mix_norm, w_in, b_in, pool_w, pool_scale, conv_w, conv_b, conv_norm_g, conv_norm_b,
           sgu_norm_g, sgu_norm_b, sgu_w, sgu_b, branch_w, branch_b, w_out, moe_norm, router_w,
           router_b, expert_w_up, expert_b_up, expert_w_down, expert_b_down, ple_norm, ple_gate_w,
           ple_proj_w, final_norm):
    B, S, D = x.shape
    T = B * S
    depth = w_in.shape[0]
    assert D == D_MODEL and S % MIX_TS == 0 and w_in.shape[2] == IN_WIDTH
    max_rows = T * TOP_K + (T // MOE_TS) * N_EXPERTS * (RUN_ALIGN - 1)
    n_blocks = -(-max_rows // MOE_BLOCK) + N_EXPERTS
    n_blocks += -n_blocks % MOE_STEP_BLOCKS
    rows = lambda a: a[:, None, :]
    b_up4 = expert_b_up[:, :, None, :]
    b_down4 = expert_b_down[:, :, None, :]
    lw = {
        "mix_norm": rows(mix_norm), "w_in": w_in.astype(BF16), "b_in": rows(b_in),
        "pool_w": pool_w.astype(BF16), "pool_scale": rows(pool_scale),
        "conv_w": conv_w, "conv_b": rows(conv_b),
        "conv_norm_g": rows(conv_norm_g), "conv_norm_b": rows(conv_norm_b),
        "sgu_norm_g": rows(sgu_norm_g), "sgu_norm_b": rows(sgu_norm_b),
        "sgu_w": sgu_w, "sgu_bt": jnp.swapaxes(sgu_b, 1, 2),
        "branch_w": branch_w.astype(BF16), "branch_b": branch_b,
        "w_out": w_out.astype(BF16), "moe_norm": rows(moe_norm),
        "router_wt": jnp.swapaxes(router_w, 1, 2), "router_b": router_b[:, :, None],
        "ple_norm": rows(ple_norm), "ple_gate_w": ple_gate_w.astype(BF16),
        "ple_proj_w": ple_proj_w.astype(BF16),
    }
    fin_g = final_norm.reshape(1, 1, D)
    for i in range(depth):
        x1, h2b, pos8, gate8, tile_counts = _mixer_call(i, x, lw)
        tables = _routing_tables(tile_counts, n_blocks)
        xs = _dispatch_call(h2b, pos8, tables, n_blocks)
        ys = _moe_call(i, xs, tables, expert_w_up, b_up4, expert_w_down, b_down4)
        x = _combine_call(i, x1.reshape(T, D), ys, pos8, gate8, p, tables, lw, fin_g,
                          last=(i == depth - 1)).reshape(B, S, D)
    return x
```

```python
import functools

import jax
import jax.numpy as jnp
from jax import lax
from jax.experimental import pallas as pl
from jax.experimental.pallas import tpu as pltpu

F32 = jnp.float32
BF16 = jnp.bfloat16
I32 = jnp.int32
U32 = jnp.uint32

D_MODEL = 1024
POOL_WINDOWS = (2, 4, 8, 16)
POOL_CH = 128
BRANCH_WIDTH = 512
CONV_K = 31
SGU_CHUNK = 128
SGU_HEADS = 4
N_BRANCH = 3
N_EXPERTS = 32
TOP_K = 4
D_FF = 1024
SWIGLU_LIMIT = 7.0
SWIGLU_ALPHA = 1.702
MOE_BLOCK = 256
MOE_STEP_BLOCKS = 2
assert MOE_STEP_BLOCKS <= 2
PLE_DIM = 256
EPS = 1e-6

V7X_SUBLANES = 8
V7X_LANES = 128
ROW_WORDS = D_MODEL // 2
ROW_SUB = ROW_WORDS // V7X_LANES
RUN_ALIGN = V7X_SUBLANES // ROW_SUB
V7X_VMEM_BYTES = 64 * 1024 * 1024

MIX_TS = 512
HALO = 32
CONV_ROWS = 32
SIDE_CHUNK = 256
MOE_TS = 256
N_STAGE = TOP_K * MOE_TS + N_EXPERTS * (RUN_ALIGN - 1)
STAGE_CHUNK = N_STAGE // 3
assert MIX_TS % MOE_TS == 0 and STAGE_CHUNK * 3 == N_STAGE and STAGE_CHUNK % V7X_SUBLANES == 0

C_POOL = 0
C_CONV = C_POOL + BRANCH_WIDTH
C_SGU_U = C_CONV + 2 * BRANCH_WIDTH
C_SGU_V = C_SGU_U + BRANCH_WIDTH
C_GATE = C_SGU_V + BRANCH_WIDTH
IN_WIDTH = C_GATE + N_BRANCH * D_MODEL
S_POOL = 0
S_SGU_U = S_POOL + BRANCH_WIDTH
S_SGU_V = S_SGU_U + BRANCH_WIDTH
S_GATE = S_SGU_V + BRANCH_WIDTH
SIDE_WIDTH = S_GATE + N_BRANCH * D_MODEL
assert S_SGU_U % SIDE_CHUNK == 0 and SIDE_WIDTH % SIDE_CHUNK == 0


def _rms_norm(x, g):
    return x * lax.rsqrt(jnp.mean(x * x, axis=-1, keepdims=True) + EPS) * g


def _layer_norm(x, g, b):
    mu = jnp.mean(x, axis=-1, keepdims=True)
    xc = x - mu
    var = jnp.mean(xc * xc, axis=-1, keepdims=True)
    return xc * lax.rsqrt(var + EPS) * g + b


def _sigmoid(x):
    return 0.5 * jnp.tanh(0.5 * x) + 0.5


def _dot(a, b):
    return jnp.dot(a, b, preferred_element_type=F32)


def _pack_rows(v):
    bits = lax.bitcast_convert_type(v, U32)
    return (bits[:, :ROW_WORDS] >> 16) | (bits[:, ROW_WORDS:] & jnp.uint32(0xFFFF0000))


def _unpack_rows(w):
    low = lax.bitcast_convert_type(w << 16, F32)
    high = lax.bitcast_convert_type(w & jnp.uint32(0xFFFF0000), F32)
    return jnp.concatenate([low, high], axis=-1).astype(BF16)


def _store_rows(ref_2d, first_row, words):
    n = words.shape[0]
    for j in range(ROW_SUB):
        ref_2d[pl.ds(first_row * ROW_SUB + j, n, stride=ROW_SUB), :] = words[:, j * V7X_LANES:(j + 1) * V7X_LANES]


def _load_rows(ref_2d, first_row, n):
    return jnp.concatenate(
        [ref_2d[pl.ds(first_row * ROW_SUB + j, n, stride=ROW_SUB), :] for j in range(ROW_SUB)], axis=-1)


def _mixer_kernel(x_ref, x_next_ref, mix_g_ref, w_in_ref, b_in_ref, pool_w_ref, pool_scale_ref, conv_w_ref,
                  conv_b_ref, cn_g_ref, cn_b_ref, sn_g_ref, sn_b_ref, sgu_w_ref, sgu_bt_ref,
                  branch_w_ref, branch_b_ref, w_out_ref, moe_g_ref, rw_t_ref, rb_ref,
                  x1_ref, h2b_ref, pos_ref, gate_ref, counts_ref,
                  pool_hist, conv_hist, conv_shift, side_buf, x1_prev, before_buf, h_buf, *, n_seq):
    i = pl.program_id(0)
    n_tiles = pl.num_programs(0) - 1
    route_refs = (before_buf, moe_g_ref, rw_t_ref, rb_ref, h2b_ref, pos_ref, gate_ref, counts_ref)

    @pl.when(i == 0)
    def _():
        x1_prev[...] = jnp.zeros_like(x1_prev)
        h_buf[...] = _rms_norm(x_ref[0], mix_g_ref[...]).astype(BF16)
        t_row = lax.broadcasted_iota(I32, (MIX_TS, MIX_TS), 0)
        t_col = lax.broadcasted_iota(I32, (MIX_TS, MIX_TS), 1)
        same_tile = t_row // MOE_TS == t_col // MOE_TS
        before_buf[...] = jnp.where(same_tile, jnp.where(t_row < t_col, 1.0, 0.0), 0.0).astype(BF16)

    @pl.when(i % n_seq == 0)
    def _():
        pool_hist[0:HALO, :] = jnp.zeros((HALO, BRANCH_WIDTH), F32)
        conv_hist[0:HALO, :] = jnp.zeros((HALO, BRANCH_WIDTH), F32)

    @pl.when(i < n_tiles)
    def _():
        _mix_tile(i % n_seq, x_ref, x_next_ref, mix_g_ref, w_in_ref, b_in_ref, pool_w_ref, pool_scale_ref,
                  conv_w_ref, conv_b_ref, cn_g_ref, cn_b_ref, sn_g_ref, sn_b_ref, sgu_w_ref, sgu_bt_ref,
                  branch_w_ref, branch_b_ref, w_out_ref, x1_ref, pool_hist, conv_hist, conv_shift, side_buf,
                  x1_prev, h_buf, functools.partial(_route_tile, x1_prev, *route_refs))

    @pl.when(i == n_tiles)
    def _():
        _route_tile(x1_prev, *route_refs)


def _mix_tile(s, x_ref, x_next_ref, mix_g_ref, w_in_ref, b_in_ref, pool_w_ref, pool_scale_ref, conv_w_ref,
              conv_b_ref, cn_g_ref, cn_b_ref, sn_g_ref, sn_b_ref, sgu_w_ref, sgu_bt_ref, branch_w_ref,
              branch_b_ref, w_out_ref, x1_ref, pool_hist, conv_hist, conv_shift, side_buf, x1_prev, h_buf,
              route_previous):
    ts = MIX_TS
    x = x_ref[0]
    h = h_buf[...]

    def in_proj(c0, width):
        return _dot(h, w_in_ref[:, c0:c0 + width]) + b_in_ref[:, c0:c0 + width]

    zb = in_proj(C_CONV, 2 * BRANCH_WIDTH)
    conv_hist[HALO:HALO + ts, :] = zb[:, :BRANCH_WIDTH] * _sigmoid(zb[:, BRANCH_WIDTH:])
    n_shift_rows = HALO + ts - V7X_SUBLANES
    for sft in range(1, V7X_SUBLANES):
        conv_shift[sft - 1] = conv_hist[sft:sft + n_shift_rows, :]
    yb_parts = []
    n_conv_blocks = ts // CONV_ROWS
    n_side_chunks = SIDE_WIDTH // SIDE_CHUNK
    for bi in range(n_conv_blocks):
        r0 = bi * CONV_ROWS
        for ci in range(bi * n_side_chunks // n_conv_blocks, (bi + 1) * n_side_chunks // n_conv_blocks):
            d0 = ci * SIDE_CHUNK
            c0 = d0 if d0 < S_SGU_U else d0 + (C_SGU_U - S_SGU_U)
            side_buf[:, d0:d0 + SIDE_CHUNK] = in_proj(c0, SIDE_CHUNK)
        acc = jnp.zeros((CONV_ROWS, BRANCH_WIDTH), F32) + conv_b_ref[...]
        for k in range(CONV_K):
            off = HALO - (CONV_K - 1) + k + r0
            base, sft = off - off % V7X_SUBLANES, off % V7X_SUBLANES
            if sft == 0:
                window = conv_hist[base:base + CONV_ROWS, :]
            else:
                window = conv_shift[sft - 1, base:base + CONV_ROWS, :]
            acc = acc + conv_w_ref[k:k + 1, :] * window
        yb_rows = _layer_norm(acc, cn_g_ref[...], cn_b_ref[...])
        yb_parts.append((yb_rows * _sigmoid(yb_rows)).astype(BF16))
    yb = jnp.concatenate(yb_parts, axis=0)
    conv_hist[0:HALO, :] = conv_hist[ts:ts + HALO, :]

    za = side_buf[:, S_POOL:S_POOL + BRANCH_WIDTH]
    pool_hist[HALO:HALO + ts, :] = za
    row = lax.broadcasted_iota(I32, (ts, 1), 0) + s * ts
    mixed = []
    for g, w in enumerate(POOL_WINDOWS):
        c0 = g * POOL_CH
        cur = za[:, c0:c0 + POOL_CH]
        acc = cur
        for j in range(1, w):
            acc = acc + pool_hist[HALO - j:HALO - j + ts, c0:c0 + POOL_CH]
        count = jnp.minimum(row + 1, w).astype(F32)
        pooled = (acc / count - cur).astype(BF16)
        mixed.append(_dot(pooled, pool_w_ref[g]))
    ya = (jnp.concatenate(mixed, axis=-1) * pool_scale_ref[...]).astype(BF16)
    pool_hist[0:HALO, :] = pool_hist[ts:ts + HALO, :]

    zu = side_buf[:, S_SGU_U:S_SGU_U + BRANCH_WIDTH]
    zv = side_buf[:, S_SGU_V:S_SGU_V + BRANCH_WIDTH]
    v = _layer_norm(zv, sn_g_ref[...], sn_b_ref[...]).astype(BF16)
    tri = (lax.broadcasted_iota(I32, (SGU_CHUNK, SGU_CHUNK), 0)
           >= lax.broadcasted_iota(I32, (SGU_CHUNK, SGU_CHUNK), 1))
    w_tri = [jnp.where(tri, sgu_w_ref[hd], 0.0).astype(BF16) for hd in range(SGU_HEADS)]
    chunks = []
    for c in range(ts // SGU_CHUNK):
        heads = []
        for hd in range(SGU_HEADS):
            vv = v[c * SGU_CHUNK:(c + 1) * SGU_CHUNK, hd * 128:(hd + 1) * 128]
            heads.append(_dot(w_tri[hd], vv) + sgu_bt_ref[:, hd:hd + 1])
        chunks.append(jnp.concatenate(heads, axis=-1))
    yc = (zu * jnp.concatenate(chunks, axis=0)).astype(BF16)

    route_previous()
    h_buf[...] = _rms_norm(x_next_ref[0], mix_g_ref[...]).astype(BF16)

    merged = jnp.zeros((ts, D_MODEL), F32)
    for k, yk in enumerate((ya, yb, yc)):
        proj = _dot(yk, branch_w_ref[k]) + branch_b_ref[k:k + 1, :]
        g0 = S_GATE + k * D_MODEL
        merged = merged + _sigmoid(side_buf[:, g0:g0 + D_MODEL]) * proj
    x1 = x + _dot(merged.astype(BF16), w_out_ref[...])
    x1_ref[0] = x1
    x1_prev[...] = x1


def _route_tile(x1_ref, before_ref, moe_g_ref, rw_t_ref, rb_ref, h2b_ref, pos_ref, gate_ref, counts_ref):
    ts = MIX_TS
    h2 = _rms_norm(x1_ref[...], moe_g_ref[...])
    h2_hi = h2.astype(BF16)
    h2b_ref[...] = h2_hi
    h2_lo = (h2 - h2_hi.astype(F32)).astype(BF16)
    rw = rw_t_ref[...]
    rw_hi = rw.astype(BF16)
    rw_lo = (rw - rw_hi.astype(F32)).astype(BF16)
    nt = (((1,), (1,)), ((), ()))
    logits = (lax.dot_general(rw_hi, h2_hi, nt, preferred_element_type=F32)
              + lax.dot_general(rw_hi, h2_lo, nt, preferred_element_type=F32)
              + lax.dot_general(rw_lo, h2_hi, nt, preferred_element_type=F32)) + rb_ref[...]
    e_iota = lax.broadcasted_iota(I32, (N_EXPERTS, ts), 0).astype(F32)
    vals = logits
    top_v, sels = [], []
    for _k in range(TOP_K):
        m = jnp.max(vals, axis=0, keepdims=True)
        idx = jnp.min(jnp.where(vals == m, e_iota, float(N_EXPERTS)), axis=0, keepdims=True)
        sel = e_iota == idx
        vals = jnp.where(sel, -jnp.inf, vals)
        top_v.append(m)
        sels.append(sel)
    exps = [jnp.exp(tv - top_v[0]) for tv in top_v]
    denom = exps[0] + exps[1] + exps[2] + exps[3]
    chosen = jnp.zeros((N_EXPERTS, ts), F32)
    for sel in sels:
        chosen = chosen + jnp.where(sel, 1.0, 0.0)
    chosen_b = chosen.astype(BF16)
    prefix = _dot(chosen_b, before_ref[...])
    lower = jnp.where(lax.broadcasted_iota(I32, (N_EXPERTS, N_EXPERTS), 0)
                      > lax.broadcasted_iota(I32, (N_EXPERTS, N_EXPERTS), 1), 1.0, 0.0).astype(BF16)
    base_parts = []
    for j in range(ts // MOE_TS):
        lanes = slice(j * MOE_TS, (j + 1) * MOE_TS)
        count = jnp.sum(chosen[:, lanes], axis=1, keepdims=True)
        run_rows = jnp.floor((count + (RUN_ALIGN - 1)) * (1.0 / RUN_ALIGN)) * RUN_ALIGN
        run_start = _dot(lower, jnp.broadcast_to(run_rows, (N_EXPERTS, V7X_LANES)).astype(BF16))[:, 0:1]
        base_parts.append(prefix[:, lanes] + run_start)
        counts_ref[j] = jnp.broadcast_to(count, (N_EXPERTS, V7X_LANES))
    base = jnp.concatenate(base_parts, axis=1)
    zeros4 = jnp.zeros((V7X_SUBLANES - TOP_K, ts), F32)
    pos = [jnp.sum(jnp.where(sel, base, 0.0), axis=0, keepdims=True) for sel in sels]
    pos_ref[...] = jnp.concatenate(pos + [zeros4], axis=0).astype(I32)
    gate_ref[...] = jnp.concatenate([e / denom for e in exps] + [zeros4], axis=0)


def _const_spec(shape, layer):
    nd = len(shape)
    return pl.BlockSpec((None,) + tuple(shape[1:]), lambda *_: (layer,) + (0,) * (nd - 1),
                        pipeline_mode=pl.Buffered(1))


def _mixer_call(layer, x, lw):
    B, S, D = x.shape
    ts = MIX_TS
    n_s = S // ts
    T = B * S
    n_tiles = B * n_s
    mixed = lambda i: jnp.minimum(i, n_tiles - 1)
    routed = lambda i: jnp.maximum(i - 1, 0)
    consts = [lw["mix_norm"], lw["w_in"], lw["b_in"], lw["pool_w"], lw["pool_scale"], lw["conv_w"],
              lw["conv_b"], lw["conv_norm_g"], lw["conv_norm_b"], lw["sgu_norm_g"], lw["sgu_norm_b"],
              lw["sgu_w"], lw["sgu_bt"], lw["branch_w"], lw["branch_b"], lw["w_out"], lw["moe_norm"],
              lw["router_wt"], lw["router_b"]]
    x_spec = pl.BlockSpec((1, ts, D), lambda i: (mixed(i) // n_s, mixed(i) % n_s, 0))
    x_next_spec = pl.BlockSpec((1, ts, D), lambda i: (mixed(i + 1) // n_s, mixed(i + 1) % n_s, 0))
    in_specs = [x_spec, x_next_spec] + [_const_spec(c.shape, layer) for c in consts]
    out_shape = (
        jax.ShapeDtypeStruct((B, S, D), F32),
        jax.ShapeDtypeStruct((T, D), BF16),
        jax.ShapeDtypeStruct((V7X_SUBLANES, T), I32),
        jax.ShapeDtypeStruct((V7X_SUBLANES, T), F32),
        jax.ShapeDtypeStruct((T // MOE_TS, N_EXPERTS, V7X_LANES), F32),
    )
    out_specs = (
        x_spec,
        pl.BlockSpec((ts, D), lambda i: (routed(i), 0)),
        pl.BlockSpec((V7X_SUBLANES, ts), lambda i: (0, routed(i))),
        pl.BlockSpec((V7X_SUBLANES, ts), lambda i: (0, routed(i))),
        pl.BlockSpec((ts // MOE_TS, N_EXPERTS, V7X_LANES), lambda i: (routed(i), 0, 0)),
    )
    return pl.pallas_call(
        functools.partial(_mixer_kernel, n_seq=n_s),
        grid=(n_tiles + 1,),
        in_specs=in_specs,
        out_specs=out_specs,
        out_shape=out_shape,
        scratch_shapes=[pltpu.VMEM((HALO + ts, BRANCH_WIDTH), F32),
                        pltpu.VMEM((HALO + ts, BRANCH_WIDTH), F32),
                        pltpu.VMEM((V7X_SUBLANES - 1, HALO + ts - V7X_SUBLANES, BRANCH_WIDTH), F32),
                        pltpu.VMEM((ts, SIDE_WIDTH), F32),
                        pltpu.VMEM((ts, D), F32),
                        pltpu.VMEM((ts, ts), BF16),
                        pltpu.VMEM((ts, D), BF16)],
        compiler_params=pltpu.CompilerParams(
            dimension_semantics=("arbitrary",),
            vmem_limit_bytes=V7X_VMEM_BYTES - 8 * 1024 * 1024),
        name="mixer",
    )(x, x, *consts)


def _row_span(first_row, n_rows):
    return pl.ds(pl.multiple_of(first_row * ROW_SUB, V7X_SUBLANES),
                 pl.multiple_of(n_rows * ROW_SUB, V7X_SUBLANES))


def _dispatch_kernel(run_src, run_n, run_dst, tile_rows, pad_dst, pad_n, n_used_ref, h2b_ref, pos_ref,
                     xs_hbm, stage, zero_rows, run_sem, fill_sem, *, n_blocks):
    i = pl.program_id(0)
    n_tiles = pl.num_programs(0)
    slot = i % 2
    block_rows = MOE_BLOCK * ROW_SUB

    def run_copy(tile, e, buf, live=1):
        k = tile * N_EXPERTS + e
        n = run_n[k] * live
        return pltpu.make_async_copy(stage.at[buf, _row_span(run_src[k], n)],
                                     xs_hbm.at[_row_span(run_dst[k], n)], run_sem.at[buf])

    def wait_runs(tile, buf):
        n = tile_rows[tile]
        pltpu.make_async_copy(stage.at[buf, _row_span(0, n)], xs_hbm.at[_row_span(0, n)],
                              run_sem.at[buf]).wait()

    def pad_copy(e):
        return pltpu.make_async_copy(zero_rows.at[_row_span(0, pad_n[e])],
                                     xs_hbm.at[_row_span(pad_dst[e], pad_n[e])], fill_sem)

    def idle_block_copy(blk):
        return pltpu.make_async_copy(
            zero_rows, xs_hbm.at[pl.ds(pl.multiple_of(blk * block_rows, block_rows), block_rows)], fill_sem)

    @pl.when(i == 0)
    def _():
        zero_rows[...] = jnp.zeros_like(zero_rows)

        def start_idle(blk, c):
            idle_block_copy(blk).start()
            return c

        def wait_idle(blk, c):
            idle_block_copy(blk).wait()
            return c

        for e in range(N_EXPERTS):
            pad_copy(e).start()
        lax.fori_loop(n_used_ref[0], n_blocks, start_idle, 0)
        for e in range(N_EXPERTS):
            pad_copy(e).wait()
        lax.fori_loop(n_used_ref[0], n_blocks, wait_idle, 0)

    @pl.when(i >= 2)
    def _():
        wait_runs(i - 2, slot)

    prev_live = jnp.minimum(i, 1)
    for e in range(N_EXPERTS):
        run_copy(jnp.maximum(i - 1, 0), e, 1 - slot, prev_live).start(priority=e % 2)

    h2b = h2b_ref[...]
    for a0 in range(0, N_STAGE, STAGE_CHUNK):
        a_iota = lax.broadcasted_iota(I32, (STAGE_CHUNK, MOE_TS), 0) + a0
        onehot = jnp.zeros((STAGE_CHUNK, MOE_TS), F32)
        for k in range(TOP_K):
            onehot = jnp.where(a_iota == pos_ref[k:k + 1, :], 1.0, onehot)
        _store_rows(stage.at[slot], a0, _pack_rows(_dot(onehot.astype(BF16), h2b)))

    @pl.when(i == n_tiles - 1)
    def _():
        for e in range(N_EXPERTS):
            run_copy(i, e, slot).start()

        @pl.when(i >= 1)
        def _():
            wait_runs(i - 1, 1 - slot)
        wait_runs(i, slot)


def _dispatch_call(h2b, pos8, tables, n_blocks):
    T, D = h2b.shape
    n_tiles = T // MOE_TS
    grid_spec = pltpu.PrefetchScalarGridSpec(
        num_scalar_prefetch=7,
        grid=(n_tiles,),
        in_specs=[pl.BlockSpec((MOE_TS, D), lambda i, *_: (i, 0)),
                  pl.BlockSpec((V7X_SUBLANES, MOE_TS), lambda i, *_: (0, i))],
        out_specs=pl.BlockSpec(memory_space=pl.ANY),
        scratch_shapes=[pltpu.VMEM((2, N_STAGE * ROW_SUB, V7X_LANES), U32),
                        pltpu.VMEM((MOE_BLOCK * ROW_SUB, V7X_LANES), U32),
                        pltpu.SemaphoreType.DMA((2,)),
                        pltpu.SemaphoreType.DMA(())],
    )
    return pl.pallas_call(
        functools.partial(_dispatch_kernel, n_blocks=n_blocks),
        grid_spec=grid_spec,
        out_shape=jax.ShapeDtypeStruct((n_blocks * MOE_BLOCK * ROW_SUB, V7X_LANES), U32),
        compiler_params=pltpu.CompilerParams(
            dimension_semantics=("arbitrary",),
            vmem_limit_bytes=V7X_VMEM_BYTES - 16 * 1024 * 1024),
        name="dispatch",
    )(tables["run_src"], tables["run_n"], tables["run_dst"], tables["tile_rows"], tables["pad_dst"],
      tables["pad_n"], tables["n_used"], h2b, pos8)


def _moe_kernel(be_ref, group_end_ref, n_used_ref, xs_ref, wup_hbm, wdn_hbm, *rest, layer):
    bias_refs = rest[:2 * MOE_STEP_BLOCKS]
    ys_ref, wup_f32, wdn_f32, wup_bf, wdn_bf, group_count, wsem = rest[2 * MOE_STEP_BLOCKS:]
    step = pl.program_id(0)
    n_used = n_used_ref[0]

    def weight_copies(e, buf):
        return (pltpu.make_async_copy(wup_hbm.at[layer, e], wup_f32.at[buf], wsem.at[buf, 0]),
                pltpu.make_async_copy(wdn_hbm.at[layer, e], wdn_f32.at[buf], wsem.at[buf, 1]))

    @pl.when(step == 0)
    def _():
        group_count[0] = 0
        for c in weight_copies(be_ref[0], 0):
            c.start()

    halves = []
    for half in range(MOE_STEP_BLOCKS):
        blk = step * MOE_STEP_BLOCKS + half
        expert = be_ref[blk]
        new_expert = (blk == 0) | (expert != be_ref[jnp.maximum(blk - 1, 0)])

        @pl.when((blk < n_used) & new_expert)
        def _(expert=expert):
            par = group_count[0] % 2
            group_count[0] = group_count[0] + 1
            next_blk = group_end_ref[expert]

            @pl.when(next_blk < n_used)
            def _():
                for c in weight_copies(be_ref[next_blk], 1 - par):
                    c.start()

            for c in weight_copies(expert, par):
                c.wait()
            wup_bf[par] = wup_f32[par].astype(BF16)
            wdn_bf[par] = wdn_f32[par].astype(BF16)

        halves.append((group_count[0] - 1) % 2)

    @pl.when(step * MOE_STEP_BLOCKS < n_used)
    def _():
        for half, par in enumerate(halves):
            bup_ref, bdn_ref = bias_refs[2 * half], bias_refs[2 * half + 1]
            r0 = half * MOE_BLOCK
            xs = _unpack_rows(_load_rows(xs_ref, r0, MOE_BLOCK))
            gu = _dot(xs, wup_bf[par]) + bup_ref[0, 0]
            x_glu = jnp.minimum(gu[:, :D_FF], SWIGLU_LIMIT)
            x_lin = jnp.clip(gu[:, D_FF:], -SWIGLU_LIMIT, SWIGLU_LIMIT)
            act = x_glu * _sigmoid(SWIGLU_ALPHA * x_glu) * (x_lin + 1.0)
            y = _dot(act.astype(BF16), wdn_bf[par]) + bdn_ref[0, 0]
            _store_rows(ys_ref, r0, _pack_rows(y.astype(BF16).astype(F32)))

    @pl.when(step * MOE_STEP_BLOCKS >= n_used)
    def _():
        ys_ref[...] = jnp.zeros_like(ys_ref)


def _moe_call(layer, xs, tables, w_up, b_up, w_down, b_down):
    n_blocks = tables["block_e"].shape[0]
    assert n_blocks % MOE_STEP_BLOCKS == 0
    step_rows = MOE_STEP_BLOCKS * MOE_BLOCK * ROW_SUB
    bias_spec = lambda width, half: pl.BlockSpec(
        (1, 1, 1, width), lambda s, be, *_: (layer, be[s * MOE_STEP_BLOCKS + half], 0, 0))
    grid_spec = pltpu.PrefetchScalarGridSpec(
        num_scalar_prefetch=3,
        grid=(n_blocks // MOE_STEP_BLOCKS,),
        in_specs=[
            pl.BlockSpec((step_rows, V7X_LANES), lambda s, *_: (s, 0)),
            pl.BlockSpec(memory_space=pl.ANY),
            pl.BlockSpec(memory_space=pl.ANY),
        ] + [bias_spec(width, half) for half in range(MOE_STEP_BLOCKS) for width in (2 * D_FF, D_MODEL)],
        out_specs=pl.BlockSpec((step_rows, V7X_LANES), lambda s, *_: (s, 0)),
        scratch_shapes=[pltpu.VMEM((2, D_MODEL, 2 * D_FF), F32), pltpu.VMEM((2, D_FF, D_MODEL), F32),
                        pltpu.VMEM((2, D_MODEL, 2 * D_FF), BF16), pltpu.VMEM((2, D_FF, D_MODEL), BF16),
                        pltpu.SMEM((1,), I32), pltpu.SemaphoreType.DMA((2, 2))],
    )
    return pl.pallas_call(
        functools.partial(_moe_kernel, layer=layer),
        grid_spec=grid_spec,
        out_shape=jax.ShapeDtypeStruct(xs.shape, U32),
        compiler_params=pltpu.CompilerParams(
            dimension_semantics=("arbitrary",),
            vmem_limit_bytes=V7X_VMEM_BYTES - 8 * 1024 * 1024),
        name="moe",
    )(tables["block_e"], tables["group_end"], tables["n_used"], xs, w_up, w_down,
      *([b_up, b_down] * MOE_STEP_BLOCKS))


def _combine_kernel(run_src, run_n, run_dst, tile_rows, x1_ref, pos_ref, gate_ref, p_ref, ple_g_ref,
                    gate_w_ref, proj_w_ref, fin_g_ref, ys_hbm, out_ref, stage, x2_buf, run_sem, *, last):
    i = pl.program_id(0)
    n_tiles = pl.num_programs(0) - 1
    slot = i % 2
    tile = jnp.minimum(i, n_tiles - 1)
    live = jnp.where(i < n_tiles, 1, 0)

    def run_copy(tile, e, buf, live=1):
        k = tile * N_EXPERTS + e
        n = run_n[k] * live
        return pltpu.make_async_copy(ys_hbm.at[_row_span(run_dst[k], n)],
                                     stage.at[buf, _row_span(run_src[k], n)], run_sem.at[buf])

    @pl.when(i == 0)
    def _():
        stage[...] = jnp.zeros_like(stage)
        x2_buf[...] = jnp.zeros_like(x2_buf)
        for e in range(N_EXPERTS):
            run_copy(0, e, 0).start()

    n_rows = tile_rows[tile] * live
    pltpu.make_async_copy(ys_hbm.at[_row_span(0, n_rows)], stage.at[slot, _row_span(0, n_rows)],
                          run_sem.at[slot]).wait()

    next_live = jnp.where(i + 1 < n_tiles, 1, 0)
    for e in range(N_EXPERTS):
        run_copy(jnp.minimum(i + 1, n_tiles - 1), e, 1 - slot, next_live).start(priority=e % 2)

    x2p = x2_buf[...]
    h3 = _rms_norm(x2p, ple_g_ref[...]).astype(BF16)
    g = _sigmoid(_dot(h3, gate_w_ref[...]))
    pp = _dot(p_ref[0, 0].astype(BF16), proj_w_ref[...])
    x3 = x2p + g * pp
    if last:
        x3 = _rms_norm(x3, fin_g_ref[...])
    out_ref[...] = x3

    x2 = x1_ref[...]
    for a0 in range(0, N_STAGE, STAGE_CHUNK):
        a_iota = lax.broadcasted_iota(I32, (STAGE_CHUNK, MOE_TS), 0) + a0
        weights_t = jnp.zeros((STAGE_CHUNK, MOE_TS), F32)
        for k in range(TOP_K):
            weights_t = jnp.where(a_iota == pos_ref[k:k + 1, :], gate_ref[k:k + 1, :], weights_t)
        y_sorted = _unpack_rows(_load_rows(stage.at[slot], a0, STAGE_CHUNK))
        x2 = x2 + lax.dot_general(weights_t.astype(BF16), y_sorted, (((0,), (0,)), ((), ())),
                                  preferred_element_type=F32)
    x2_buf[...] = x2


def _combine_call(layer, x1, ys, pos8, gate8, p, tables, lw, fin_g, last):
    T, D = x1.shape
    n_tiles = T // MOE_TS
    n_seq = p.shape[2] // MOE_TS
    consts = [lw["ple_norm"], lw["ple_gate_w"], lw["ple_proj_w"]]
    gathered = lambda i: jnp.minimum(i, n_tiles - 1)
    finished = lambda i: jnp.maximum(i - 1, 0)
    grid_spec = pltpu.PrefetchScalarGridSpec(
        num_scalar_prefetch=4,
        grid=(n_tiles + 1,),
        in_specs=[pl.BlockSpec((MOE_TS, D), lambda i, *_: (gathered(i), 0)),
                  pl.BlockSpec((V7X_SUBLANES, MOE_TS), lambda i, *_: (0, gathered(i))),
                  pl.BlockSpec((V7X_SUBLANES, MOE_TS), lambda i, *_: (0, gathered(i))),
                  pl.BlockSpec((1, 1, MOE_TS, PLE_DIM),
                               lambda i, *_: (layer, finished(i) // n_seq, finished(i) % n_seq, 0))]
        + [_const_spec(c.shape, layer) for c in consts] + [_const_spec(fin_g.shape, 0)]
        + [pl.BlockSpec(memory_space=pl.ANY)],
        out_specs=pl.BlockSpec((MOE_TS, D), lambda i, *_: (finished(i), 0)),
        scratch_shapes=[pltpu.VMEM((2, N_STAGE * ROW_SUB, V7X_LANES), U32),
                        pltpu.VMEM((MOE_TS, D), F32),
                        pltpu.SemaphoreType.DMA((2,))],
    )
    return pl.pallas_call(
        functools.partial(_combine_kernel, last=last),
        grid_spec=grid_spec,
        out_shape=jax.ShapeDtypeStruct((T, D), F32),
        compiler_params=pltpu.CompilerParams(
            dimension_semantics=("arbitrary",),
            vmem_limit_bytes=V7X_VMEM_BYTES - 16 * 1024 * 1024),
        name="combine",
    )(tables["run_src"], tables["run_n"], tables["run_dst"], tables["tile_rows"], x1, pos8, gate8, p,
      *consts, fin_g, ys)


def _routing_tables(tile_counts, n_blocks):
    c = tile_counts[:, :, 0].astype(I32)
    c = (c + RUN_ALIGN - 1) // RUN_ALIGN * RUN_ALIGN
    counts = jnp.sum(c, axis=0)
    padded = (counts + MOE_BLOCK - 1) // MOE_BLOCK * MOE_BLOCK
    pad_end = jnp.cumsum(padded)
    pad_start = pad_end - padded
    run_dst = pad_start[None, :] + jnp.cumsum(c, axis=0) - c
    run_src = jnp.cumsum(c, axis=1) - c
    n_used = pad_end[-1] // MOE_BLOCK
    blk0 = jnp.arange(n_blocks, dtype=I32) * MOE_BLOCK
    be = jnp.minimum(jnp.sum(blk0[:, None] >= pad_end[None, :], axis=1), N_EXPERTS - 1)
    be = be[jnp.minimum(jnp.arange(n_blocks), n_used - 1)]
    return {
        "group_end": (pad_end // MOE_BLOCK).astype(I32),
        "run_src": run_src.reshape(-1).astype(I32), "run_n": c.reshape(-1),
        "run_dst": run_dst.reshape(-1).astype(I32), "tile_rows": jnp.sum(c, axis=1).astype(I32),
        "pad_dst": (pad_start + counts).astype(I32), "pad_n": (padded - counts).astype(I32),
        "n_used": n_used.reshape(1).astype(I32), "block_e": be.astype(I32),
    }


def kernel(x, p, mix_norm, w_in, b_in, pool_w, pool_scale, conv_w, conv_b, conv_norm_g, conv_norm_b,
           sgu_norm_g, sgu_norm_b, sgu_w, sgu_b, branch_w, branch_b, w_out, moe_norm, router_w,
           router_b, expert_w_up, expert_b_up, expert_w_down, expert_b_down, ple_norm, ple_gate_w,
           ple_proj_w, final_norm):
    B, S, D = x.shape
    T = B * S
    depth = w_in.shape[0]
    assert D == D_MODEL and S % MIX_TS == 0 and w_in.shape[2] == IN_WIDTH
    max_rows = T * TOP_K + (T // MOE_TS) * N_EXPERTS * (RUN_ALIGN - 1)
    n_blocks = -(-max_rows // MOE_BLOCK) + N_EXPERTS
    n_blocks += -n_blocks % MOE_STEP_BLOCKS
    rows = lambda a: a[:, None, :]
    b_up4 = expert_b_up[:, :, None, :]
    b_down4 = expert_b_down[:, :, None, :]
    lw = {
        "mix_norm": rows(mix_norm), "w_in": w_in.astype(BF16), "b_in": rows(b_in),
        "pool_w": pool_w.astype(BF16), "pool_scale": rows(pool_scale),
        "conv_w": conv_w, "conv_b": rows(conv_b),
        "conv_norm_g": rows(conv_norm_g), "conv_norm_b": rows(conv_norm_b),
        "sgu_norm_g": rows(sgu_norm_g), "sgu_norm_b": rows(sgu_norm_b),
        "sgu_w": sgu_w, "sgu_bt": jnp.swapaxes(sgu_b, 1, 2),
        "branch_w": branch_w.astype(BF16), "branch_b": branch_b,
        "w_out": w_out.astype(BF16), "moe_norm": rows(moe_norm),
        "router_wt": jnp.swapaxes(router_w, 1, 2), "router_b": router_b[:, :, None],
        "ple_norm": rows(ple_norm), "ple_gate_w": ple_gate_w.astype(BF16),
        "ple_proj_w": ple_proj_w.astype(BF16),
    }
    fin_g = final_norm.reshape(1, 1, D)
    for i in range(depth):
        x1, h2b, pos8, gate8, tile_counts = _mixer_call(i, x, lw)
        tables = _routing_tables(tile_counts, n_blocks)
        xs = _dispatch_call(h2b, pos8, tables, n_blocks)
        ys = _moe_call(i, xs, tables, expert_w_up, b_up4, expert_w_down, b_down4)
        x = _combine_call(i, x1.reshape(T, D), ys, pos8, gate8, p, tables, lw, fin_g,
                          last=(i == depth - 1)).reshape(B, S, D)
    return x
```
